```python
import jax, jax.numpy as jnp
from jax import lax
import numpy as np

D_MODEL = 1024
BATCH = 2
SEQ = 16384
DEPTH = 4

GRID_W = 64
CTX_LEN = 256
N_MIXERS = 2
N_A_LAYERS = (DEPTH + 1) // 2
N_B_LAYERS = DEPTH // 2
EPS = 1e-6
MLSTM_HEADS = 4
MLSTM_DQK = D_MODEL // (2 * MLSTM_HEADS)
MLSTM_DV = D_MODEL // MLSTM_HEADS
MLSTM_CHUNK = 64
IGATE_SOFTCAP = 15.0
FGATE_BIAS_INIT = 3.0
MLSTM_NQ = MLSTM_HEADS * MLSTM_DQK
MLSTM_NV = MLSTM_HEADS * MLSTM_DV
MLSTM_PROJ = 2 * MLSTM_NQ + MLSTM_NV + D_MODEL + 4 * MLSTM_HEADS
CONV_W = 3
N_GROUPS = 4
EXPERTS_PER_GROUP = 8
N_EXPERTS = N_GROUPS * EXPERTS_PER_GROUP
TOP_K = 2
D_EXPERT = D_MODEL // 2
MOE_BLOCK = 256

kernel_name = 'hybrid_mlstm_shortconv_hmoe_dit'


def rmsnorm(x, g):
    xf = x.astype(jnp.float32)
    y = xf * lax.rsqrt(jnp.mean(xf * xf, axis=-1, keepdims=True) + EPS)
    return (y * g.astype(jnp.float32)).astype(x.dtype)


def modulate(x, shift, scale):
    return x * (1 + scale) + shift


def centred_conv3(z, w, axis):
    n = z.shape[axis]
    pad = [(0, 0)] * z.ndim
    pad[axis] = (1, 1)
    zp = jnp.pad(z, pad)
    left = lax.slice_in_dim(zp, 0, n, axis=axis)
    mid = lax.slice_in_dim(zp, 1, n + 1, axis=axis)
    right = lax.slice_in_dim(zp, 2, n + 2, axis=axis)
    return w[0] * left + w[1] * mid + w[2] * right


def short_conv_mixer(h, w_in, conv_w, w_out, grid):
    bg, cg, u = jnp.split(h @ w_in, 3, axis=-1)
    z = cg * u
    if grid:
        b_, t, ch = z.shape
        rows = t // GRID_W
        y = centred_conv3(z.reshape(b_, rows, GRID_W, ch), conv_w, axis=2).reshape(b_, t, ch)
    else:
        y = centred_conv3(z, conv_w, axis=1)
    return (bg * y) @ w_out


def mlstm_scan(q, k, v, li, lf, state, return_h):
    b_, nh, t, _ = q.shape
    dv = v.shape[-1]
    nc = t // MLSTM_CHUNK

    def chunks(a):
        a = a.reshape(a.shape[:2] + (nc, MLSTM_CHUNK) + a.shape[3:])
        return jnp.moveaxis(a, 2, 0)

    causal = jnp.tril(jnp.ones((MLSTM_CHUNK, MLSTM_CHUNK), dtype=bool))

    def step(carry, inp):
        c_st, n_st, m_st = carry
        qc, kc, vc, ic, fc = inp
        b = jnp.cumsum(fc, axis=-1)
        b_last = b[..., -1]
        w_end = b_last[..., None] - b + ic
        m_new = jnp.maximum(b_last + m_st, w_end.max(-1))
        a_end = jnp.exp(w_end - m_new[..., None])
        decay = jnp.exp(b_last + m_st - m_new)
        c_new = decay[..., None, None] * c_st + jnp.einsum('bhsk,bhsv->bhkv', kc * a_end[..., None], vc)
        n_new = decay[..., None] * n_st + jnp.einsum('bhsk,bhs->bhk', kc, a_end)
        if not return_h:
            return (c_new, n_new, m_new), None
        logw = jnp.where(causal, b[..., :, None] - b[..., None, :] + ic[..., None, :], -jnp.inf)
        inter = b + m_st[..., None]
        m_row = jnp.maximum(inter, logw.max(-1))
        s = jnp.einsum('bhjk,bhsk->bhjs', qc, kc) * jnp.exp(logw - m_row[..., None])
        a_inter = jnp.exp(inter - m_row)
        num = jnp.einsum('bhjs,bhsv->bhjv', s, vc) + a_inter[..., None] * jnp.einsum('bhjk,bhkv->bhjv', qc, c_st)
        den = s.sum(-1) + a_inter * jnp.einsum('bhjk,bhk->bhj', qc, n_st)
        h = num / jnp.maximum(jnp.abs(den), jnp.exp(-m_row))[..., None]
        return (c_new, n_new, m_new), h

    state, hs = lax.scan(step, state, (chunks(q), chunks(k), chunks(v), chunks(li), chunks(lf)))
    if return_h:
        hs = jnp.moveaxis(hs, 0, 2).reshape(b_, nh, t, dv)
    return hs, state


def mlstm_project(h, w_in, b_gate):
    p = (h @ w_in).astype(jnp.float32)
    b_, t, _ = p.shape
    q, k, v, o, g = jnp.split(p, [MLSTM_NQ, 2 * MLSTM_NQ, 2 * MLSTM_NQ + MLSTM_NV,
                                  2 * MLSTM_NQ + MLSTM_NV + D_MODEL], axis=-1)

    def heads(a, d):
        return a.reshape(b_, t, MLSTM_HEADS, d).transpose(0, 2, 1, 3)

    q = heads(q, MLSTM_DQK) * (MLSTM_DQK ** -0.5)
    k = heads(k, MLSTM_DQK)
    v = heads(v, MLSTM_DV)
    g = (g + b_gate.astype(jnp.float32)).reshape(b_, t, 4, MLSTM_HEADS).transpose(2, 0, 3, 1)
    li = IGATE_SOFTCAP * jnp.tanh(g[0::2] / IGATE_SOFTCAP)
    lf = jax.nn.log_sigmoid(g[1::2])
    return q, k, v, jax.nn.sigmoid(o), li, lf


def mlstm_bidir(q, k, v, li, lf, st_f0, st_b0, return_h):
    def flip(a):
        return jnp.flip(a, axis=2)
    h_f, st_f = mlstm_scan(q, k, v, li[0], lf[0], st_f0, return_h)
    h_b, st_b = mlstm_scan(flip(q), flip(k), flip(v), flip(li[1]), flip(lf[1]), st_b0, return_h)
    h = h_f + flip(h_b) if return_h else None
    return h, st_f, st_b


def mlstm_output(h, o, head_g, w_out, dtype):
    h = h * lax.rsqrt(jnp.mean(h * h, axis=-1, keepdims=True) + EPS)
    b_, nh, t, dv = h.shape
    h = h.transpose(0, 2, 1, 3).reshape(b_, t, nh * dv) * head_g.astype(jnp.float32) * o
    return h.astype(dtype) @ w_out


def mlstm_mixer(h_lat, h_ctx, w_in, b_gate, head_g, w_out, ctx_out):
    b_ = h_lat.shape[0]
    zero = (jnp.zeros((b_, MLSTM_HEADS, MLSTM_DQK, MLSTM_DV), jnp.float32),
            jnp.zeros((b_, MLSTM_HEADS, MLSTM_DQK), jnp.float32),
            jnp.zeros((b_, MLSTM_HEADS), jnp.float32))
    qc, kc, vc, oc, lic, lfc = mlstm_project(h_ctx, w_in, b_gate)
    hc, st_f, st_b = mlstm_bidir(qc, kc, vc, lic, lfc, zero, zero, ctx_out)
    ql, kl, vl, ol, lil, lfl = mlstm_project(h_lat, w_in, b_gate)
    hl, _, _ = mlstm_bidir(ql, kl, vl, lil, lfl, st_f, st_b, True)
    y_lat = mlstm_output(hl, ol, head_g, w_out, h_lat.dtype)
    y_ctx = mlstm_output(hc, oc, head_g, w_out, h_ctx.dtype) if ctx_out else None
    return y_lat, y_ctx


def hier_moe(h, w_group, b_group, w_router, b_router, w_gate, w_up, w_down):
    t, d = h.shape
    p_group = jax.nn.softmax((h @ w_group).astype(jnp.float32) + b_group.astype(jnp.float32), axis=-1)
    grp = jnp.argmax(p_group, axis=-1)
    p_grp = jnp.take_along_axis(p_group, grp[:, None], axis=1)
    logits = ((h @ w_router).astype(jnp.float32) + b_router.astype(jnp.float32)).reshape(t, N_GROUPS, EXPERTS_PER_GROUP)
    logits = jnp.take_along_axis(logits, grp[:, None, None], axis=1)[:, 0]
    top_p, top_i = lax.top_k(jax.nn.softmax(logits, axis=-1), TOP_K)
    gate = p_grp * top_p / top_p.sum(-1, keepdims=True)
    eid = grp[:, None] * EXPERTS_PER_GROUP + top_i
    n_assign = t * TOP_K
    e_flat = eid.reshape(-1)
    tok_flat = jnp.repeat(jnp.arange(t, dtype=jnp.int32), TOP_K)
    order = jnp.argsort(e_flat)
    e_s, tok_s, w_s = e_flat[order], tok_flat[order], gate.reshape(-1)[order]
    counts = jnp.bincount(e_flat, length=N_EXPERTS)
    starts = jnp.cumsum(counts) - counts
    padded = (counts + MOE_BLOCK - 1) // MOE_BLOCK * MOE_BLOCK
    pends = jnp.cumsum(padded)
    pstarts = pends - padded
    dest = pstarts[e_s] + jnp.arange(n_assign) - starts[e_s]
    n_blocks = -(-n_assign // MOE_BLOCK) + N_EXPERTS
    slot_tok = jnp.full((n_blocks * MOE_BLOCK,), t, jnp.int32).at[dest].set(tok_s)
    slot_w = jnp.zeros((n_blocks * MOE_BLOCK,), h.dtype).at[dest].set(w_s.astype(h.dtype))
    blk_e = jnp.minimum(jnp.searchsorted(pends, jnp.arange(n_blocks) * MOE_BLOCK, side='right'), N_EXPERTS - 1)
    xb = jnp.concatenate([h, jnp.zeros((1, d), h.dtype)], axis=0)[slot_tok].reshape(n_blocks, MOE_BLOCK, d)

    def expert_block(args):
        xblk, e = args
        return (jax.nn.silu(xblk @ w_gate[e]) * (xblk @ w_up[e])) @ w_down[e]

    yb = lax.map(expert_block, (xb, blk_e)).reshape(-1, d) * slot_w[:, None]
    return jax.ops.segment_sum(yb, slot_tok, num_segments=t + 1)[:t]


def setup_inputs(seed: int = 0) -> dict:
    key = jax.random.key(seed)
    ks = jax.random.split(key, 22)
    f32 = jnp.float32

    def nrm(k, shape, scale):
        return scale * jax.random.normal(k, shape, f32)

    d = D_MODEL
    gate_idx = jnp.arange(4 * MLSTM_HEADS) // MLSTM_HEADS
    fgate_offset = jnp.where(gate_idx % 2 == 1, FGATE_BIAS_INIT, 0.0).astype(f32)
    return {
        'x': nrm(ks[0], (BATCH, SEQ, d), 1.0),
        'c': nrm(ks[1], (BATCH, d), 1.0),
        'ctx': nrm(ks[2], (BATCH, CTX_LEN, d), 1.0),
        'c_ctx': nrm(ks[3], (d,), 1.0),
        'mod_w': nrm(ks[4], (DEPTH, d, 6 * d), 0.5 * d ** -0.5),
        'mod_b': nrm(ks[5], (DEPTH, 6 * d), 0.01),
        'norm_g': 1.0 + nrm(ks[6], (DEPTH, 2, d), 0.02),
        'final_g': 1.0 + nrm(ks[7], (d,), 0.02),
        'a_w_in': nrm(ks[8], (N_A_LAYERS, d, MLSTM_PROJ), d ** -0.5),
        'a_b_gate': nrm(ks[9], (N_A_LAYERS, 4 * MLSTM_HEADS), 0.1) + fgate_offset,
        'a_head_g': 1.0 + nrm(ks[10], (N_A_LAYERS, MLSTM_NV), 0.02),
        'a_w_out': nrm(ks[11], (N_A_LAYERS, MLSTM_NV, d), MLSTM_NV ** -0.5),
        'b_w_in': nrm(ks[12], (N_B_LAYERS, d, 3 * d), d ** -0.5),
        'b_conv_w': nrm(ks[13], (N_B_LAYERS, CONV_W, d), CONV_W ** -0.5),
        'b_w_out': nrm(ks[14], (N_B_LAYERS, d, d), d ** -0.5),
        'moe_w_group': nrm(ks[15], (DEPTH, d, N_GROUPS), d ** -0.5),
        'moe_b_group': nrm(ks[16], (DEPTH, N_GROUPS), 0.01),
        'moe_w_router': nrm(ks[17], (DEPTH, d, N_EXPERTS), d ** -0.5),
        'moe_b_router': nrm(ks[18], (DEPTH, N_EXPERTS), 0.01),
        'moe_w_gate': nrm(ks[19], (DEPTH, N_EXPERTS, d, D_EXPERT), d ** -0.5),
        'moe_w_up': nrm(ks[20], (DEPTH, N_EXPERTS, d, D_EXPERT), d ** -0.5),
        'moe_w_down': nrm(ks[21], (DEPTH, N_EXPERTS, D_EXPERT, d), D_EXPERT ** -0.5),
    }


def reference(x, c, ctx, c_ctx, mod_w, mod_b, norm_g, final_g, a_w_in, a_b_gate, a_head_g, a_w_out,
              b_w_in, b_conv_w, b_w_out, moe_w_group, moe_b_group, moe_w_router, moe_b_router,
              moe_w_gate, moe_w_up, moe_w_down):
    silu_c = jax.nn.silu(c)
    silu_cc = jax.nn.silu(c_ctx)
    lat, cx = x, ctx
    b_, s_, d = x.shape
    for l in range(DEPTH):
        kind, j = l % N_MIXERS, l // N_MIXERS
        ctx_after = any(i % N_MIXERS == 0 for i in range(l + 1, DEPTH))
        ctx_here = kind == 0 or ctx_after
        mod = (silu_c @ mod_w[l] + mod_b[l])[:, None, :]
        sh1, sc1, g1, sh2, sc2, g2 = jnp.split(mod, 6, axis=-1)
        hl = modulate(rmsnorm(lat, norm_g[l, 0]), sh1, sc1)
        if ctx_here:
            mc = silu_cc @ mod_w[l] + mod_b[l]
            csh1, csc1, cg1, csh2, csc2, cg2 = jnp.split(mc, 6, axis=-1)
            hc = modulate(rmsnorm(cx, norm_g[l, 0]), csh1, csc1)
        if kind == 0:
            yl, yc = mlstm_mixer(hl, hc, a_w_in[j], a_b_gate[j], a_head_g[j], a_w_out[j], ctx_after)
        else:
            yl = short_conv_mixer(hl, b_w_in[j], b_conv_w[j], b_w_out[j], grid=True)
            yc = short_conv_mixer(hc, b_w_in[j], b_conv_w[j], b_w_out[j], grid=False) if ctx_after else None
        lat = lat + g1 * yl
        hl2 = modulate(rmsnorm(lat, norm_g[l, 1]), sh2, sc2)
        moe_args = (moe_w_group[l], moe_b_group[l], moe_w_router[l], moe_b_router[l],
                    moe_w_gate[l], moe_w_up[l], moe_w_down[l])
        if ctx_after:
            cx = cx + cg1 * yc
            hc2 = modulate(rmsnorm(cx, norm_g[l, 1]), csh2, csc2)
            n_ctx = hc2.shape[1]
            y = hier_moe(jnp.concatenate([hl2.reshape(-1, d), hc2.reshape(-1, d)], axis=0), *moe_args)
            lat = lat + g2 * y[:b_ * s_].reshape(b_, s_, d)
            cx = cx + cg2 * y[b_ * s_:].reshape(b_, n_ctx, d)
        else:
            lat = lat + g2 * hier_moe(hl2.reshape(-1, d), *moe_args).reshape(b_, s_, d)
    return rmsnorm(lat, final_g)
```

```python
import functools

import jax
import jax.numpy as jnp
from jax import lax
from jax.experimental import pallas as pl
from jax.experimental.pallas import tpu as pltpu

F32 = jnp.float32
BF16 = jnp.bfloat16

EPS = 1e-6
HEADS = 4
DQK = 128
DV = 256
NQ = HEADS * DQK
NV = HEADS * DV
IGATE_SOFTCAP = 15.0
GRID_W = 64
N_GROUPS = 4
EPG = 8
N_EXPERTS = N_GROUPS * EPG
ROUTE_ROWS = 40
MOE_BLOCK = 256
MLSTM_CHUNK = 128
VMEM_LIMIT = 52 * 1024 * 1024


def _cparams(*sem):
    return pltpu.CompilerParams(dimension_semantics=sem, vmem_limit_bytes=VMEM_LIMIT)


def _nt_dot(a, b):
    return lax.dot_general(a, b, (((1,), (1,)), ((), ())), preferred_element_type=F32)


def _rms(x, g):
    return x * lax.rsqrt(jnp.mean(x * x, axis=-1, keepdims=True) + EPS) * g


def _mod_kernel(c_ref, w_ref, b_ref, o_ref):
    c = c_ref[...]
    s = (c * jax.nn.sigmoid(c)).astype(BF16)
    o_ref[0] = jnp.dot(s, w_ref[0].astype(BF16), preferred_element_type=F32) + b_ref[0]


def _modulation(cond, mod_w, mod_b):
    depth, d, n = mod_w.shape
    tn = 1024
    return pl.pallas_call(
        _mod_kernel,
        grid=(depth, n // tn),
        in_specs=[pl.BlockSpec((8, d), lambda l, j: (0, 0)),
                  pl.BlockSpec((1, d, tn), lambda l, j: (l, 0, j)),
                  pl.BlockSpec((1, 1, tn), lambda l, j: (l, 0, j))],
        out_specs=pl.BlockSpec((1, 8, tn), lambda l, j: (l, 0, j)),
        out_shape=jax.ShapeDtypeStruct((depth, 8, n), F32),
        compiler_params=_cparams("arbitrary", "arbitrary"),
    )(cond, mod_w, mod_b.reshape(depth, 1, n))


def _residual_in(x_ref, comb_refs, mv):
    x = x_ref[...]
    if comb_refs:
        ya_ref, yb_ref = comb_refs
        x = x + mv[6:7] * (ya_ref[...].astype(F32) + yb_ref[...].astype(F32))
    return x


def _route(x2, wr_hi_ref, wr_lo_ref, rb_ref, ids_ref, gts_ref):
    x_hi = x2.astype(BF16)
    x_lo = (x2 - x_hi.astype(F32)).astype(BF16)
    w_hi = wr_hi_ref[...]
    lg = _nt_dot(w_hi, x_hi) + _nt_dot(wr_lo_ref[...], x_hi) + _nt_dot(w_hi, x_lo)
    lg = lg + rb_ref[...]
    row = lax.broadcasted_iota(jnp.int32, (EPG, lg.shape[1]), 0)
    gl = lg[N_EXPERTS:N_EXPERTS + EPG]
    gl = jnp.where(row < N_GROUPS, gl, -jnp.inf)
    gmx = jnp.max(gl, axis=0, keepdims=True)
    grp = jnp.min(jnp.where(gl == gmx, row, EPG), axis=0, keepdims=True)
    p_grp = 1.0 / jnp.sum(jnp.exp(gl - gmx), axis=0, keepdims=True)
    sel = lg[0:EPG]
    for g in range(1, N_GROUPS):
        sel = jnp.where(grp == g, lg[g * EPG:(g + 1) * EPG], sel)
    mx1 = jnp.max(sel, axis=0, keepdims=True)
    i1 = jnp.min(jnp.where(sel == mx1, row, EPG), axis=0, keepdims=True)
    rest = jnp.where(row == i1, -jnp.inf, sel)
    mx2 = jnp.max(rest, axis=0, keepdims=True)
    i2 = jnp.min(jnp.where(rest == mx2, row, EPG), axis=0, keepdims=True)
    e2 = jnp.exp(mx2 - mx1)
    inv = p_grp / (1.0 + e2)
    ids_ref[0:1, :] = grp * EPG + i1
    ids_ref[1:2, :] = grp * EPG + i2
    gts_ref[0:1, :] = inv
    gts_ref[1:2, :] = inv * e2


def _norm2_and_route(x, mv, ng_ref, route_refs, h2_ref, ids_ref, gts_ref):
    x2 = _rms(x, ng_ref[1:2]) * (1.0 + mv[4:5]) + mv[3:4]
    h2_ref[...] = x2.astype(BF16)
    _route(x2, *route_refs, ids_ref, gts_ref)


def _tile_specs(tm, d, n_comb, tiles_per_batch):
    tok = pl.BlockSpec((tm, d), lambda i: (i, 0))
    specs = [tok] + [tok] * n_comb
    specs.append(pl.BlockSpec((None, 8, d), lambda i: (i // tiles_per_batch, 0, 0)))
    specs.append(pl.BlockSpec((2, d), lambda i: (0, 0)))
    return specs


def _full(shape):
    return pl.BlockSpec(shape, lambda i: (0,) * len(shape))


def _route_specs(d):
    return [_full((ROUTE_ROWS, d)), _full((ROUTE_ROWS, d)), _full((ROUTE_ROWS, 1))]


def _in_a_kernel(*refs, has_comb):
    if has_comb:
        x_ref, ya_ref, yb_ref = refs[:3]
        comb, refs = (ya_ref, yb_ref), refs[3:]
    else:
        x_ref, comb, refs = refs[0], (), refs[1:]
    mv_ref, ng_ref, wq_ref, wkt_ref, wv_ref, wo_ref, wgt_ref, bg_ref = refs[:8]
    outs = refs[8:]
    if has_comb:
        xo_ref, outs = outs[0], outs[1:]
    q_ref, kt_ref, v_ref, o_ref, a_ref = outs
    mv = mv_ref[...]
    x = _residual_in(x_ref, comb, mv)
    if has_comb:
        xo_ref[...] = x
    h = (_rms(x, ng_ref[0:1]) * (1.0 + mv[1:2]) + mv[0:1]).astype(BF16)
    q_ref[...] = (jnp.dot(h, wq_ref[...], preferred_element_type=F32) * (DQK ** -0.5)).astype(BF16)
    kt_ref[0] = _nt_dot(wkt_ref[...], h).astype(BF16)
    v_ref[...] = jnp.dot(h, wv_ref[...], preferred_element_type=F32).astype(BF16)
    o_ref[...] = jax.nn.sigmoid(jnp.dot(h, wo_ref[...], preferred_element_type=F32)).astype(BF16)
    g = _nt_dot(wgt_ref[...], h) + bg_ref[...]
    gi, gf = g[0:8], g[8:16]
    a_ref[0, 0:8, :] = IGATE_SOFTCAP * jnp.tanh(gi / IGATE_SOFTCAP)
    a_ref[0, 8:16, :] = jnp.minimum(gf, 0.0) - jnp.log(1.0 + jnp.exp(-jnp.abs(gf)))


def _in_a(x, comb, modv, ng, w, batch, tm):
    t, d = x.shape
    s = t // batch
    tpb = s // tm
    has_comb = bool(comb)
    wq, wkt, wv, wo, wgt, bg = w
    in_specs = _tile_specs(tm, d, len(comb), tpb) + [
        _full(wq.shape), _full(wkt.shape), _full(wv.shape), _full(wo.shape), _full(wgt.shape),
        _full(bg.shape)]
    tok = lambda n: pl.BlockSpec((tm, n), lambda i: (i, 0))
    rows = lambda n: pl.BlockSpec((1, n, tm), lambda i: (i // tpb, 0, i % tpb))
    out_specs = [tok(NQ), rows(NQ), tok(NV), tok(d), rows(16)]
    out_shape = [jax.ShapeDtypeStruct((t, NQ), BF16), jax.ShapeDtypeStruct((batch, NQ, s), BF16),
                 jax.ShapeDtypeStruct((t, NV), BF16), jax.ShapeDtypeStruct((t, d), BF16),
                 jax.ShapeDtypeStruct((batch, 16, s), F32)]
    if has_comb:
        out_specs = [tok(d)] + out_specs
        out_shape = [jax.ShapeDtypeStruct((t, d), F32)] + out_shape
    outs = pl.pallas_call(
        functools.partial(_in_a_kernel, has_comb=has_comb),
        grid=(t // tm,), in_specs=in_specs, out_specs=out_specs, out_shape=out_shape,
        compiler_params=_cparams("parallel"),
    )(x, *comb, modv, ng, wq, wkt, wv, wo, wgt, bg)
    if has_comb:
        return outs[0], outs[1:]
    return x, outs


def _chunk_scan(x, op, fill, forward):
    length = x.shape[-1]
    pos = lax.broadcasted_iota(jnp.int32, x.shape, 1)
    s = 1
    while s < length:
        if forward:
            x = op(x, jnp.where(pos >= s, pltpu.roll(x, s, axis=1), fill))
        else:
            x = op(x, jnp.where(pos < length - s, pltpu.roll(x, length - s, axis=1), fill))
        s *= 2
    return x


def _gate_kernel(a_ref, u_ref, b_ref, cm_ref):
    for r in range(2 * HEADS):
        fwd = r < HEADS
        b = _chunk_scan(a_ref[0, 8 + r], jnp.add, 0.0, fwd)
        u = a_ref[0, r] - b
        u_ref[0, r] = u
        b_ref[0, r] = b
        cm_ref[0, r] = _chunk_scan(u, jnp.maximum, -jnp.inf, fwd)


def _gate_scans(act, chunk):
    batch, _, s = act.shape
    nc = s // chunk
    a4 = act.reshape(batch, 16, nc, chunk)
    spec = pl.BlockSpec((1, 8, nc, chunk), lambda b: (b, 0, 0, 0))
    shp = jax.ShapeDtypeStruct((batch, 8, nc, chunk), F32)
    u, b, cm = pl.pallas_call(
        _gate_kernel, grid=(batch,),
        in_specs=[pl.BlockSpec((1, 16, nc, chunk), lambda b: (b, 0, 0, 0))],
        out_specs=[spec, spec, spec], out_shape=[shp, shp, shp],
        compiler_params=_cparams("parallel"),
    )(a4)
    rows = jnp.transpose(u, (0, 2, 1, 3))
    cols = jnp.transpose(jnp.concatenate([b, cm], axis=1), (0, 2, 3, 1))
    return rows, cols


def _mlstm_kernel(qf_ref, kf_ref, vf_ref, rf_ref, cf_ref, qb_ref, kb_ref, vb_ref, rb_ref, cb_ref,
                  c0_ref, n0_ref, m0_ref, hf_ref, hb_ref, ct_ref, nt_ref, mt_ref,
                  c_s, n_s, m_s, *, chunk):
    step = pl.program_id(1)
    last = pl.num_programs(1) - 1

    @pl.when(step == 0)
    def _():
        c_s[...] = c0_ref[0]
        n_s[...] = n0_ref[0]
        m_s[...] = m0_ref[0]

    ri = lax.broadcasted_iota(jnp.int32, (chunk, chunk), 0)
    ci = lax.broadcasted_iota(jnp.int32, (chunk, chunk), 1)
    for d, (q_ref, k_ref, v_ref, r_ref, c_ref, h_ref) in enumerate(
            ((qf_ref, kf_ref, vf_ref, rf_ref, cf_ref, hf_ref),
             (qb_ref, kb_ref, vb_ref, rb_ref, cb_ref, hb_ref))):
        mask = (ci <= ri) if d == 0 else (ci >= ri)
        end = chunk - 1 if d == 0 else 0
        for h in range(HEADS):
            idx = d * HEADS + h
            q = q_ref[:, h * DQK:(h + 1) * DQK]
            kt = k_ref[0, h * DQK:(h + 1) * DQK, :]
            v = v_ref[:, h * DV:(h + 1) * DV]
            u = r_ref[0, 0, idx:idx + 1, :]
            b = c_ref[0, 0, :, idx:idx + 1]
            cm = c_ref[0, 0, :, 8 + idx:9 + idx]
            c_st = c_s[idx]
            n_st = n_s[idx]
            m_st = m_s[idx]
            mm = jnp.maximum(m_st, cm)
            dmat = jnp.where(mask, jnp.exp(u - mm), 0.0)
            n_b = jnp.broadcast_to(n_st, (DQK, DQK)).astype(BF16)
            sx = jnp.dot(q, jnp.concatenate([kt, n_b], axis=1), preferred_element_type=F32)
            p = sx[:, :chunk] * dmat
            a_int = jnp.exp(m_st - mm)
            num = (jnp.dot(p.astype(BF16), v, preferred_element_type=F32)
                   + a_int * jnp.dot(q, c_st.astype(BF16), preferred_element_type=F32))
            den = jnp.sum(p, axis=-1, keepdims=True) + a_int * sx[:, chunk:chunk + 1]
            floor = jnp.exp(-b - mm)
            h_ref[:, h * DV:(h + 1) * DV] = num / jnp.maximum(jnp.abs(den), floor)
            mm_end = mm[end:end + 1]
            a_end = jnp.exp(u - mm_end)
            decay = jnp.exp(m_st - mm_end)
            kta = kt.astype(F32) * a_end
            c_s[idx] = decay * c_st + jnp.dot(kta.astype(BF16), v, preferred_element_type=F32)
            n_s[idx] = decay * n_st + jnp.sum(kta, axis=-1, keepdims=True)
            m_s[idx] = b[end:end + 1] + mm_end

    @pl.when(step == last)
    def _():
        ct_ref[0] = c_s[...]
        nt_ref[0] = n_s[...]
        mt_ref[0] = m_s[...]


def _mlstm(q, kt, v, rows, cols, state, batch, chunk):
    t = q.shape[0]
    s = t // batch
    nc = s // chunk
    c0, n0, m0 = state
    fwd = lambda b, c: b * nc + c
    bwd = lambda b, c: b * nc + nc - 1 - c

    def specs(cidx, tidx):
        return [pl.BlockSpec((chunk, NQ), lambda b, c: (tidx(b, c), 0)),
                pl.BlockSpec((1, NQ, chunk), lambda b, c: (b, 0, cidx(c))),
                pl.BlockSpec((chunk, NV), lambda b, c: (tidx(b, c), 0)),
                pl.BlockSpec((1, 1, 8, chunk), lambda b, c: (b, cidx(c), 0, 0)),
                pl.BlockSpec((1, 1, chunk, 16), lambda b, c: (b, cidx(c), 0, 0))]

    st_specs = [pl.BlockSpec((1, 8, DQK, DV), lambda b, c: (b, 0, 0, 0)),
                pl.BlockSpec((1, 8, DQK, 1), lambda b, c: (b, 0, 0, 0)),
                pl.BlockSpec((1, 8, 1, 1), lambda b, c: (b, 0, 0, 0))]
    st_shape = [jax.ShapeDtypeStruct((batch, 8, DQK, DV), F32),
                jax.ShapeDtypeStruct((batch, 8, DQK, 1), F32),
                jax.ShapeDtypeStruct((batch, 8, 1, 1), F32)]
    hf, hb, ct, nt, mt = pl.pallas_call(
        functools.partial(_mlstm_kernel, chunk=chunk),
        grid=(batch, nc),
        in_specs=specs(lambda c: c, fwd) + specs(lambda c: nc - 1 - c, bwd) + st_specs,
        out_specs=[pl.BlockSpec((chunk, NV), lambda b, c: (fwd(b, c), 0)),
                   pl.BlockSpec((chunk, NV), lambda b, c: (bwd(b, c), 0))] + st_specs,
        out_shape=[jax.ShapeDtypeStruct((t, NV), F32), jax.ShapeDtypeStruct((t, NV), F32)] + st_shape,
        scratch_shapes=[pltpu.VMEM((8, DQK, DV), F32), pltpu.VMEM((8, DQK, 1), F32),
                        pltpu.VMEM((8, 1, 1), F32)],
        compiler_params=_cparams("arbitrary", "arbitrary"),
    )(q, kt, v, rows, cols, q, kt, v, rows, cols, c0, n0, m0)
    return hf, hb, (ct, nt, mt)


def _out_a_kernel(hf_ref, hb_ref, o_ref, x_ref, mv_ref, ng_ref, hg_ref, wout_ref,
                  wr_hi_ref, wr_lo_ref, rb_ref, xo_ref, h2_ref, ids_ref, gts_ref):
    mv = mv_ref[...]
    hsum = hf_ref[...] + hb_ref[...]
    parts = []
    for h in range(HEADS):
        hh = hsum[:, h * DV:(h + 1) * DV]
        parts.append(hh * lax.rsqrt(jnp.mean(hh * hh, axis=-1, keepdims=True) + EPS))
    hn = jnp.concatenate(parts, axis=1) * hg_ref[...] * o_ref[...].astype(F32)
    y = jnp.dot(hn.astype(BF16), wout_ref[...], preferred_element_type=F32)
    x = x_ref[...] + mv[2:3] * y
    xo_ref[...] = x
    _norm2_and_route(x, mv, ng_ref, (wr_hi_ref, wr_lo_ref, rb_ref), h2_ref, ids_ref, gts_ref)


def _out_a(hf, hb, o, x, modv, ng, head_g, w_out, route_w, batch, tm):
    t, d = x.shape
    tpb = t // batch // tm
    tok = lambda n: pl.BlockSpec((tm, n), lambda i: (i, 0))
    in_specs = ([tok(NV), tok(NV), tok(d)] + _tile_specs(tm, d, 0, tpb)
                + [_full((1, NV)), _full(w_out.shape)] + _route_specs(d))
    lanes = pl.BlockSpec((2, tm), lambda i: (0, i))
    return pl.pallas_call(
        _out_a_kernel, grid=(t // tm,), in_specs=in_specs,
        out_specs=[tok(d), tok(d), lanes, lanes],
        out_shape=[jax.ShapeDtypeStruct((t, d), F32), jax.ShapeDtypeStruct((t, d), BF16),
                   jax.ShapeDtypeStruct((2, t), jnp.int32), jax.ShapeDtypeStruct((2, t), F32)],
        compiler_params=_cparams("parallel"),
    )(hf, hb, o, x, modv, ng, head_g, w_out, *route_w)


def _conv_kernel(*refs, has_comb, row_w, with_moe):
    if has_comb:
        x_ref, ya_ref, yb_ref = refs[:3]
        comb, refs = (ya_ref, yb_ref), refs[3:]
    else:
        x_ref, comb, refs = refs[0], (), refs[1:]
    mv_ref, ng_ref, win_ref, cw_ref, wout_ref = refs[:5]
    refs = refs[5:]
    mv = mv_ref[...]
    x = _residual_in(x_ref, comb, mv)
    d = x.shape[1]
    h = (_rms(x, ng_ref[0:1]) * (1.0 + mv[1:2]) + mv[0:1]).astype(BF16)
    p = jnp.dot(h, win_ref[...], preferred_element_type=F32)
    bg = p[:, :d]
    z = p[:, d:2 * d] * p[:, 2 * d:]
    tm = z.shape[0]
    pos = lax.broadcasted_iota(jnp.int32, z.shape, 0) % row_w
    left = jnp.where(pos == 0, 0.0, pltpu.roll(z, 1, axis=0))
    right = jnp.where(pos == row_w - 1, 0.0, pltpu.roll(z, tm - 1, axis=0))
    cw = cw_ref[...]
    y = cw[0:1] * left + cw[1:2] * z + cw[2:3] * right
    yl = jnp.dot((bg * y).astype(BF16), wout_ref[...], preferred_element_type=F32)
    x = x + mv[2:3] * yl
    if with_moe:
        wr_hi_ref, wr_lo_ref, rb_ref, xo_ref, h2_ref, ids_ref, gts_ref = refs
        xo_ref[...] = x
        _norm2_and_route(x, mv, ng_ref, (wr_hi_ref, wr_lo_ref, rb_ref), h2_ref, ids_ref, gts_ref)
    else:
        refs[0][...] = x


def _conv_layer(x, comb, modv, ng, w_in, conv_w, w_out, route_w, batch, tm, row_w):
    t, d = x.shape
    tpb = t // batch // tm
    tok = lambda n: pl.BlockSpec((tm, n), lambda i: (i, 0))
    in_specs = (_tile_specs(tm, d, len(comb), tpb)
                + [_full(w_in.shape), _full(conv_w.shape), _full(w_out.shape)] + _route_specs(d))
    lanes = pl.BlockSpec((2, tm), lambda i: (0, i))
    return pl.pallas_call(
        functools.partial(_conv_kernel, has_comb=bool(comb), row_w=row_w, with_moe=True),
        grid=(t // tm,), in_specs=in_specs,
        out_specs=[tok(d), tok(d), lanes, lanes],
        out_shape=[jax.ShapeDtypeStruct((t, d), F32), jax.ShapeDtypeStruct((t, d), BF16),
                   jax.ShapeDtypeStruct((2, t), jnp.int32), jax.ShapeDtypeStruct((2, t), F32)],
        compiler_params=_cparams("parallel"),
    )(x, *comb, modv, ng, w_in, conv_w, w_out, *route_w)


def _moe_kernel(be_ref, x_ref, wg_ref, wu_ref, wd_ref, sw_ref, y_ref, wg_s, wu_s, wd_s):
    i = pl.program_id(0)
    prev = be_ref[jnp.maximum(i - 1, 0)]

    @pl.when((i == 0) | (be_ref[i] != prev))
    def _():
        wg_s[...] = wg_ref[0].astype(BF16)
        wu_s[...] = wu_ref[0].astype(BF16)
        wd_s[...] = wd_ref[0].astype(BF16)

    x = x_ref[...]
    g = jnp.dot(x, wg_s[...], preferred_element_type=F32)
    u = jnp.dot(x, wu_s[...], preferred_element_type=F32)
    a = (g * jax.nn.sigmoid(g) * u).astype(BF16)
    y_ref[...] = (jnp.dot(a, wd_s[...], preferred_element_type=F32) * sw_ref[...]).astype(BF16)


def _moe_experts(xb, blk_e, slot_w, w_gate, w_up, w_down):
    n_slots, d = xb.shape
    de = w_gate.shape[-1]
    n_blocks = n_slots // MOE_BLOCK
    grid_spec = pltpu.PrefetchScalarGridSpec(
        num_scalar_prefetch=1, grid=(n_blocks,),
        in_specs=[pl.BlockSpec((MOE_BLOCK, d), lambda i, be: (i, 0)),
                  pl.BlockSpec((1, d, de), lambda i, be: (be[i], 0, 0)),
                  pl.BlockSpec((1, d, de), lambda i, be: (be[i], 0, 0)),
                  pl.BlockSpec((1, de, d), lambda i, be: (be[i], 0, 0)),
                  pl.BlockSpec((MOE_BLOCK, 1), lambda i, be: (i, 0))],
        out_specs=pl.BlockSpec((MOE_BLOCK, d), lambda i, be: (i, 0)),
        scratch_shapes=[pltpu.VMEM((d, de), BF16), pltpu.VMEM((d, de), BF16),
                        pltpu.VMEM((de, d), BF16)])
    return pl.pallas_call(
        _moe_kernel, grid_spec=grid_spec,
        out_shape=jax.ShapeDtypeStruct((n_slots, d), BF16),
        compiler_params=_cparams("arbitrary"),
    )(blk_e, xb, w_gate, w_up, w_down, slot_w)


def _moe(h2, ids, gts, w_gate, w_up, w_down):
    t = h2.shape[0]
    n_assign = 2 * t
    e_flat = ids.reshape(-1)
    onehot = (e_flat[:, None] == jnp.arange(N_EXPERTS, dtype=jnp.int32)[None, :]).astype(jnp.int32)
    csum = jnp.cumsum(onehot, axis=0)
    rank = jnp.take_along_axis(csum, e_flat[:, None], axis=1)[:, 0] - 1
    counts = csum[-1]
    padded = (counts + MOE_BLOCK - 1) // MOE_BLOCK * MOE_BLOCK
    pends = jnp.cumsum(padded)
    dest = (pends - padded)[e_flat] + rank
    n_blocks = -(-n_assign // MOE_BLOCK) + N_EXPERTS
    n_slots = n_blocks * MOE_BLOCK
    tok = jnp.tile(jnp.arange(t, dtype=jnp.int32), 2)
    slot_tok = jnp.zeros((n_slots,), jnp.int32).at[dest].set(tok)
    slot_w = jnp.zeros((n_slots,), F32).at[dest].set(gts.reshape(-1))
    blk_e = jnp.minimum(jnp.searchsorted(pends, jnp.arange(n_blocks, dtype=jnp.int32) * MOE_BLOCK,
                                         side='right'), N_EXPERTS - 1).astype(jnp.int32)
    xb = jnp.take(h2, slot_tok, axis=0)
    yb = _moe_experts(xb, blk_e, slot_w[:, None], w_gate, w_up, w_down)
    return jnp.take(yb, dest[:t], axis=0), jnp.take(yb, dest[t:], axis=0)


def _final_kernel(x_ref, ya_ref, yb_ref, mv_ref, g_ref, o_ref):
    x = _residual_in(x_ref, (ya_ref, yb_ref), mv_ref[...])
    o_ref[...] = _rms(x, g_ref[...])


def _final(x, comb, modv, g, batch, tm):
    t, d = x.shape
    tpb = t // batch // tm
    tok = pl.BlockSpec((tm, d), lambda i: (i, 0))
    return pl.pallas_call(
        _final_kernel, grid=(t // tm,),
        in_specs=[tok, tok, tok, pl.BlockSpec((None, 8, d), lambda i: (i // tpb, 0, 0)), _full((1, d))],
        out_specs=tok, out_shape=jax.ShapeDtypeStruct((t, d), F32),
        compiler_params=_cparams("parallel"),
    )(x, *comb, modv, g)


def _prep_a(w_in, b_gate):
    d = w_in.shape[0]
    wq = w_in[:, :NQ].astype(BF16)
    wkt = w_in[:, NQ:2 * NQ].T.astype(BF16)
    wv = w_in[:, 2 * NQ:2 * NQ + NV].astype(BF16)
    wo = w_in[:, 2 * NQ + NV:2 * NQ + NV + d].astype(BF16)
    perm = jnp.array([0, 1, 2, 3, 8, 9, 10, 11, 4, 5, 6, 7, 12, 13, 14, 15], jnp.int32)
    wgt = w_in[:, 2 * NQ + NV + d:].T[perm].astype(BF16)
    bg = b_gate.astype(F32)[perm][:, None]
    return wq, wkt, wv, wo, wgt, bg


def _prep_route(w_group, b_group, w_router, b_router):
    d = w_group.shape[0]
    pad = ROUTE_ROWS - N_EXPERTS - N_GROUPS
    wt = jnp.concatenate([w_router.T, w_group.T, jnp.zeros((pad, d), F32)], axis=0).astype(F32)
    hi = wt.astype(BF16)
    lo = (wt - hi.astype(F32)).astype(BF16)
    rb = jnp.concatenate([b_router, b_group, jnp.zeros((pad,), F32)]).astype(F32)[:, None]
    return hi, lo, rb


def _modv(mod, l, row, batch):
    d = mod.shape[-1] // 6
    rows = [row] * batch if isinstance(row, int) else row
    cur = jnp.stack([mod[l, r].reshape(6, d) for r in rows])
    prev = jnp.stack([mod[l - 1, r].reshape(6, d)[5:6] for r in rows]) if l > 0 else jnp.zeros((batch, 1, d), F32)
    return jnp.concatenate([cur, prev, jnp.zeros((batch, 1, d), F32)], axis=1)


def _modv_prev_only(mod, l, batch):
    d = mod.shape[-1] // 6
    prev = jnp.stack([mod[l - 1, r].reshape(6, d)[5:6] for r in range(batch)])
    return jnp.concatenate([jnp.zeros((batch, 6, d), F32), prev, jnp.zeros((batch, 1, d), F32)], axis=1)


def kernel(x, c, ctx, c_ctx, mod_w, mod_b, norm_g, final_g, a_w_in, a_b_gate, a_head_g, a_w_out,
           b_w_in, b_conv_w, b_w_out, moe_w_group, moe_b_group, moe_w_router, moe_b_router,
           moe_w_gate, moe_w_up, moe_w_down):
    batch, seq, d = x.shape
    n_ctx = ctx.shape[1]
    depth = mod_w.shape[0]
    assert batch + 1 <= 8 and seq % MLSTM_CHUNK == 0 and n_ctx % MLSTM_CHUNK == 0
    tm = min(512, seq)
    t_lat = batch * seq

    cond = jnp.concatenate([c, c_ctx[None, :], jnp.zeros((8 - batch - 1, d), F32)], axis=0)
    mod = _modulation(cond, mod_w, mod_b)

    lat = x.reshape(t_lat, d)
    cx = ctx.reshape(batch * n_ctx, d)
    comb_lat, comb_ctx = (), ()
    for l in range(depth):
        kind, j = l % 2, l // 2
        ctx_after = any(i % 2 == 0 for i in range(l + 1, depth))
        ctx_here = kind == 0 or ctx_after
        mv_lat = _modv(mod, l, list(range(batch)), batch)
        mv_ctx = _modv(mod, l, batch, batch)
        route_w = _prep_route(moe_w_group[l], moe_b_group[l], moe_w_router[l], moe_b_router[l])
        if kind == 0:
            wa = _prep_a(a_w_in[j], a_b_gate[j])
            state = (jnp.zeros((batch, 8, DQK, DV), F32), jnp.zeros((batch, 8, DQK, 1), F32),
                     jnp.zeros((batch, 8, 1, 1), F32))
            cx, (qc, ktc, vc, oc, actc) = _in_a(cx, comb_ctx, mv_ctx, norm_g[l], wa, batch, n_ctx)
            rows_c, cols_c = _gate_scans(actc, MLSTM_CHUNK)
            hfc, hbc, state = _mlstm(qc, ktc, vc, rows_c, cols_c, state, batch, MLSTM_CHUNK)
            lat, (ql, ktl, vl, ol, actl) = _in_a(lat, comb_lat, mv_lat, norm_g[l], wa, batch, tm)
            rows_l, cols_l = _gate_scans(actl, MLSTM_CHUNK)
            hfl, hbl, _ = _mlstm(ql, ktl, vl, rows_l, cols_l, state, batch, MLSTM_CHUNK)
            w_out = a_w_out[j].astype(BF16)
            head_g = a_head_g[j].astype(F32)[None, :]
            lat, h2l, idl, gtl = _out_a(hfl, hbl, ol, lat, mv_lat, norm_g[l], head_g, w_out, route_w,
                                        batch, tm)
            if ctx_after:
                cx, h2c, idc, gtc = _out_a(hfc, hbc, oc, cx, mv_ctx, norm_g[l], head_g, w_out, route_w,
                                           batch, n_ctx)
        else:
            w_in = b_w_in[j].astype(BF16)
            w_out = b_w_out[j].astype(BF16)
            conv_w = b_conv_w[j].astype(F32)
            lat, h2l, idl, gtl = _conv_layer(lat, comb_lat, mv_lat, norm_g[l], w_in, conv_w, w_out, route_w,
                                             batch, tm, GRID_W)
            if ctx_after:
                cx, h2c, idc, gtc = _conv_layer(cx, comb_ctx, mv_ctx, norm_g[l], w_in, conv_w, w_out,
                                                route_w, batch, n_ctx, n_ctx)
        if ctx_after:
            h2 = jnp.concatenate([h2l, h2c], axis=0)
            ids = jnp.concatenate([idl, idc], axis=1)
            gts = jnp.concatenate([gtl, gtc], axis=1)
        else:
            h2, ids, gts = h2l, idl, gtl
        ya, yb = _moe(h2, ids, gts, moe_w_gate[l], moe_w_up[l], moe_w_down[l])
        comb_lat = (ya[:t_lat], yb[:t_lat])
        comb_ctx = (ya[t_lat:], yb[t_lat:]) if ctx_after else ()
    out = _final(lat, comb_lat, _modv_prev_only(mod, depth, batch), final_g.astype(F32)[None, :], batch, tm)
    return out.reshape(batch, seq, d)
```

```python
import functools

import jax
import jax.numpy as jnp
from jax import lax
from jax.experimental import pallas as pl
from jax.experimental.pallas import tpu as pltpu

F32 = jnp.float32
BF16 = jnp.bfloat16

EPS = 1e-6
HEADS = 4
DQK = 128
DV = 256
NQ = HEADS * DQK
NV = HEADS * DV
IGATE_SOFTCAP = 15.0
GRID_W = 64
N_GROUPS = 4
EPG = 8
N_EXPERTS = N_GROUPS * EPG
ROUTE_ROWS = 40
MOE_BLOCK = 256
MLSTM_CHUNK = 128
LANES = 128
VMEM_LIMIT = 52 * 1024 * 1024


def _cparams(*sem):
    return pltpu.CompilerParams(dimension_semantics=sem, vmem_limit_bytes=VMEM_LIMIT)


def _nt_dot(a, b):
    return lax.dot_general(a, b, (((1,), (1,)), ((), ())), preferred_element_type=F32)


def _rms(x, g):
    return x * lax.rsqrt(jnp.mean(x * x, axis=-1, keepdims=True) + EPS) * g


def _mod_kernel(c_ref, w_ref, b_ref, o_ref):
    c = c_ref[...]
    s = (c * jax.nn.sigmoid(c)).astype(BF16)
    o_ref[0] = jnp.dot(s, w_ref[0].astype(BF16), preferred_element_type=F32) + b_ref[0]


def _modulation(cond, mod_w, mod_b):
    depth, d, n = mod_w.shape
    tn = 1024
    return pl.pallas_call(
        _mod_kernel,
        grid=(depth, n // tn),
        in_specs=[pl.BlockSpec((8, d), lambda l, j: (0, 0)),
                  pl.BlockSpec((1, d, tn), lambda l, j: (l, 0, j)),
                  pl.BlockSpec((1, 1, tn), lambda l, j: (l, 0, j))],
        out_specs=pl.BlockSpec((1, 8, tn), lambda l, j: (l, 0, j)),
        out_shape=jax.ShapeDtypeStruct((depth, 8, n), F32),
        compiler_params=_cparams("arbitrary", "arbitrary"),
    )(cond, mod_w, mod_b.reshape(depth, 1, n))


def _residual_in(x_ref, comb_refs, mv):
    x = x_ref[...]
    if comb_refs:
        ya_ref, yb_ref, gt_ref = comb_refs
        gt = gt_ref[...]
        x = x + mv[6:7] * (gt[:, 0:1] * ya_ref[...].astype(F32) + gt[:, 1:2] * yb_ref[...].astype(F32))
    return x


def _route(x2, wr_hi_ref, wr_lo_ref, rb_ref, tri_ref, ids_ref, gts_ref, cnt_ref):
    tm = x2.shape[0]
    x_hi = x2.astype(BF16)
    x_lo = (x2 - x_hi.astype(F32)).astype(BF16)
    w_hi = wr_hi_ref[...]
    lg = _nt_dot(w_hi, x_hi) + _nt_dot(wr_lo_ref[...], x_hi) + _nt_dot(w_hi, x_lo)
    lg = lg + rb_ref[...]
    row = lax.broadcasted_iota(jnp.int32, (EPG, tm), 0)
    gl = lg[N_EXPERTS:N_EXPERTS + EPG]
    gl = jnp.where(row < N_GROUPS, gl, -jnp.inf)
    gmx = jnp.max(gl, axis=0, keepdims=True)
    grp = jnp.min(jnp.where(gl == gmx, row, EPG), axis=0, keepdims=True)
    p_grp = 1.0 / jnp.sum(jnp.exp(gl - gmx), axis=0, keepdims=True)
    sel = lg[0:EPG]
    for g in range(1, N_GROUPS):
        sel = jnp.where(grp == g, lg[g * EPG:(g + 1) * EPG], sel)
    mx1 = jnp.max(sel, axis=0, keepdims=True)
    i1 = jnp.min(jnp.where(sel == mx1, row, EPG), axis=0, keepdims=True)
    rest = jnp.where(row == i1, -jnp.inf, sel)
    mx2 = jnp.max(rest, axis=0, keepdims=True)
    i2 = jnp.min(jnp.where(rest == mx2, row, EPG), axis=0, keepdims=True)
    e2 = jnp.exp(mx2 - mx1)
    inv = p_grp / (1.0 + e2)
    eid1 = grp * EPG + i1
    eid2 = grp * EPG + i2
    gts_ref[0:1, :] = inv
    gts_ref[1:2, :] = inv * e2
    erow = lax.broadcasted_iota(jnp.int32, (N_EXPERTS, tm), 0)
    oh1 = erow == eid1
    oh2 = erow == eid2
    member = jnp.where(oh1, 1.0, jnp.where(oh2, 1.0, 0.0)).astype(BF16)
    before = jnp.dot(member, tri_ref[...], preferred_element_type=F32)
    ids_ref[0:1, :] = eid1
    ids_ref[1:2, :] = eid2
    ids_ref[2:3, :] = jnp.sum(jnp.where(oh1, before, 0.0), axis=0, keepdims=True).astype(jnp.int32)
    ids_ref[3:4, :] = jnp.sum(jnp.where(oh2, before, 0.0), axis=0, keepdims=True).astype(jnp.int32)
    cnt_ref[0] = jnp.dot(member, jnp.ones((tm, LANES), BF16), preferred_element_type=F32)


def _norm2_and_route(x, mv, ng_ref, route_refs, out_refs):
    xo_ref, h2_ref, ids_ref, gts_ref, cnt_ref = out_refs
    xo_ref[...] = x
    x2 = _rms(x, ng_ref[1:2]) * (1.0 + mv[4:5]) + mv[3:4]
    h2_ref[...] = x2.astype(BF16)
    _route(x2, *route_refs, ids_ref, gts_ref, cnt_ref)


def _full(shape):
    return pl.BlockSpec(shape, lambda i: (0,) * len(shape))


def _tile_specs(tm, d, has_comb, tiles_per_batch):
    tok = pl.BlockSpec((tm, d), lambda i: (i, 0))
    specs = [tok]
    if has_comb:
        specs += [tok, tok, pl.BlockSpec((tm, 2), lambda i: (i, 0))]
    specs.append(pl.BlockSpec((None, 8, d), lambda i: (i // tiles_per_batch, 0, 0)))
    specs.append(_full((2, d)))
    return specs


def _route_specs(d, tm):
    return [_full((ROUTE_ROWS, d)), _full((ROUTE_ROWS, d)), _full((ROUTE_ROWS, 1)), _full((tm, tm))]


def _mixer_out(t, d, tm, h2_rows, h2_row0):
    off = h2_row0 // tm
    tok = pl.BlockSpec((tm, d), lambda i: (i, 0))
    specs = [tok, pl.BlockSpec((tm, d), lambda i: (i + off, 0)),
             pl.BlockSpec((4, tm), lambda i: (0, i)), pl.BlockSpec((2, tm), lambda i: (0, i)),
             pl.BlockSpec((1, N_EXPERTS, LANES), lambda i: (i, 0, 0))]
    shapes = [jax.ShapeDtypeStruct((t, d), F32), jax.ShapeDtypeStruct((h2_rows, d), BF16),
              jax.ShapeDtypeStruct((4, t), jnp.int32), jax.ShapeDtypeStruct((2, t), F32),
              jax.ShapeDtypeStruct((t // tm, N_EXPERTS, LANES), F32)]
    return specs, shapes


def _in_a_kernel(*refs, has_comb):
    n_in = 4 if has_comb else 1
    x_ref, comb, refs = refs[0], refs[1:n_in], refs[n_in:]
    mv_ref, ng_ref, wq_ref, wkt_ref, wv_ref, wo_ref, wgt_ref, bg_ref = refs[:8]
    outs = refs[8:]
    if has_comb:
        xo_ref, outs = outs[0], outs[1:]
    q_ref, kt_ref, v_ref, o_ref, a_ref = outs
    mv = mv_ref[...]
    x = _residual_in(x_ref, comb, mv)
    if has_comb:
        xo_ref[...] = x
    h = (_rms(x, ng_ref[0:1]) * (1.0 + mv[1:2]) + mv[0:1]).astype(BF16)
    q_ref[...] = (jnp.dot(h, wq_ref[...], preferred_element_type=F32) * (DQK ** -0.5)).astype(BF16)
    kt_ref[0] = _nt_dot(wkt_ref[...], h).astype(BF16)
    v_ref[...] = jnp.dot(h, wv_ref[...], preferred_element_type=F32).astype(BF16)
    o_ref[...] = jax.nn.sigmoid(jnp.dot(h, wo_ref[...], preferred_element_type=F32)).astype(BF16)
    g = _nt_dot(wgt_ref[...], h) + bg_ref[...]
    gi, gf = g[0:8], g[8:16]
    a_ref[0, 0:8, :] = IGATE_SOFTCAP * jnp.tanh(gi / IGATE_SOFTCAP)
    a_ref[0, 8:16, :] = jnp.minimum(gf, 0.0) - jnp.log(1.0 + jnp.exp(-jnp.abs(gf)))


def _in_a(x, comb, modv, ng, w, batch, tm):
    t, d = x.shape
    s = t // batch
    tpb = s // tm
    has_comb = bool(comb)
    in_specs = _tile_specs(tm, d, has_comb, tpb) + [_full(a.shape) for a in w]
    tok = lambda n: pl.BlockSpec((tm, n), lambda i: (i, 0))
    rows = lambda n: pl.BlockSpec((1, n, tm), lambda i: (i // tpb, 0, i % tpb))
    out_specs = [tok(NQ), rows(NQ), tok(NV), tok(d), rows(16)]
    out_shape = [jax.ShapeDtypeStruct((t, NQ), BF16), jax.ShapeDtypeStruct((batch, NQ, s), BF16),
                 jax.ShapeDtypeStruct((t, NV), BF16), jax.ShapeDtypeStruct((t, d), BF16),
                 jax.ShapeDtypeStruct((batch, 16, s), F32)]
    if has_comb:
        out_specs = [tok(d)] + out_specs
        out_shape = [jax.ShapeDtypeStruct((t, d), F32)] + out_shape
    outs = pl.pallas_call(
        functools.partial(_in_a_kernel, has_comb=has_comb),
        grid=(t // tm,), in_specs=in_specs, out_specs=out_specs, out_shape=out_shape,
        compiler_params=_cparams("parallel"),
    )(x, *comb, modv, ng, *w)
    if has_comb:
        return outs[0], outs[1:]
    return x, outs


def _chunk_scan(x, op, fill, forward):
    length = x.shape[-1]
    pos = lax.broadcasted_iota(jnp.int32, x.shape, 1)
    s = 1
    while s < length:
        if forward:
            x = op(x, jnp.where(pos >= s, pltpu.roll(x, s, axis=1), fill))
        else:
            x = op(x, jnp.where(pos < length - s, pltpu.roll(x, length - s, axis=1), fill))
        s *= 2
    return x


def _gate_kernel(a_ref, u_ref, b_ref, cm_ref):
    for r in range(2 * HEADS):
        fwd = r < HEADS
        b = _chunk_scan(a_ref[0, 8 + r], jnp.add, 0.0, fwd)
        u = a_ref[0, r] - b
        u_ref[0, r] = u
        b_ref[0, r] = b
        cm_ref[0, r] = _chunk_scan(u, jnp.maximum, -jnp.inf, fwd)


def _gate_scans(act, chunk):
    batch, _, s = act.shape
    nc = s // chunk
    a4 = act.reshape(batch, 16, nc, chunk)
    spec = pl.BlockSpec((1, 8, nc, chunk), lambda b: (b, 0, 0, 0))
    shp = jax.ShapeDtypeStruct((batch, 8, nc, chunk), F32)
    u, b, cm = pl.pallas_call(
        _gate_kernel, grid=(batch,),
        in_specs=[pl.BlockSpec((1, 16, nc, chunk), lambda b: (b, 0, 0, 0))],
        out_specs=[spec, spec, spec], out_shape=[shp, shp, shp],
        compiler_params=_cparams("parallel"),
    )(a4)
    rows = jnp.transpose(u, (0, 2, 1, 3))
    cols = jnp.transpose(jnp.concatenate([b, cm], axis=1), (0, 2, 3, 1))
    return rows, cols


def _mlstm_kernel(qf_ref, kf_ref, vf_ref, rf_ref, cf_ref, qb_ref, kb_ref, vb_ref, rb_ref, cb_ref,
                  c0_ref, n0_ref, m0_ref, hf_ref, hb_ref, ct_ref, nt_ref, mt_ref,
                  c_s, n_s, m_s, *, chunk):
    step = pl.program_id(1)
    last = pl.num_programs(1) - 1

    @pl.when(step == 0)
    def _():
        c_s[...] = c0_ref[0]
        n_s[...] = n0_ref[0]
        m_s[...] = m0_ref[0]

    ri = lax.broadcasted_iota(jnp.int32, (chunk, chunk), 0)
    ci = lax.broadcasted_iota(jnp.int32, (chunk, chunk), 1)
    for d, (q_ref, k_ref, v_ref, r_ref, c_ref, h_ref) in enumerate(
            ((qf_ref, kf_ref, vf_ref, rf_ref, cf_ref, hf_ref),
             (qb_ref, kb_ref, vb_ref, rb_ref, cb_ref, hb_ref))):
        mask = (ci <= ri) if d == 0 else (ci >= ri)
        end = chunk - 1 if d == 0 else 0
        for h in range(HEADS):
            idx = d * HEADS + h
            q = q_ref[:, h * DQK:(h + 1) * DQK]
            kt = k_ref[0, h * DQK:(h + 1) * DQK, :]
            v = v_ref[:, h * DV:(h + 1) * DV]
            u = r_ref[0, 0, idx:idx + 1, :]
            b = c_ref[0, 0, :, idx:idx + 1]
            cm = c_ref[0, 0, :, 8 + idx:9 + idx]
            c_st = c_s[idx]
            n_st = n_s[idx]
            m_st = m_s[idx]
            mm = jnp.maximum(m_st, cm)
            dmat = jnp.where(mask, jnp.exp(u - mm), 0.0)
            n_b = jnp.broadcast_to(n_st, (DQK, DQK)).astype(BF16)
            sx = jnp.dot(q, jnp.concatenate([kt, n_b], axis=1), preferred_element_type=F32)
            p = sx[:, :chunk] * dmat
            a_int = jnp.exp(m_st - mm)
            num = (jnp.dot(p.astype(BF16), v, preferred_element_type=F32)
                   + a_int * jnp.dot(q, c_st.astype(BF16), preferred_element_type=F32))
            den = jnp.sum(p, axis=-1, keepdims=True) + a_int * sx[:, chunk:chunk + 1]
            floor = jnp.exp(-b - mm)
            h_ref[:, h * DV:(h + 1) * DV] = num / jnp.maximum(jnp.abs(den), floor)
            mm_end = mm[end:end + 1]
            a_end = jnp.exp(u - mm_end)
            decay = jnp.exp(m_st - mm_end)
            kta = kt.astype(F32) * a_end
            c_s[idx] = decay * c_st + jnp.dot(kta.astype(BF16), v, preferred_element_type=F32)
            n_s[idx] = decay * n_st + jnp.sum(kta, axis=-1, keepdims=True)
            m_s[idx] = b[end:end + 1] + mm_end

    @pl.when(step == last)
    def _():
        ct_ref[0] = c_s[...]
        nt_ref[0] = n_s[...]
        mt_ref[0] = m_s[...]


def _mlstm(q, kt, v, rows, cols, state, batch, chunk):
    t = q.shape[0]
    s = t // batch
    nc = s // chunk
    c0, n0, m0 = state
    fwd = lambda b, c: b * nc + c
    bwd = lambda b, c: b * nc + nc - 1 - c

    def specs(cidx, tidx):
        return [pl.BlockSpec((chunk, NQ), lambda b, c: (tidx(b, c), 0)),
                pl.BlockSpec((1, NQ, chunk), lambda b, c: (b, 0, cidx(c))),
                pl.BlockSpec((chunk, NV), lambda b, c: (tidx(b, c), 0)),
                pl.BlockSpec((1, 1, 8, chunk), lambda b, c: (b, cidx(c), 0, 0)),
                pl.BlockSpec((1, 1, chunk, 16), lambda b, c: (b, cidx(c), 0, 0))]

    st_specs = [pl.BlockSpec((1, 8, DQK, DV), lambda b, c: (b, 0, 0, 0)),
                pl.BlockSpec((1, 8, DQK, 1), lambda b, c: (b, 0, 0, 0)),
                pl.BlockSpec((1, 8, 1, 1), lambda b, c: (b, 0, 0, 0))]
    st_shape = [jax.ShapeDtypeStruct((batch, 8, DQK, DV), F32),
                jax.ShapeDtypeStruct((batch, 8, DQK, 1), F32),
                jax.ShapeDtypeStruct((batch, 8, 1, 1), F32)]
    hf, hb, ct, nt, mt = pl.pallas_call(
        functools.partial(_mlstm_kernel, chunk=chunk),
        grid=(batch, nc),
        in_specs=specs(lambda c: c, fwd) + specs(lambda c: nc - 1 - c, bwd) + st_specs,
        out_specs=[pl.BlockSpec((chunk, NV), lambda b, c: (fwd(b, c), 0)),
                   pl.BlockSpec((chunk, NV), lambda b, c: (bwd(b, c), 0))] + st_specs,
        out_shape=[jax.ShapeDtypeStruct((t, NV), F32), jax.ShapeDtypeStruct((t, NV), F32)] + st_shape,
        scratch_shapes=[pltpu.VMEM((8, DQK, DV), F32), pltpu.VMEM((8, DQK, 1), F32),
                        pltpu.VMEM((8, 1, 1), F32)],
        compiler_params=_cparams("arbitrary", "arbitrary"),
    )(q, kt, v, rows, cols, q, kt, v, rows, cols, c0, n0, m0)
    return hf, hb, (ct, nt, mt)


def _out_a_kernel(*refs, aliased):
    hf_ref, hb_ref, o_ref, x_ref, mv_ref, ng_ref, hg_ref, wout_ref = refs[:8]
    route_refs = refs[8:12]
    out_refs = refs[12 + int(aliased):]
    mv = mv_ref[...]
    hsum = hf_ref[...] + hb_ref[...]
    parts = []
    for h in range(HEADS):
        hh = hsum[:, h * DV:(h + 1) * DV]
        parts.append(hh * lax.rsqrt(jnp.mean(hh * hh, axis=-1, keepdims=True) + EPS))
    hn = jnp.concatenate(parts, axis=1) * hg_ref[...] * o_ref[...].astype(F32)
    y = jnp.dot(hn.astype(BF16), wout_ref[...], preferred_element_type=F32)
    x = x_ref[...] + mv[2:3] * y
    _norm2_and_route(x, mv, ng_ref, route_refs, out_refs)


def _out_a(hf, hb, o, x, modv, ng, head_g, w_out, route_w, batch, tm, h2_buf, h2_rows, h2_row0):
    t, d = x.shape
    tpb = t // batch // tm
    tok = lambda n: pl.BlockSpec((tm, n), lambda i: (i, 0))
    in_specs = ([tok(NV), tok(NV), tok(d)] + _tile_specs(tm, d, False, tpb)
                + [_full((1, NV)), _full(w_out.shape)] + _route_specs(d, tm))
    args = [hf, hb, o, x, modv, ng, head_g, w_out, *route_w]
    return _mixer_call(functools.partial(_out_a_kernel, aliased=h2_buf is not None),
                       in_specs, args, t, d, tm, h2_buf, h2_rows, h2_row0)


def _mixer_call(body, in_specs, args, t, d, tm, h2_buf, h2_rows, h2_row0):
    out_specs, out_shape = _mixer_out(t, d, tm, h2_rows, h2_row0)
    aliases = {}
    if h2_buf is not None:
        aliases = {len(args): 1}
        in_specs = in_specs + [pl.BlockSpec(memory_space=pl.ANY)]
        args = args + [h2_buf]
    return pl.pallas_call(
        body, grid=(t // tm,), in_specs=in_specs, out_specs=out_specs, out_shape=out_shape,
        input_output_aliases=aliases, compiler_params=_cparams("parallel"),
    )(*args)


def _conv_kernel(*refs, has_comb, row_w, aliased):
    n_in = 4 if has_comb else 1
    x_ref, comb, refs = refs[0], refs[1:n_in], refs[n_in:]
    mv_ref, ng_ref, win_ref, cw_ref, wout_ref = refs[:5]
    route_refs = refs[5:9]
    out_refs = refs[9 + int(aliased):]
    mv = mv_ref[...]
    x = _residual_in(x_ref, comb, mv)
    d = x.shape[1]
    h = (_rms(x, ng_ref[0:1]) * (1.0 + mv[1:2]) + mv[0:1]).astype(BF16)
    p = jnp.dot(h, win_ref[...], preferred_element_type=F32)
    bg = p[:, :d]
    z = p[:, d:2 * d] * p[:, 2 * d:]
    tm = z.shape[0]
    pos = lax.broadcasted_iota(jnp.int32, z.shape, 0) % row_w
    left = jnp.where(pos == 0, 0.0, pltpu.roll(z, 1, axis=0))
    right = jnp.where(pos == row_w - 1, 0.0, pltpu.roll(z, tm - 1, axis=0))
    cw = cw_ref[...]
    y = cw[0:1] * left + cw[1:2] * z + cw[2:3] * right
    yl = jnp.dot((bg * y).astype(BF16), wout_ref[...], preferred_element_type=F32)
    x = x + mv[2:3] * yl
    _norm2_and_route(x, mv, ng_ref, route_refs, out_refs)


def _conv_layer(x, comb, modv, ng, w_in, conv_w, w_out, route_w, batch, tm, row_w, h2_buf, h2_rows, h2_row0):
    t, d = x.shape
    tpb = t // batch // tm
    in_specs = (_tile_specs(tm, d, bool(comb), tpb)
                + [_full(w_in.shape), _full(conv_w.shape), _full(w_out.shape)] + _route_specs(d, tm))
    args = [x, *comb, modv, ng, w_in, conv_w, w_out, *route_w]
    body = functools.partial(_conv_kernel, has_comb=bool(comb), row_w=row_w, aliased=h2_buf is not None)
    return _mixer_call(body, in_specs, args, t, d, tm, h2_buf, h2_rows, h2_row0)


def _moe_kernel(be_ref, nb_ref, x_ref, wg_ref, wu_ref, wd_ref, y_ref, wg_s, wu_s, wd_s):
    i = pl.program_id(0)

    @pl.when(i < nb_ref[0])
    def _():
        prev = be_ref[jnp.maximum(i - 1, 0)]

        @pl.when((i == 0) | (be_ref[i] != prev))
        def _():
            wg_s[...] = wg_ref[0].astype(BF16)
            wu_s[...] = wu_ref[0].astype(BF16)
            wd_s[...] = wd_ref[0].astype(BF16)

        x = x_ref[...]
        g = jnp.dot(x, wg_s[...], preferred_element_type=F32)
        u = jnp.dot(x, wu_s[...], preferred_element_type=F32)
        a = (g * jax.nn.sigmoid(g) * u).astype(BF16)
        y_ref[...] = jnp.dot(a, wd_s[...], preferred_element_type=F32).astype(BF16)


def _moe_experts(xb, blk_e, nb_used, w_gate, w_up, w_down):
    n_slots, d = xb.shape
    de = w_gate.shape[-1]
    n_blocks = n_slots // MOE_BLOCK
    grid_spec = pltpu.PrefetchScalarGridSpec(
        num_scalar_prefetch=2, grid=(n_blocks,),
        in_specs=[pl.BlockSpec((MOE_BLOCK, d), lambda i, be, nb: (i, 0)),
                  pl.BlockSpec((1, d, de), lambda i, be, nb: (be[i], 0, 0)),
                  pl.BlockSpec((1, d, de), lambda i, be, nb: (be[i], 0, 0)),
                  pl.BlockSpec((1, de, d), lambda i, be, nb: (be[i], 0, 0))],
        out_specs=pl.BlockSpec((MOE_BLOCK, d), lambda i, be, nb: (i, 0)),
        scratch_shapes=[pltpu.VMEM((d, de), BF16), pltpu.VMEM((d, de), BF16),
                        pltpu.VMEM((de, d), BF16)])
    return pl.pallas_call(
        _moe_kernel, grid_spec=grid_spec,
        out_shape=jax.ShapeDtypeStruct((n_slots, d), BF16),
        compiler_params=_cparams("arbitrary"),
    )(blk_e, nb_used, xb, w_gate, w_up, w_down)


def _dest_slots(ids, base, tm):
    t = ids.shape[1]
    nt = t // tm
    eid = ids[0:2].reshape(2, nt, tm)
    rank = ids[2:4].reshape(2, nt, tm)
    onehot = eid[..., None] == jnp.arange(N_EXPERTS, dtype=jnp.int32)
    off = jnp.sum(jnp.where(onehot, base[None, :, None, :], 0), axis=-1)
    return (rank + off).reshape(2, t)


def _moe(h2, routed, layer, w_gate, w_up, w_down):
    t = h2.shape[0]
    counts_tiles = jnp.concatenate([cnt[:, :, 0] for _, cnt, _ in routed], axis=0).astype(jnp.int32)
    tile_first = jnp.cumsum(counts_tiles, axis=0) - counts_tiles
    counts = jnp.sum(counts_tiles, axis=0)
    padded = (counts + MOE_BLOCK - 1) // MOE_BLOCK * MOE_BLOCK
    pends = jnp.cumsum(padded)
    base = (pends - padded)[None, :] + tile_first
    dests, row = [], 0
    for ids, cnt, tm in routed:
        nt = cnt.shape[0]
        dests.append(_dest_slots(ids, base[row:row + nt], tm))
        row += nt
    dest = jnp.concatenate(dests, axis=1) if len(dests) > 1 else dests[0]
    n_blocks = -(-2 * t // MOE_BLOCK) + N_EXPERTS
    n_slots = n_blocks * MOE_BLOCK
    tok = jnp.tile(jnp.arange(t, dtype=jnp.int32), 2)
    slot_tok = jnp.zeros((n_slots,), jnp.int32).at[dest.reshape(-1)].set(
        tok, unique_indices=True, mode='promise_in_bounds')
    blk_start = jnp.arange(n_blocks, dtype=jnp.int32) * MOE_BLOCK
    blk_e = jnp.minimum(jnp.sum((pends[None, :] <= blk_start[:, None]).astype(jnp.int32), axis=1),
                        N_EXPERTS - 1) + layer * N_EXPERTS
    nb_used = (pends[-1:] // MOE_BLOCK).astype(jnp.int32)
    xb = h2.at[slot_tok].get(mode='promise_in_bounds')
    yb = _moe_experts(xb, blk_e, nb_used, w_gate, w_up, w_down)
    return [(yb.at[d[0]].get(mode='promise_in_bounds'), yb.at[d[1]].get(mode='promise_in_bounds'))
            for d in dests]


def _final_kernel(x_ref, ya_ref, yb_ref, gt_ref, mv_ref, g_ref, o_ref):
    x = _residual_in(x_ref, (ya_ref, yb_ref, gt_ref), mv_ref[...])
    o_ref[...] = _rms(x, g_ref[...])


def _final(x, comb, modv, g, batch, tm):
    t, d = x.shape
    tpb = t // batch // tm
    tok = pl.BlockSpec((tm, d), lambda i: (i, 0))
    return pl.pallas_call(
        _final_kernel, grid=(t // tm,),
        in_specs=[tok, tok, tok, pl.BlockSpec((tm, 2), lambda i: (i, 0)),
                  pl.BlockSpec((None, 8, d), lambda i: (i // tpb, 0, 0)), _full((1, d))],
        out_specs=tok, out_shape=jax.ShapeDtypeStruct((t, d), F32),
        compiler_params=_cparams("parallel"),
    )(x, *comb, modv, g)


def _prep_a(w_in, b_gate):
    d = w_in.shape[0]
    wq = w_in[:, :NQ].astype(BF16)
    wkt = w_in[:, NQ:2 * NQ].T.astype(BF16)
    wv = w_in[:, 2 * NQ:2 * NQ + NV].astype(BF16)
    wo = w_in[:, 2 * NQ + NV:2 * NQ + NV + d].astype(BF16)
    perm = jnp.array([0, 1, 2, 3, 8, 9, 10, 11, 4, 5, 6, 7, 12, 13, 14, 15], jnp.int32)
    wgt = w_in[:, 2 * NQ + NV + d:].T[perm].astype(BF16)
    bg = b_gate.astype(F32)[perm][:, None]
    return wq, wkt, wv, wo, wgt, bg


def _prep_route(w_group, b_group, w_router, b_router, tm):
    d = w_group.shape[0]
    pad = ROUTE_ROWS - N_EXPERTS - N_GROUPS
    wt = jnp.concatenate([w_router.T, w_group.T, jnp.zeros((pad, d), F32)], axis=0).astype(F32)
    hi = wt.astype(BF16)
    lo = (wt - hi.astype(F32)).astype(BF16)
    rb = jnp.concatenate([b_router, b_group, jnp.zeros((pad,), F32)]).astype(F32)[:, None]
    tri = jnp.triu(jnp.ones((tm, tm), BF16), k=1)
    return hi, lo, rb, tri


def _modv(mod, l, rows):
    depth = mod.shape[0]
    d = mod.shape[-1] // 6
    zero = jnp.zeros((len(rows), 1, d), F32)
    cur = jnp.stack([mod[l, r].reshape(6, d) for r in rows]) if l < depth else jnp.zeros((len(rows), 6, d), F32)
    prev = jnp.stack([mod[l - 1, r].reshape(6, d)[5:6] for r in rows]) if l > 0 else zero
    return jnp.concatenate([cur, prev, zero], axis=1)


def kernel(x, c, ctx, c_ctx, mod_w, mod_b, norm_g, final_g, a_w_in, a_b_gate, a_head_g, a_w_out,
           b_w_in, b_conv_w, b_w_out, moe_w_group, moe_b_group, moe_w_router, moe_b_router,
           moe_w_gate, moe_w_up, moe_w_down):
    batch, seq, d = x.shape
    n_ctx = ctx.shape[1]
    depth = mod_w.shape[0]
    assert batch + 1 <= 8 and seq % MLSTM_CHUNK == 0 and n_ctx % MLSTM_CHUNK == 0
    tm = min(512, seq)
    t_lat = batch * seq
    t_ctx = batch * n_ctx
    assert t_lat % n_ctx == 0 and seq % GRID_W == 0

    cond = jnp.concatenate([c, c_ctx[None, :], jnp.zeros((8 - batch - 1, d), F32)], axis=0)
    mod = _modulation(cond, mod_w, mod_b)
    de = moe_w_gate.shape[-1]
    w_gate = moe_w_gate.reshape(depth * N_EXPERTS, d, de)
    w_up = moe_w_up.reshape(depth * N_EXPERTS, d, de)
    w_down = moe_w_down.reshape(depth * N_EXPERTS, de, d)

    lat = x.reshape(t_lat, d)
    cx = ctx.reshape(t_ctx, d)
    comb_lat, comb_ctx = (), ()
    for l in range(depth):
        kind, j = l % 2, l // 2
        ctx_after = any(i % 2 == 0 for i in range(l + 1, depth))
        mv_lat = _modv(mod, l, list(range(batch)))
        mv_ctx = _modv(mod, l, [batch] * batch)
        route_args = (moe_w_group[l], moe_b_group[l], moe_w_router[l], moe_b_router[l])
        route_lat = _prep_route(*route_args, tm)
        route_ctx = _prep_route(*route_args, n_ctx)
        h2_rows = t_lat + (t_ctx if ctx_after else 0)
        if kind == 0:
            wa = _prep_a(a_w_in[j], a_b_gate[j])
            state = (jnp.zeros((batch, 8, DQK, DV), F32), jnp.zeros((batch, 8, DQK, 1), F32),
                     jnp.zeros((batch, 8, 1, 1), F32))
            cx, (qc, ktc, vc, oc, actc) = _in_a(cx, comb_ctx, mv_ctx, norm_g[l], wa, batch, n_ctx)
            rows_c, cols_c = _gate_scans(actc, MLSTM_CHUNK)
            hfc, hbc, state = _mlstm(qc, ktc, vc, rows_c, cols_c, state, batch, MLSTM_CHUNK)
            lat, (ql, ktl, vl, ol, actl) = _in_a(lat, comb_lat, mv_lat, norm_g[l], wa, batch, tm)
            rows_l, cols_l = _gate_scans(actl, MLSTM_CHUNK)
            hfl, hbl, _ = _mlstm(ql, ktl, vl, rows_l, cols_l, state, batch, MLSTM_CHUNK)
            w_out = a_w_out[j].astype(BF16)
            head_g = a_head_g[j].astype(F32)[None, :]
            lat, h2, idl, gtl, cntl = _out_a(hfl, hbl, ol, lat, mv_lat, norm_g[l], head_g, w_out, route_lat,
                                             batch, tm, None, h2_rows, 0)
            if ctx_after:
                cx, h2, idc, gtc, cntc = _out_a(hfc, hbc, oc, cx, mv_ctx, norm_g[l], head_g, w_out, route_ctx,
                                                batch, n_ctx, h2, h2_rows, t_lat)
        else:
            w_in = b_w_in[j].astype(BF16)
            w_out = b_w_out[j].astype(BF16)
            conv_w = b_conv_w[j].astype(F32)
            lat, h2, idl, gtl, cntl = _conv_layer(lat, comb_lat, mv_lat, norm_g[l], w_in, conv_w, w_out,
                                                  route_lat, batch, tm, GRID_W, None, h2_rows, 0)
            if ctx_after:
                cx, h2, idc, gtc, cntc = _conv_layer(cx, comb_ctx, mv_ctx, norm_g[l], w_in, conv_w, w_out,
                                                     route_ctx, batch, n_ctx, n_ctx, h2, h2_rows, t_lat)
        routed = [(idl, cntl, tm)] + ([(idc, cntc, n_ctx)] if ctx_after else [])
        outs = _moe(h2, routed, l, w_gate, w_up, w_down)
        comb_lat = (*outs[0], gtl.T)
        comb_ctx = (*outs[1], gtc.T) if ctx_after else ()
    out = _final(lat, comb_lat, _modv(mod, depth, list(range(batch))), final_g.astype(F32)[None, :], batch, tm)
    return out.reshape(batch, seq, d)
```

```python
import functools

import jax
import jax.numpy as jnp
from jax import lax
from jax.experimental import pallas as pl
from jax.experimental.pallas import tpu as pltpu

F32 = jnp.float32
BF16 = jnp.bfloat16

EPS = 1e-6
HEADS = 4
DQK = 128
DV = 256
NQ = HEADS * DQK
NV = HEADS * DV
IGATE_SOFTCAP = 15.0
GRID_W = 64
N_GROUPS = 4
EPG = 8
N_EXPERTS = N_GROUPS * EPG
ROUTE_ROWS = 40
MOE_BLOCK = 256
MLSTM_CHUNK = 128
LANES = 128
VMEM_LIMIT = 52 * 1024 * 1024


def _cparams(*sem):
    return pltpu.CompilerParams(dimension_semantics=sem, vmem_limit_bytes=VMEM_LIMIT)


def _nt_dot(a, b):
    return lax.dot_general(a, b, (((1,), (1,)), ((), ())), preferred_element_type=F32)


def _rms(x, g):
    return x * lax.rsqrt(jnp.mean(x * x, axis=-1, keepdims=True) + EPS) * g


def _mod_kernel(c_ref, w_ref, b_ref, o_ref):
    c = c_ref[...]
    s = (c * jax.nn.sigmoid(c)).astype(BF16)
    o_ref[0] = jnp.dot(s, w_ref[0].astype(BF16), preferred_element_type=F32) + b_ref[0]


def _modulation(cond, mod_w, mod_b):
    depth, d, n = mod_w.shape
    tn = 1024
    return pl.pallas_call(
        _mod_kernel,
        grid=(depth, n // tn),
        in_specs=[pl.BlockSpec((8, d), lambda l, j: (0, 0)),
                  pl.BlockSpec((1, d, tn), lambda l, j: (l, 0, j)),
                  pl.BlockSpec((1, 1, tn), lambda l, j: (l, 0, j))],
        out_specs=pl.BlockSpec((1, 8, tn), lambda l, j: (l, 0, j)),
        out_shape=jax.ShapeDtypeStruct((depth, 8, n), F32),
        compiler_params=_cparams("arbitrary", "arbitrary"),
    )(cond, mod_w, mod_b.reshape(depth, 1, n))


def _residual_in(x_ref, comb_refs, mv):
    x = x_ref[...]
    if comb_refs:
        ya_ref, yb_ref, gt_ref = comb_refs
        gt = gt_ref[...]
        x = x + mv[6:7] * (gt[:, 0:1] * ya_ref[...].astype(F32) + gt[:, 1:2] * yb_ref[...].astype(F32))
    return x


def _route(x2, wr_hi_ref, wr_lo_ref, rb_ref, tri_ref, ids_ref, gts_ref, cnt_ref):
    tm = x2.shape[0]
    x_hi = x2.astype(BF16)
    x_lo = (x2 - x_hi.astype(F32)).astype(BF16)
    w_hi = wr_hi_ref[...]
    lg = _nt_dot(w_hi, x_hi) + _nt_dot(wr_lo_ref[...], x_hi) + _nt_dot(w_hi, x_lo)
    lg = lg + rb_ref[...]
    row = lax.broadcasted_iota(jnp.int32, (EPG, tm), 0)
    gl = lg[N_EXPERTS:N_EXPERTS + EPG]
    gl = jnp.where(row < N_GROUPS, gl, -jnp.inf)
    gmx = jnp.max(gl, axis=0, keepdims=True)
    grp = jnp.min(jnp.where(gl == gmx, row, EPG), axis=0, keepdims=True)
    p_grp = 1.0 / jnp.sum(jnp.exp(gl - gmx), axis=0, keepdims=True)
    sel = lg[0:EPG]
    for g in range(1, N_GROUPS):
        sel = jnp.where(grp == g, lg[g * EPG:(g + 1) * EPG], sel)
    mx1 = jnp.max(sel, axis=0, keepdims=True)
    i1 = jnp.min(jnp.where(sel == mx1, row, EPG), axis=0, keepdims=True)
    rest = jnp.where(row == i1, -jnp.inf, sel)
    mx2 = jnp.max(rest, axis=0, keepdims=True)
    i2 = jnp.min(jnp.where(rest == mx2, row, EPG), axis=0, keepdims=True)
    e2 = jnp.exp(mx2 - mx1)
    inv = p_grp / (1.0 + e2)
    eid1 = grp * EPG + i1
    eid2 = grp * EPG + i2
    gts_ref[0:1, :] = inv
    gts_ref[1:2, :] = inv * e2
    erow = lax.broadcasted_iota(jnp.int32, (N_EXPERTS, tm), 0)
    oh1 = erow == eid1
    oh2 = erow == eid2
    member = jnp.where(oh1, 1.0, jnp.where(oh2, 1.0, 0.0)).astype(BF16)
    before = jnp.dot(member, tri_ref[...], preferred_element_type=F32)
    ids_ref[0:1, :] = eid1
    ids_ref[1:2, :] = eid2
    ids_ref[2:3, :] = jnp.sum(jnp.where(oh1, before, 0.0), axis=0, keepdims=True).astype(jnp.int32)
    ids_ref[3:4, :] = jnp.sum(jnp.where(oh2, before, 0.0), axis=0, keepdims=True).astype(jnp.int32)
    cnt_ref[0] = jnp.dot(member, jnp.ones((tm, LANES), BF16), preferred_element_type=F32)


def _norm2_and_route(x, mv, ng_ref, route_refs, out_refs):
    xo_ref, h2_ref, ids_ref, gts_ref, cnt_ref = out_refs
    xo_ref[...] = x
    x2 = _rms(x, ng_ref[1:2]) * (1.0 + mv[4:5]) + mv[3:4]
    h2_ref[...] = x2.astype(BF16)
    _route(x2, *route_refs, ids_ref, gts_ref, cnt_ref)


def _full(shape):
    return pl.BlockSpec(shape, lambda i: (0,) * len(shape))


def _tile_specs(tm, d, has_comb, tiles_per_batch):
    tok = pl.BlockSpec((tm, d), lambda i: (i, 0))
    specs = [tok]
    if has_comb:
        specs += [tok, tok, pl.BlockSpec((tm, 2), lambda i: (i, 0))]
    specs.append(pl.BlockSpec((None, 8, d), lambda i: (i // tiles_per_batch, 0, 0)))
    specs.append(_full((2, d)))
    return specs


def _route_specs(d, tm):
    return [_full((ROUTE_ROWS, d)), _full((ROUTE_ROWS, d)), _full((ROUTE_ROWS, 1)), _full((tm, tm))]


def _mixer_out(t, d, tm, h2_rows, h2_row0):
    off = h2_row0 // tm
    tok = pl.BlockSpec((tm, d), lambda i: (i, 0))
    specs = [tok, pl.BlockSpec((tm, d), lambda i: (i + off, 0)),
             pl.BlockSpec((4, tm), lambda i: (0, i)), pl.BlockSpec((2, tm), lambda i: (0, i)),
             pl.BlockSpec((1, N_EXPERTS, LANES), lambda i: (i, 0, 0))]
    shapes = [jax.ShapeDtypeStruct((t, d), F32), jax.ShapeDtypeStruct((h2_rows, d), BF16),
              jax.ShapeDtypeStruct((4, t), jnp.int32), jax.ShapeDtypeStruct((2, t), F32),
              jax.ShapeDtypeStruct((t // tm, N_EXPERTS, LANES), F32)]
    return specs, shapes


def _in_a_kernel(*refs, has_comb):
    n_in = 4 if has_comb else 1
    x_ref, comb, refs = refs[0], refs[1:n_in], refs[n_in:]
    mv_ref, ng_ref, wqt_ref, wk_ref, wvt_ref, wot_ref, wgt_ref, bg_ref = refs[:8]
    outs = refs[8:]
    if has_comb:
        xo_ref, outs = outs[0], outs[1:]
    qt_ref, k_ref, vt_ref, ot_ref, a_ref = outs
    mv = mv_ref[...]
    x = _residual_in(x_ref, comb, mv)
    if has_comb:
        xo_ref[...] = x
    h = (_rms(x, ng_ref[0:1]) * (1.0 + mv[1:2]) + mv[0:1]).astype(BF16)
    qt_ref[0] = (_nt_dot(wqt_ref[...], h) * (DQK ** -0.5)).astype(BF16)
    k_ref[...] = jnp.dot(h, wk_ref[...], preferred_element_type=F32).astype(BF16)
    vt_ref[0] = _nt_dot(wvt_ref[...], h).astype(BF16)
    ot_ref[0] = jax.nn.sigmoid(_nt_dot(wot_ref[...], h)).astype(BF16)
    g = _nt_dot(wgt_ref[...], h) + bg_ref[...]
    gi, gf = g[0:8], g[8:16]
    a_ref[0, 0:8, :] = IGATE_SOFTCAP * jnp.tanh(gi / IGATE_SOFTCAP)
    a_ref[0, 8:16, :] = jnp.minimum(gf, 0.0) - jnp.log(1.0 + jnp.exp(-jnp.abs(gf)))


def _in_a(x, comb, modv, ng, w, batch, tm):
    t, d = x.shape
    s = t // batch
    tpb = s // tm
    has_comb = bool(comb)
    in_specs = _tile_specs(tm, d, has_comb, tpb) + [_full(a.shape) for a in w]
    tok = lambda n: pl.BlockSpec((tm, n), lambda i: (i, 0))
    rows = lambda n: pl.BlockSpec((1, n, tm), lambda i: (i // tpb, 0, i % tpb))
    out_specs = [rows(NQ), tok(NQ), rows(NV), rows(d), rows(16)]
    out_shape = [jax.ShapeDtypeStruct((batch, NQ, s), BF16), jax.ShapeDtypeStruct((t, NQ), BF16),
                 jax.ShapeDtypeStruct((batch, NV, s), BF16), jax.ShapeDtypeStruct((batch, d, s), BF16),
                 jax.ShapeDtypeStruct((batch, 16, s), F32)]
    if has_comb:
        out_specs = [tok(d)] + out_specs
        out_shape = [jax.ShapeDtypeStruct((t, d), F32)] + out_shape
    outs = pl.pallas_call(
        functools.partial(_in_a_kernel, has_comb=has_comb),
        grid=(t // tm,), in_specs=in_specs, out_specs=out_specs, out_shape=out_shape,
        compiler_params=_cparams("parallel"),
    )(x, *comb, modv, ng, *w)
    if has_comb:
        return outs[0], outs[1:]
    return x, outs


def _chunk_scan(x, op, fill, forward):
    length = x.shape[-1]
    pos = lax.broadcasted_iota(jnp.int32, x.shape, 1)
    s = 1
    while s < length:
        if forward:
            x = op(x, jnp.where(pos >= s, pltpu.roll(x, s, axis=1), fill))
        else:
            x = op(x, jnp.where(pos < length - s, pltpu.roll(x, length - s, axis=1), fill))
        s *= 2
    return x


def _gate_kernel(a_ref, o_ref):
    for r in range(2 * HEADS):
        fwd = r < HEADS
        b = _chunk_scan(a_ref[0, 8 + r], jnp.add, 0.0, fwd)
        u = a_ref[0, r] - b
        cm = _chunk_scan(u, jnp.maximum, -jnp.inf, fwd)
        end = b.shape[-1] - 1 if fwd else 0
        o_ref[0, r] = u
        o_ref[0, 8 + r] = b
        o_ref[0, 16 + r] = cm
        o_ref[0, 24 + r] = jnp.broadcast_to(cm[:, end:end + 1], cm.shape)
        o_ref[0, 32 + r] = jnp.broadcast_to(b[:, end:end + 1], b.shape)


GATE_ROWS = 40


def _gate_scans(act, chunk):
    batch, _, s = act.shape
    nc = s // chunk
    g = pl.pallas_call(
        _gate_kernel, grid=(batch,),
        in_specs=[pl.BlockSpec((1, 16, nc, chunk), lambda b: (b, 0, 0, 0))],
        out_specs=pl.BlockSpec((1, GATE_ROWS, nc, chunk), lambda b: (b, 0, 0, 0)),
        out_shape=jax.ShapeDtypeStruct((batch, GATE_ROWS, nc, chunk), F32),
        compiler_params=_cparams("parallel"),
    )(act.reshape(batch, 16, nc, chunk))
    rows = jnp.transpose(g, (0, 2, 1, 3))
    cols = jnp.transpose(g[:, 0:8], (0, 2, 3, 1))
    return rows, cols


def _mlstm_kernel(qf_ref, kf_ref, vf_ref, rf_ref, cf_ref, qb_ref, kb_ref, vb_ref, rb_ref, cb_ref,
                  c0_ref, n0_ref, m0_ref, hf_ref, hb_ref, ct_ref, nt_ref, mt_ref,
                  c_s, n_s, m_s, *, chunk):
    step = pl.program_id(1)
    last = pl.num_programs(1) - 1

    @pl.when(step == 0)
    def _():
        c_s[...] = c0_ref[0]
        n_s[...] = n0_ref[0]
        m_s[...] = m0_ref[0]

    si = lax.broadcasted_iota(jnp.int32, (chunk, chunk), 0)
    ji = lax.broadcasted_iota(jnp.int32, (chunk, chunk), 1)
    for d, (q_ref, k_ref, v_ref, r_ref, c_ref, h_ref) in enumerate(
            ((qf_ref, kf_ref, vf_ref, rf_ref, cf_ref, hf_ref),
             (qb_ref, kb_ref, vb_ref, rb_ref, cb_ref, hb_ref))):
        mask = (si <= ji) if d == 0 else (si >= ji)
        for h in range(HEADS):
            idx = d * HEADS + h
            qt = q_ref[0, h * DQK:(h + 1) * DQK, :]
            k = k_ref[:, h * DQK:(h + 1) * DQK]
            vt = v_ref[0, h * DV:(h + 1) * DV, :]
            u_col = c_ref[0, 0, :, idx:idx + 1]
            u_row = r_ref[0, 0, idx:idx + 1, :]
            b_row = r_ref[0, 0, 8 + idx:9 + idx, :]
            cm_row = r_ref[0, 0, 16 + idx:17 + idx, :]
            cm_end = r_ref[0, 0, 24 + idx:25 + idx, :]
            b_end = r_ref[0, 0, 32 + idx:33 + idx, :]
            ct_st = c_s[idx]
            n_st = n_s[idx]
            m_st = m_s[idx]
            mm = jnp.maximum(m_st, cm_row)
            dt = jnp.where(mask, jnp.exp(u_col - mm), 0.0)
            lhs = jnp.concatenate([k, jnp.broadcast_to(n_st, (16, DQK)).astype(BF16)], axis=0)
            sx = jnp.dot(lhs, qt, preferred_element_type=F32)
            pt = sx[:chunk] * dt
            a_int = jnp.exp(m_st - mm)
            den = jnp.sum(pt, axis=0, keepdims=True) + a_int * sx[chunk:chunk + 1]
            inv = 1.0 / jnp.maximum(jnp.abs(den), jnp.exp(-b_row - mm))
            wts = jnp.concatenate([(pt * inv).astype(BF16),
                                   (qt.astype(F32) * (a_int * inv)).astype(BF16)], axis=0)
            vals = jnp.concatenate([vt, ct_st.astype(BF16)], axis=1)
            h_ref[0, h * DV:(h + 1) * DV, :] = jnp.dot(vals, wts, preferred_element_type=F32)
            mm_end = jnp.maximum(m_st, cm_end)
            a_end = jnp.exp(u_row - mm_end)
            decay = jnp.exp(m_st - mm_end)
            upd = jnp.concatenate([(vt.astype(F32) * a_end).astype(BF16),
                                   jnp.broadcast_to(a_end, (16, chunk)).astype(BF16)], axis=0)
            upd = jnp.dot(upd, k, preferred_element_type=F32)
            c_s[idx] = decay * ct_st + upd[:DV]
            n_s[idx] = decay * n_st + upd[DV:DV + 1]
            m_s[idx] = b_end + mm_end

    @pl.when(step == last)
    def _():
        ct_ref[0] = c_s[...]
        nt_ref[0] = n_s[...]
        mt_ref[0] = m_s[...]


def _mlstm_zero_state(batch, chunk):
    return (jnp.zeros((batch, 8, DV, DQK), F32), jnp.zeros((batch, 8, 1, DQK), F32),
            jnp.zeros((batch, 8, 1, chunk), F32))


def _mlstm(qt, k, vt, rows, cols, state, batch, chunk):
    t = k.shape[0]
    s = t // batch
    nc = s // chunk
    assert chunk == DQK
    c0, n0, m0 = state

    def specs(cidx):
        return [pl.BlockSpec((1, NQ, chunk), lambda b, c: (b, 0, cidx(c))),
                pl.BlockSpec((chunk, NQ), lambda b, c: (b * nc + cidx(c), 0)),
                pl.BlockSpec((1, NV, chunk), lambda b, c: (b, 0, cidx(c))),
                pl.BlockSpec((1, 1, GATE_ROWS, chunk), lambda b, c: (b, cidx(c), 0, 0)),
                pl.BlockSpec((1, 1, chunk, 8), lambda b, c: (b, cidx(c), 0, 0))]

    fwd = lambda c: c
    bwd = lambda c: nc - 1 - c
    st_specs = [pl.BlockSpec((1, 8, DV, DQK), lambda b, c: (b, 0, 0, 0)),
                pl.BlockSpec((1, 8, 1, DQK), lambda b, c: (b, 0, 0, 0)),
                pl.BlockSpec((1, 8, 1, chunk), lambda b, c: (b, 0, 0, 0))]
    st_shape = [jax.ShapeDtypeStruct(a.shape, F32) for a in state]
    h_shape = jax.ShapeDtypeStruct((batch, NV, s), F32)
    args = (qt, k, vt, rows, cols)
    hf, hb, ct, nt, mt = pl.pallas_call(
        functools.partial(_mlstm_kernel, chunk=chunk),
        grid=(batch, nc),
        in_specs=specs(fwd) + specs(bwd) + st_specs,
        out_specs=[pl.BlockSpec((1, NV, chunk), lambda b, c: (b, 0, fwd(c))),
                   pl.BlockSpec((1, NV, chunk), lambda b, c: (b, 0, bwd(c)))] + st_specs,
        out_shape=[h_shape, h_shape] + st_shape,
        scratch_shapes=[pltpu.VMEM((8, DV, DQK), F32), pltpu.VMEM((8, 1, DQK), F32),
                        pltpu.VMEM((8, 1, chunk), F32)],
        compiler_params=_cparams("arbitrary", "arbitrary"),
    )(*args, *args, c0, n0, m0)
    return hf, hb, (ct, nt, mt)


def _out_a_kernel(*refs, aliased):
    hf_ref, hb_ref, o_ref, x_ref, mv_ref, ng_ref, hg_ref, wout_ref = refs[:8]
    route_refs = refs[8:12]
    out_refs = refs[12 + int(aliased):]
    mv = mv_ref[...]
    hsum = hf_ref[0] + hb_ref[0]
    parts = []
    for h in range(HEADS):
        hh = hsum[h * DV:(h + 1) * DV]
        parts.append(hh * lax.rsqrt(jnp.mean(hh * hh, axis=0, keepdims=True) + EPS))
    hn = jnp.concatenate(parts, axis=0) * hg_ref[...] * o_ref[0].astype(F32)
    y = jnp.dot(wout_ref[...], hn.astype(BF16), preferred_element_type=F32).T
    x = x_ref[...] + mv[2:3] * y
    _norm2_and_route(x, mv, ng_ref, route_refs, out_refs)


def _out_a(hf, hb, o, x, modv, ng, head_g, w_out_t, route_w, batch, tm, h2_buf, h2_rows, h2_row0):
    t, d = x.shape
    tpb = t // batch // tm
    rows = lambda n: pl.BlockSpec((1, n, tm), lambda i: (i // tpb, 0, i % tpb))
    in_specs = ([rows(NV), rows(NV), rows(d)] + _tile_specs(tm, d, False, tpb)
                + [_full((NV, 1)), _full(w_out_t.shape)] + _route_specs(d, tm))
    w_out = w_out_t
    args = [hf, hb, o, x, modv, ng, head_g, w_out, *route_w]
    return _mixer_call(functools.partial(_out_a_kernel, aliased=h2_buf is not None),
                       in_specs, args, t, d, tm, h2_buf, h2_rows, h2_row0)


def _mixer_call(body, in_specs, args, t, d, tm, h2_buf, h2_rows, h2_row0):
    out_specs, out_shape = _mixer_out(t, d, tm, h2_rows, h2_row0)
    aliases = {}
    if h2_buf is not None:
        aliases = {len(args): 1}
        in_specs = in_specs + [pl.BlockSpec(memory_space=pl.ANY)]
        args = args + [h2_buf]
    return pl.pallas_call(
        body, grid=(t // tm,), in_specs=in_specs, out_specs=out_specs, out_shape=out_shape,
        input_output_aliases=aliases, compiler_params=_cparams("parallel"),
    )(*args)


def _conv_kernel(*refs, has_comb, row_w, aliased):
    n_in = 4 if has_comb else 1
    x_ref, comb, refs = refs[0], refs[1:n_in], refs[n_in:]
    mv_ref, ng_ref, win_ref, cw_ref, wout_ref = refs[:5]
    route_refs = refs[5:9]
    out_refs = refs[9 + int(aliased):]
    mv = mv_ref[...]
    x = _residual_in(x_ref, comb, mv)
    d = x.shape[1]
    h = (_rms(x, ng_ref[0:1]) * (1.0 + mv[1:2]) + mv[0:1]).astype(BF16)
    p = jnp.dot(h, win_ref[...], preferred_element_type=F32)
    bg = p[:, :d]
    z = p[:, d:2 * d] * p[:, 2 * d:]
    tm = z.shape[0]
    pos = lax.broadcasted_iota(jnp.int32, z.shape, 0) % row_w
    left = jnp.where(pos == 0, 0.0, pltpu.roll(z, 1, axis=0))
    right = jnp.where(pos == row_w - 1, 0.0, pltpu.roll(z, tm - 1, axis=0))
    cw = cw_ref[...]
    y = cw[0:1] * left + cw[1:2] * z + cw[2:3] * right
    yl = jnp.dot((bg * y).astype(BF16), wout_ref[...], preferred_element_type=F32)
    x = x + mv[2:3] * yl
    _norm2_and_route(x, mv, ng_ref, route_refs, out_refs)


def _conv_layer(x, comb, modv, ng, w_in, conv_w, w_out, route_w, batch, tm, row_w, h2_buf, h2_rows, h2_row0):
    t, d = x.shape
    tpb = t // batch // tm
    in_specs = (_tile_specs(tm, d, bool(comb), tpb)
                + [_full(w_in.shape), _full(conv_w.shape), _full(w_out.shape)] + _route_specs(d, tm))
    args = [x, *comb, modv, ng, w_in, conv_w, w_out, *route_w]
    body = functools.partial(_conv_kernel, has_comb=bool(comb), row_w=row_w, aliased=h2_buf is not None)
    return _mixer_call(body, in_specs, args, t, d, tm, h2_buf, h2_rows, h2_row0)


def _moe_kernel(be_ref, nb_ref, x_ref, wg_ref, wu_ref, wd_ref, y_ref, wg_s, wu_s, wd_s):
    i = pl.program_id(0)

    @pl.when(i < nb_ref[0])
    def _():
        prev = be_ref[jnp.maximum(i - 1, 0)]

        @pl.when((i == 0) | (be_ref[i] != prev))
        def _():
            wg_s[...] = wg_ref[0].astype(BF16)
            wu_s[...] = wu_ref[0].astype(BF16)
            wd_s[...] = wd_ref[0].astype(BF16)

        x = x_ref[...]
        g = jnp.dot(x, wg_s[...], preferred_element_type=F32)
        u = jnp.dot(x, wu_s[...], preferred_element_type=F32)
        a = (g * jax.nn.sigmoid(g) * u).astype(BF16)
        y_ref[...] = jnp.dot(a, wd_s[...], preferred_element_type=F32).astype(BF16)


def _moe_experts(xb, blk_e, nb_used, w_gate, w_up, w_down):
    n_slots, d = xb.shape
    de = w_gate.shape[-1]
    n_blocks = n_slots // MOE_BLOCK
    grid_spec = pltpu.PrefetchScalarGridSpec(
        num_scalar_prefetch=2, grid=(n_blocks,),
        in_specs=[pl.BlockSpec((MOE_BLOCK, d), lambda i, be, nb: (i, 0)),
                  pl.BlockSpec((1, d, de), lambda i, be, nb: (be[i], 0, 0)),
                  pl.BlockSpec((1, d, de), lambda i, be, nb: (be[i], 0, 0)),
                  pl.BlockSpec((1, de, d), lambda i, be, nb: (be[i], 0, 0))],
        out_specs=pl.BlockSpec((MOE_BLOCK, d), lambda i, be, nb: (i, 0)),
        scratch_shapes=[pltpu.VMEM((d, de), BF16), pltpu.VMEM((d, de), BF16),
                        pltpu.VMEM((de, d), BF16)])
    return pl.pallas_call(
        _moe_kernel, grid_spec=grid_spec,
        out_shape=jax.ShapeDtypeStruct((n_slots, d), BF16),
        compiler_params=_cparams("arbitrary"),
    )(blk_e, nb_used, xb, w_gate, w_up, w_down)


def _dest_slots(ids, base, tm):
    t = ids.shape[1]
    nt = t // tm
    eid = ids[0:2].reshape(2, nt, tm)
    rank = ids[2:4].reshape(2, nt, tm)
    onehot = eid[..., None] == jnp.arange(N_EXPERTS, dtype=jnp.int32)
    off = jnp.sum(jnp.where(onehot, base[None, :, None, :], 0), axis=-1)
    return (rank + off).reshape(2, t)


def _moe(h2, routed, layer, w_gate, w_up, w_down):
    t = h2.shape[0]
    counts_tiles = jnp.concatenate([cnt[:, :, 0] for _, cnt, _ in routed], axis=0).astype(jnp.int32)
    tile_first = jnp.cumsum(counts_tiles, axis=0) - counts_tiles
    counts = jnp.sum(counts_tiles, axis=0)
    padded = (counts + MOE_BLOCK - 1) // MOE_BLOCK * MOE_BLOCK
    pends = jnp.cumsum(padded)
    base = (pends - padded)[None, :] + tile_first
    dests, row = [], 0
    for ids, cnt, tm in routed:
        nt = cnt.shape[0]
        dests.append(_dest_slots(ids, base[row:row + nt], tm))
        row += nt
    dest = jnp.concatenate(dests, axis=1) if len(dests) > 1 else dests[0]
    n_blocks = -(-2 * t // MOE_BLOCK) + N_EXPERTS
    n_slots = n_blocks * MOE_BLOCK
    tok = jnp.tile(jnp.arange(t, dtype=jnp.int32), 2)
    slot_tok = jnp.zeros((n_slots,), jnp.int32).at[dest.reshape(-1)].set(
        tok, unique_indices=True, mode='promise_in_bounds')
    blk_start = jnp.arange(n_blocks, dtype=jnp.int32) * MOE_BLOCK
    blk_e = jnp.minimum(jnp.sum((pends[None, :] <= blk_start[:, None]).astype(jnp.int32), axis=1),
                        N_EXPERTS - 1) + layer * N_EXPERTS
    nb_used = (pends[-1:] // MOE_BLOCK).astype(jnp.int32)
    xb = h2.at[slot_tok].get(mode='promise_in_bounds')
    yb = _moe_experts(xb, blk_e, nb_used, w_gate, w_up, w_down)
    return [(yb.at[d[0]].get(mode='promise_in_bounds'), yb.at[d[1]].get(mode='promise_in_bounds'))
            for d in dests]


def _final_kernel(x_ref, ya_ref, yb_ref, gt_ref, mv_ref, g_ref, o_ref):
    x = _residual_in(x_ref, (ya_ref, yb_ref, gt_ref), mv_ref[...])
    o_ref[...] = _rms(x, g_ref[...])


def _final(x, comb, modv, g, batch, tm):
    t, d = x.shape
    tpb = t // batch // tm
    tok = pl.BlockSpec((tm, d), lambda i: (i, 0))
    return pl.pallas_call(
        _final_kernel, grid=(t // tm,),
        in_specs=[tok, tok, tok, pl.BlockSpec((tm, 2), lambda i: (i, 0)),
                  pl.BlockSpec((None, 8, d), lambda i: (i // tpb, 0, 0)), _full((1, d))],
        out_specs=tok, out_shape=jax.ShapeDtypeStruct((t, d), F32),
        compiler_params=_cparams("parallel"),
    )(x, *comb, modv, g)


def _prep_a(w_in, b_gate):
    d = w_in.shape[0]
    wqt = w_in[:, :NQ].T.astype(BF16)
    wk = w_in[:, NQ:2 * NQ].astype(BF16)
    wvt = w_in[:, 2 * NQ:2 * NQ + NV].T.astype(BF16)
    wot = w_in[:, 2 * NQ + NV:2 * NQ + NV + d].T.astype(BF16)
    perm = jnp.array([0, 1, 2, 3, 8, 9, 10, 11, 4, 5, 6, 7, 12, 13, 14, 15], jnp.int32)
    wgt = w_in[:, 2 * NQ + NV + d:].T[perm].astype(BF16)
    bg = b_gate.astype(F32)[perm][:, None]
    return wqt, wk, wvt, wot, wgt, bg


def _prep_route(w_group, b_group, w_router, b_router, tm):
    d = w_group.shape[0]
    pad = ROUTE_ROWS - N_EXPERTS - N_GROUPS
    wt = jnp.concatenate([w_router.T, w_group.T, jnp.zeros((pad, d), F32)], axis=0).astype(F32)
    hi = wt.astype(BF16)
    lo = (wt - hi.astype(F32)).astype(BF16)
    rb = jnp.concatenate([b_router, b_group, jnp.zeros((pad,), F32)]).astype(F32)[:, None]
    tri = jnp.triu(jnp.ones((tm, tm), BF16), k=1)
    return hi, lo, rb, tri


def _modv(mod, l, rows):
    depth = mod.shape[0]
    d = mod.shape[-1] // 6
    zero = jnp.zeros((len(rows), 1, d), F32)
    cur = jnp.stack([mod[l, r].reshape(6, d) for r in rows]) if l < depth else jnp.zeros((len(rows), 6, d), F32)
    prev = jnp.stack([mod[l - 1, r].reshape(6, d)[5:6] for r in rows]) if l > 0 else zero
    return jnp.concatenate([cur, prev, zero], axis=1)


def kernel(x, c, ctx, c_ctx, mod_w, mod_b, norm_g, final_g, a_w_in, a_b_gate, a_head_g, a_w_out,
           b_w_in, b_conv_w, b_w_out, moe_w_group, moe_b_group, moe_w_router, moe_b_router,
           moe_w_gate, moe_w_up, moe_w_down):
    batch, seq, d = x.shape
    n_ctx = ctx.shape[1]
    depth = mod_w.shape[0]
    assert batch + 1 <= 8 and seq % MLSTM_CHUNK == 0 and n_ctx % MLSTM_CHUNK == 0
    tm = min(512, seq)
    t_lat = batch * seq
    t_ctx = batch * n_ctx
    assert t_lat % n_ctx == 0 and seq % GRID_W == 0

    cond = jnp.concatenate([c, c_ctx[None, :], jnp.zeros((8 - batch - 1, d), F32)], axis=0)
    mod = _modulation(cond, mod_w, mod_b)
    de = moe_w_gate.shape[-1]
    w_gate = moe_w_gate.reshape(depth * N_EXPERTS, d, de)
    w_up = moe_w_up.reshape(depth * N_EXPERTS, d, de)
    w_down = moe_w_down.reshape(depth * N_EXPERTS, de, d)

    lat = x.reshape(t_lat, d)
    cx = ctx.reshape(t_ctx, d)
    comb_lat, comb_ctx = (), ()
    for l in range(depth):
        kind, j = l % 2, l // 2
        ctx_after = any(i % 2 == 0 for i in range(l + 1, depth))
        mv_lat = _modv(mod, l, list(range(batch)))
        mv_ctx = _modv(mod, l, [batch] * batch)
        route_args = (moe_w_group[l], moe_b_group[l], moe_w_router[l], moe_b_router[l])
        route_lat = _prep_route(*route_args, tm)
        route_ctx = _prep_route(*route_args, n_ctx)
        h2_rows = t_lat + (t_ctx if ctx_after else 0)
        if kind == 0:
            wa = _prep_a(a_w_in[j], a_b_gate[j])
            state = _mlstm_zero_state(batch, MLSTM_CHUNK)
            cx, (qtc, kc, vtc, oc, actc) = _in_a(cx, comb_ctx, mv_ctx, norm_g[l], wa, batch, n_ctx)
            hfc, hbc, state = _mlstm(qtc, kc, vtc, *_gate_scans(actc, MLSTM_CHUNK), state, batch, MLSTM_CHUNK)
            lat, (qtl, kl, vtl, ol, actl) = _in_a(lat, comb_lat, mv_lat, norm_g[l], wa, batch, tm)
            hfl, hbl, _ = _mlstm(qtl, kl, vtl, *_gate_scans(actl, MLSTM_CHUNK), state, batch, MLSTM_CHUNK)
            w_out = a_w_out[j].T.astype(BF16)
            head_g = a_head_g[j].astype(F32)[:, None]
            lat, h2, idl, gtl, cntl = _out_a(hfl, hbl, ol, lat, mv_lat, norm_g[l], head_g, w_out, route_lat,
                                             batch, tm, None, h2_rows, 0)
            if ctx_after:
                cx, h2, idc, gtc, cntc = _out_a(hfc, hbc, oc, cx, mv_ctx, norm_g[l], head_g, w_out, route_ctx,
                                                batch, n_ctx, h2, h2_rows, t_lat)
        else:
            w_in = b_w_in[j].astype(BF16)
            w_out = b_w_out[j].astype(BF16)
            conv_w = b_conv_w[j].astype(F32)
            lat, h2, idl, gtl, cntl = _conv_layer(lat, comb_lat, mv_lat, norm_g[l], w_in, conv_w, w_out,
                                                  route_lat, batch, tm, GRID_W, None, h2_rows, 0)
            if ctx_after:
                cx, h2, idc, gtc, cntc = _conv_layer(cx, comb_ctx, mv_ctx, norm_g[l], w_in, conv_w, w_out,
                                                     route_ctx, batch, n_ctx, n_ctx, h2, h2_rows, t_lat)
        routed = [(idl, cntl, tm)] + ([(idc, cntc, n_ctx)] if ctx_after else [])
        outs = _moe(h2, routed, l, w_gate, w_up, w_down)
        comb_lat = (*outs[0], gtl.T)
        comb_ctx = (*outs[1], gtc.T) if ctx_after else ()
    out = _final(lat, comb_lat, _modv(mod, depth, list(range(batch))), final_g.astype(F32)[None, :], batch, tm)
    return out.reshape(batch, seq, d)
```

```python
import functools

import jax
import jax.numpy as jnp
from jax import lax
from jax.experimental import pallas as pl
from jax.experimental.pallas import tpu as pltpu

F32 = jnp.float32
BF16 = jnp.bfloat16

EPS = 1e-6
HEADS = 4
DQK = 128
DV = 256
NQ = HEADS * DQK
NV = HEADS * DV
IGATE_SOFTCAP = 15.0
GRID_W = 64
N_GROUPS = 4
EPG = 8
N_EXPERTS = N_GROUPS * EPG
ROUTE_ROWS = 40
MOE_BLOCK = 256
SORT_PAD = 16
SORT_CHUNK = 256
PIECES = MOE_BLOCK // SORT_PAD
MLSTM_CHUNK = 128
LANES = 128
VMEM_LIMIT = 52 * 1024 * 1024


def _cparams(*sem):
    return pltpu.CompilerParams(dimension_semantics=sem, vmem_limit_bytes=VMEM_LIMIT)


def _nt_dot(a, b):
    return lax.dot_general(a, b, (((1,), (1,)), ((), ())), preferred_element_type=F32)


def _rms(x, g):
    return x * lax.rsqrt(jnp.mean(x * x, axis=-1, keepdims=True) + EPS) * g


def _mod_kernel(c_ref, w_ref, b_ref, o_ref):
    c = c_ref[...]
    s = (c * jax.nn.sigmoid(c)).astype(BF16)
    o_ref[0] = jnp.dot(s, w_ref[0].astype(BF16), preferred_element_type=F32) + b_ref[0]


def _modulation(cond, mod_w, mod_b):
    depth, d, n = mod_w.shape
    tn = 1024
    return pl.pallas_call(
        _mod_kernel,
        grid=(depth, n // tn),
        in_specs=[pl.BlockSpec((8, d), lambda l, j: (0, 0)),
                  pl.BlockSpec((1, d, tn), lambda l, j: (l, 0, j)),
                  pl.BlockSpec((1, 1, tn), lambda l, j: (l, 0, j))],
        out_specs=pl.BlockSpec((1, 8, tn), lambda l, j: (l, 0, j)),
        out_shape=jax.ShapeDtypeStruct((depth, 8, n), F32),
        compiler_params=_cparams("arbitrary", "arbitrary"),
    )(cond, mod_w, mod_b.reshape(depth, 1, n))


def _residual_in(x_ref, comb_refs, mv):
    x = x_ref[...]
    if comb_refs:
        ya_ref, yb_ref, gt_ref = comb_refs
        gt = gt_ref[...]
        x = x + mv[6:7] * (gt[:, 0:1] * ya_ref[...].astype(F32) + gt[:, 1:2] * yb_ref[...].astype(F32))
    return x


def _route(x2, wr_hi_ref, wr_lo_ref, rb_ref, tri_ref, xs_ref, ids_ref, gts_ref, cnt_ref):
    tm = x2.shape[0]
    x_hi = x2.astype(BF16)
    x_lo = (x2 - x_hi.astype(F32)).astype(BF16)
    w_hi = wr_hi_ref[...]
    lg = _nt_dot(w_hi, x_hi) + _nt_dot(wr_lo_ref[...], x_hi) + _nt_dot(w_hi, x_lo)
    lg = lg + rb_ref[...]
    row = lax.broadcasted_iota(jnp.int32, (EPG, tm), 0)
    gl = lg[N_EXPERTS:N_EXPERTS + EPG]
    gl = jnp.where(row < N_GROUPS, gl, -jnp.inf)
    gmx = jnp.max(gl, axis=0, keepdims=True)
    grp = jnp.min(jnp.where(gl == gmx, row, EPG), axis=0, keepdims=True)
    p_grp = 1.0 / jnp.sum(jnp.exp(gl - gmx), axis=0, keepdims=True)
    sel = lg[0:EPG]
    for g in range(1, N_GROUPS):
        sel = jnp.where(grp == g, lg[g * EPG:(g + 1) * EPG], sel)
    mx1 = jnp.max(sel, axis=0, keepdims=True)
    i1 = jnp.min(jnp.where(sel == mx1, row, EPG), axis=0, keepdims=True)
    rest = jnp.where(row == i1, -jnp.inf, sel)
    mx2 = jnp.max(rest, axis=0, keepdims=True)
    i2 = jnp.min(jnp.where(rest == mx2, row, EPG), axis=0, keepdims=True)
    e2 = jnp.exp(mx2 - mx1)
    inv = p_grp / (1.0 + e2)
    eid1 = grp * EPG + i1
    eid2 = grp * EPG + i2
    gts_ref[0:1, :] = inv
    gts_ref[1:2, :] = inv * e2
    erow = lax.broadcasted_iota(jnp.int32, (N_EXPERTS, tm), 0)
    oh1 = erow == eid1
    oh2 = erow == eid2
    member = jnp.where(oh1, 1.0, jnp.where(oh2, 1.0, 0.0)).astype(BF16)
    before = jnp.dot(member, tri_ref[...], preferred_element_type=F32)
    cnt = jnp.dot(member, jnp.ones((tm, LANES), BF16), preferred_element_type=F32)
    cnt_ref[0] = cnt
    ids_ref[0:1, :] = eid1
    ids_ref[1:2, :] = eid2
    ids_ref[2:3, :] = jnp.sum(jnp.where(oh1, before, 0.0), axis=0, keepdims=True).astype(jnp.int32)
    ids_ref[3:4, :] = jnp.sum(jnp.where(oh2, before, 0.0), axis=0, keepdims=True).astype(jnp.int32)
    cnt_pad = jnp.floor((cnt + (SORT_PAD - 1)) * (1.0 / SORT_PAD)) * SORT_PAD
    er = lax.broadcasted_iota(jnp.int32, (N_EXPERTS, N_EXPERTS), 0)
    ec = lax.broadcasted_iota(jnp.int32, (N_EXPERTS, N_EXPERTS), 1)
    lower = jnp.where(ec < er, 1.0, 0.0).astype(BF16)
    first = jnp.dot(lower, cnt_pad.astype(BF16), preferred_element_type=F32)
    place = before + jnp.concatenate([first] * (tm // LANES), axis=1)
    pos1 = jnp.sum(jnp.where(oh1, place, 0.0), axis=0, keepdims=True).astype(jnp.int32)
    pos2 = jnp.sum(jnp.where(oh2, place, 0.0), axis=0, keepdims=True).astype(jnp.int32)
    x2b = x2.astype(BF16)
    for r0 in range(0, xs_ref.shape[0], SORT_CHUNK):
        r = lax.broadcasted_iota(jnp.int32, (SORT_CHUNK, tm), 0) + r0
        perm = jnp.where(r == pos1, 1.0, jnp.where(r == pos2, 1.0, 0.0)).astype(BF16)
        xs_ref[r0:r0 + SORT_CHUNK, :] = jnp.dot(perm, x2b, preferred_element_type=F32).astype(BF16)


def _norm2_and_route(x, mv, ng_ref, route_refs, out_refs):
    xo_ref, xs_ref, ids_ref, gts_ref, cnt_ref = out_refs
    xo_ref[...] = x
    x2 = _rms(x, ng_ref[1:2]) * (1.0 + mv[4:5]) + mv[3:4]
    _route(x2, *route_refs, xs_ref, ids_ref, gts_ref, cnt_ref)


def _full(shape):
    return pl.BlockSpec(shape, lambda i: (0,) * len(shape))


def _tile_specs(tm, d, has_comb, tiles_per_batch):
    tok = pl.BlockSpec((tm, d), lambda i: (i, 0))
    specs = [tok]
    if has_comb:
        specs += [tok, tok, pl.BlockSpec((tm, 2), lambda i: (i, 0))]
    specs.append(pl.BlockSpec((None, 8, d), lambda i: (i // tiles_per_batch, 0, 0)))
    specs.append(_full((2, d)))
    return specs


def _route_specs(d, tm):
    return [_full((ROUTE_ROWS, d)), _full((ROUTE_ROWS, d)), _full((ROUTE_ROWS, 1)), _full((tm, tm))]


def _sorted_rows(tm):
    return 2 * tm + N_EXPERTS * SORT_PAD


def _mixer_out(t, d, tm, h2_rows, h2_row0):
    sr = _sorted_rows(tm)
    assert h2_row0 % sr == 0
    off = h2_row0 // sr
    tok = pl.BlockSpec((tm, d), lambda i: (i, 0))
    specs = [tok, pl.BlockSpec((sr, d), lambda i: (i + off, 0)),
             pl.BlockSpec((4, tm), lambda i: (0, i)), pl.BlockSpec((2, tm), lambda i: (0, i)),
             pl.BlockSpec((1, N_EXPERTS, LANES), lambda i: (i, 0, 0))]
    shapes = [jax.ShapeDtypeStruct((t, d), F32), jax.ShapeDtypeStruct((h2_rows, d), BF16),
              jax.ShapeDtypeStruct((4, t), jnp.int32), jax.ShapeDtypeStruct((2, t), F32),
              jax.ShapeDtypeStruct((t // tm, N_EXPERTS, LANES), F32)]
    return specs, shapes


def _in_a_kernel(*refs, has_comb):
    n_in = 4 if has_comb else 1
    x_ref, comb, refs = refs[0], refs[1:n_in], refs[n_in:]
    mv_ref, ng_ref, wqt_ref, wk_ref, wvt_ref, wot_ref, wgt_ref, bg_ref = refs[:8]
    outs = refs[8:]
    if has_comb:
        xo_ref, outs = outs[0], outs[1:]
    qt_ref, k_ref, vt_ref, ot_ref, a_ref = outs
    mv = mv_ref[...]
    x = _residual_in(x_ref, comb, mv)
    if has_comb:
        xo_ref[...] = x
    h = (_rms(x, ng_ref[0:1]) * (1.0 + mv[1:2]) + mv[0:1]).astype(BF16)
    qt_ref[0] = (_nt_dot(wqt_ref[...], h) * (DQK ** -0.5)).astype(BF16)
    k_ref[...] = jnp.dot(h, wk_ref[...], preferred_element_type=F32).astype(BF16)
    vt_ref[0] = _nt_dot(wvt_ref[...], h).astype(BF16)
    ot_ref[0] = jax.nn.sigmoid(_nt_dot(wot_ref[...], h)).astype(BF16)
    g = _nt_dot(wgt_ref[...], h) + bg_ref[...]
    gi, gf = g[0:8], g[8:16]
    a_ref[0, 0:8, :] = IGATE_SOFTCAP * jnp.tanh(gi / IGATE_SOFTCAP)
    a_ref[0, 8:16, :] = jnp.minimum(gf, 0.0) - jnp.log(1.0 + jnp.exp(-jnp.abs(gf)))


def _in_a(x, comb, modv, ng, w, batch, tm):
    t, d = x.shape
    s = t // batch
    tpb = s // tm
    has_comb = bool(comb)
    in_specs = _tile_specs(tm, d, has_comb, tpb) + [_full(a.shape) for a in w]
    tok = lambda n: pl.BlockSpec((tm, n), lambda i: (i, 0))
    rows = lambda n: pl.BlockSpec((1, n, tm), lambda i: (i // tpb, 0, i % tpb))
    out_specs = [rows(NQ), tok(NQ), rows(NV), rows(d), rows(16)]
    out_shape = [jax.ShapeDtypeStruct((batch, NQ, s), BF16), jax.ShapeDtypeStruct((t, NQ), BF16),
                 jax.ShapeDtypeStruct((batch, NV, s), BF16), jax.ShapeDtypeStruct((batch, d, s), BF16),
                 jax.ShapeDtypeStruct((batch, 16, s), F32)]
    if has_comb:
        out_specs = [tok(d)] + out_specs
        out_shape = [jax.ShapeDtypeStruct((t, d), F32)] + out_shape
    outs = pl.pallas_call(
        functools.partial(_in_a_kernel, has_comb=has_comb),
        grid=(t // tm,), in_specs=in_specs, out_specs=out_specs, out_shape=out_shape,
        compiler_params=_cparams("parallel"),
    )(x, *comb, modv, ng, *w)
    if has_comb:
        return outs[0], outs[1:]
    return x, outs


def _chunk_scan(x, op, fill, forward):
    length = x.shape[-1]
    pos = lax.broadcasted_iota(jnp.int32, x.shape, 1)
    s = 1
    while s < length:
        if forward:
            x = op(x, jnp.where(pos >= s, pltpu.roll(x, s, axis=1), fill))
        else:
            x = op(x, jnp.where(pos < length - s, pltpu.roll(x, length - s, axis=1), fill))
        s *= 2
    return x


def _gate_kernel(a_ref, o_ref):
    for r in range(2 * HEADS):
        fwd = r < HEADS
        b = _chunk_scan(a_ref[0, 8 + r], jnp.add, 0.0, fwd)
        u = a_ref[0, r] - b
        cm = _chunk_scan(u, jnp.maximum, -jnp.inf, fwd)
        end = b.shape[-1] - 1 if fwd else 0
        o_ref[0, r] = u
        o_ref[0, 8 + r] = b
        o_ref[0, 16 + r] = cm
        o_ref[0, 24 + r] = jnp.broadcast_to(cm[:, end:end + 1], cm.shape)
        o_ref[0, 32 + r] = jnp.broadcast_to(b[:, end:end + 1], b.shape)


GATE_ROWS = 40


def _gate_scans(act, chunk):
    batch, _, s = act.shape
    nc = s // chunk
    g = pl.pallas_call(
        _gate_kernel, grid=(batch,),
        in_specs=[pl.BlockSpec((1, 16, nc, chunk), lambda b: (b, 0, 0, 0))],
        out_specs=pl.BlockSpec((1, GATE_ROWS, nc, chunk), lambda b: (b, 0, 0, 0)),
        out_shape=jax.ShapeDtypeStruct((batch, GATE_ROWS, nc, chunk), F32),
        compiler_params=_cparams("parallel"),
    )(act.reshape(batch, 16, nc, chunk))
    rows = jnp.transpose(g, (0, 2, 1, 3))
    cols = jnp.transpose(g[:, 0:8], (0, 2, 3, 1))
    return rows, cols


def _mlstm_kernel(qf_ref, kf_ref, vf_ref, rf_ref, cf_ref, qb_ref, kb_ref, vb_ref, rb_ref, cb_ref,
                  c0_ref, n0_ref, m0_ref, hf_ref, hb_ref, ct_ref, nt_ref, mt_ref,
                  c_s, n_s, m_s, *, chunk):
    step = pl.program_id(1)
    last = pl.num_programs(1) - 1

    @pl.when(step == 0)
    def _():
        c_s[...] = c0_ref[0]
        n_s[...] = n0_ref[0]
        m_s[...] = m0_ref[0]

    si = lax.broadcasted_iota(jnp.int32, (chunk, chunk), 0)
    ji = lax.broadcasted_iota(jnp.int32, (chunk, chunk), 1)
    for d, (q_ref, k_ref, v_ref, r_ref, c_ref, h_ref) in enumerate(
            ((qf_ref, kf_ref, vf_ref, rf_ref, cf_ref, hf_ref),
             (qb_ref, kb_ref, vb_ref, rb_ref, cb_ref, hb_ref))):
        mask = (si <= ji) if d == 0 else (si >= ji)
        for h in range(HEADS):
            idx = d * HEADS + h
            qt = q_ref[0, h * DQK:(h + 1) * DQK, :]
            k = k_ref[:, h * DQK:(h + 1) * DQK]
            vt = v_ref[0, h * DV:(h + 1) * DV, :]
            u_col = c_ref[0, 0, :, idx:idx + 1]
            u_row = r_ref[0, 0, idx:idx + 1, :]
            b_row = r_ref[0, 0, 8 + idx:9 + idx, :]
            cm_row = r_ref[0, 0, 16 + idx:17 + idx, :]
            cm_end = r_ref[0, 0, 24 + idx:25 + idx, :]
            b_end = r_ref[0, 0, 32 + idx:33 + idx, :]
            ct_st = c_s[idx]
            n_st = n_s[idx]
            m_st = m_s[idx]
            mm = jnp.maximum(m_st, cm_row)
            dt = jnp.where(mask, jnp.exp(u_col - mm), 0.0)
            lhs = jnp.concatenate([k, jnp.broadcast_to(n_st, (16, DQK)).astype(BF16)], axis=0)
            sx = jnp.dot(lhs, qt, preferred_element_type=F32)
            pt = sx[:chunk] * dt
            a_int = jnp.exp(m_st - mm)
            den = jnp.sum(pt, axis=0, keepdims=True) + a_int * sx[chunk:chunk + 1]
            inv = 1.0 / jnp.maximum(jnp.abs(den), jnp.exp(-b_row - mm))
            wts = jnp.concatenate([(pt * inv).astype(BF16),
                                   (qt.astype(F32) * (a_int * inv)).astype(BF16)], axis=0)
            vals = jnp.concatenate([vt, ct_st.astype(BF16)], axis=1)
            h_ref[0, h * DV:(h + 1) * DV, :] = jnp.dot(vals, wts, preferred_element_type=F32)
            mm_end = jnp.maximum(m_st, cm_end)
            a_end = jnp.exp(u_row - mm_end)
            decay = jnp.exp(m_st - mm_end)
            upd = jnp.concatenate([(vt.astype(F32) * a_end).astype(BF16),
                                   jnp.broadcast_to(a_end, (16, chunk)).astype(BF16)], axis=0)
            upd = jnp.dot(upd, k, preferred_element_type=F32)
            c_s[idx] = decay * ct_st + upd[:DV]
            n_s[idx] = decay * n_st + upd[DV:DV + 1]
            m_s[idx] = b_end + mm_end

    @pl.when(step == last)
    def _():
        ct_ref[0] = c_s[...]
        nt_ref[0] = n_s[...]
        mt_ref[0] = m_s[...]


def _mlstm_zero_state(batch, chunk):
    return (jnp.zeros((batch, 8, DV, DQK), F32), jnp.zeros((batch, 8, 1, DQK), F32),
            jnp.zeros((batch, 8, 1, chunk), F32))


def _mlstm(qt, k, vt, rows, cols, state, batch, chunk):
    t = k.shape[0]
    s = t // batch
    nc = s // chunk
    assert chunk == DQK
    c0, n0, m0 = state

    def specs(cidx):
        return [pl.BlockSpec((1, NQ, chunk), lambda b, c: (b, 0, cidx(c))),
                pl.BlockSpec((chunk, NQ), lambda b, c: (b * nc + cidx(c), 0)),
                pl.BlockSpec((1, NV, chunk), lambda b, c: (b, 0, cidx(c))),
                pl.BlockSpec((1, 1, GATE_ROWS, chunk), lambda b, c: (b, cidx(c), 0, 0)),
                pl.BlockSpec((1, 1, chunk, 8), lambda b, c: (b, cidx(c), 0, 0))]

    fwd = lambda c: c
    bwd = lambda c: nc - 1 - c
    st_specs = [pl.BlockSpec((1, 8, DV, DQK), lambda b, c: (b, 0, 0, 0)),
                pl.BlockSpec((1, 8, 1, DQK), lambda b, c: (b, 0, 0, 0)),
                pl.BlockSpec((1, 8, 1, chunk), lambda b, c: (b, 0, 0, 0))]
    st_shape = [jax.ShapeDtypeStruct(a.shape, F32) for a in state]
    h_shape = jax.ShapeDtypeStruct((batch, NV, s), F32)
    args = (qt, k, vt, rows, cols)
    hf, hb, ct, nt, mt = pl.pallas_call(
        functools.partial(_mlstm_kernel, chunk=chunk),
        grid=(batch, nc),
        in_specs=specs(fwd) + specs(bwd) + st_specs,
        out_specs=[pl.BlockSpec((1, NV, chunk), lambda b, c: (b, 0, fwd(c))),
                   pl.BlockSpec((1, NV, chunk), lambda b, c: (b, 0, bwd(c)))] + st_specs,
        out_shape=[h_shape, h_shape] + st_shape,
        scratch_shapes=[pltpu.VMEM((8, DV, DQK), F32), pltpu.VMEM((8, 1, DQK), F32),
                        pltpu.VMEM((8, 1, chunk), F32)],
        compiler_params=_cparams("arbitrary", "arbitrary"),
    )(*args, *args, c0, n0, m0)
    return hf, hb, (ct, nt, mt)


def _out_a_kernel(*refs, aliased):
    hf_ref, hb_ref, o_ref, x_ref, mv_ref, ng_ref, hg_ref, wout_ref = refs[:8]
    route_refs = refs[8:12]
    out_refs = refs[12 + int(aliased):]
    mv = mv_ref[...]
    hsum = hf_ref[0] + hb_ref[0]
    parts = []
    for h in range(HEADS):
        hh = hsum[h * DV:(h + 1) * DV]
        parts.append(hh * lax.rsqrt(jnp.mean(hh * hh, axis=0, keepdims=True) + EPS))
    hn = jnp.concatenate(parts, axis=0) * hg_ref[...] * o_ref[0].astype(F32)
    y = jnp.dot(wout_ref[...], hn.astype(BF16), preferred_element_type=F32).T
    x = x_ref[...] + mv[2:3] * y
    _norm2_and_route(x, mv, ng_ref, route_refs, out_refs)


def _out_a(hf, hb, o, x, modv, ng, head_g, w_out_t, route_w, batch, tm, h2_buf, h2_rows, h2_row0):
    t, d = x.shape
    tpb = t // batch // tm
    rows = lambda n: pl.BlockSpec((1, n, tm), lambda i: (i // tpb, 0, i % tpb))
    in_specs = ([rows(NV), rows(NV), rows(d)] + _tile_specs(tm, d, False, tpb)
                + [_full((NV, 1)), _full(w_out_t.shape)] + _route_specs(d, tm))
    w_out = w_out_t
    args = [hf, hb, o, x, modv, ng, head_g, w_out, *route_w]
    return _mixer_call(functools.partial(_out_a_kernel, aliased=h2_buf is not None),
                       in_specs, args, t, d, tm, h2_buf, h2_rows, h2_row0)


def _mixer_call(body, in_specs, args, t, d, tm, h2_buf, h2_rows, h2_row0):
    out_specs, out_shape = _mixer_out(t, d, tm, h2_rows, h2_row0)
    aliases = {}
    if h2_buf is not None:
        aliases = {len(args): 1}
        in_specs = in_specs + [pl.BlockSpec(memory_space=pl.ANY)]
        args = args + [h2_buf]
    return pl.pallas_call(
        body, grid=(t // tm,), in_specs=in_specs, out_specs=out_specs, out_shape=out_shape,
        input_output_aliases=aliases, compiler_params=_cparams("parallel"),
    )(*args)


def _conv_kernel(*refs, has_comb, row_w, aliased):
    n_in = 4 if has_comb else 1
    x_ref, comb, refs = refs[0], refs[1:n_in], refs[n_in:]
    mv_ref, ng_ref, win_ref, cw_ref, wout_ref = refs[:5]
    route_refs = refs[5:9]
    out_refs = refs[9 + int(aliased):]
    mv = mv_ref[...]
    x = _residual_in(x_ref, comb, mv)
    d = x.shape[1]
    h = (_rms(x, ng_ref[0:1]) * (1.0 + mv[1:2]) + mv[0:1]).astype(BF16)
    p = jnp.dot(h, win_ref[...], preferred_element_type=F32)
    bg = p[:, :d]
    z = p[:, d:2 * d] * p[:, 2 * d:]
    tm = z.shape[0]
    pos = lax.broadcasted_iota(jnp.int32, z.shape, 0) % row_w
    left = jnp.where(pos == 0, 0.0, pltpu.roll(z, 1, axis=0))
    right = jnp.where(pos == row_w - 1, 0.0, pltpu.roll(z, tm - 1, axis=0))
    cw = cw_ref[...]
    y = cw[0:1] * left + cw[1:2] * z + cw[2:3] * right
    yl = jnp.dot((bg * y).astype(BF16), wout_ref[...], preferred_element_type=F32)
    x = x + mv[2:3] * yl
    _norm2_and_route(x, mv, ng_ref, route_refs, out_refs)


def _conv_layer(x, comb, modv, ng, w_in, conv_w, w_out, route_w, batch, tm, row_w, h2_buf, h2_rows, h2_row0):
    t, d = x.shape
    tpb = t // batch // tm
    in_specs = (_tile_specs(tm, d, bool(comb), tpb)
                + [_full(w_in.shape), _full(conv_w.shape), _full(w_out.shape)] + _route_specs(d, tm))
    args = [x, *comb, modv, ng, w_in, conv_w, w_out, *route_w]
    body = functools.partial(_conv_kernel, has_comb=bool(comb), row_w=row_w, aliased=h2_buf is not None)
    return _mixer_call(body, in_specs, args, t, d, tm, h2_buf, h2_rows, h2_row0)


def _moe_kernel(be_ref, nb_ref, src_ref, xs_ref, wg_ref, wu_ref, wd_ref, y_ref,
                xbuf, sem, wg_s, wu_s, wd_s):
    i = pl.program_id(0)
    nb = nb_ref[0]

    def piece_copies(blk, slot):
        copies = []
        for j in range(PIECES):
            src = pl.multiple_of(src_ref[blk * PIECES + j], SORT_PAD)
            copies.append(pltpu.make_async_copy(
                xs_ref.at[pl.ds(src, SORT_PAD), :],
                xbuf.at[slot, pl.ds(j * SORT_PAD, SORT_PAD), :], sem.at[slot]))
        return copies

    @pl.when(i < nb)
    def _():
        slot = i % 2

        @pl.when(i == 0)
        def _():
            for c in piece_copies(0, 0):
                c.start()

        @pl.when(i + 1 < nb)
        def _():
            for c in piece_copies(i + 1, 1 - slot):
                c.start()

        prev = be_ref[jnp.maximum(i - 1, 0)]

        @pl.when((i == 0) | (be_ref[i] != prev))
        def _():
            wg_s[...] = wg_ref[0].astype(BF16)
            wu_s[...] = wu_ref[0].astype(BF16)
            wd_s[...] = wd_ref[0].astype(BF16)

        for c in piece_copies(i, slot):
            c.wait()
        x = xbuf[slot]
        g = jnp.dot(x, wg_s[...], preferred_element_type=F32)
        u = jnp.dot(x, wu_s[...], preferred_element_type=F32)
        a = (g * jax.nn.sigmoid(g) * u).astype(BF16)
        y_ref[...] = jnp.dot(a, wd_s[...], preferred_element_type=F32).astype(BF16)

    @pl.when(i >= nb)
    def _():
        y_ref[...] = jnp.zeros(y_ref.shape, y_ref.dtype)


def _moe_experts(xs, blk_e, nb_used, piece_src, w_gate, w_up, w_down):
    d = xs.shape[1]
    de = w_gate.shape[-1]
    n_blocks = blk_e.shape[0]
    grid_spec = pltpu.PrefetchScalarGridSpec(
        num_scalar_prefetch=3, grid=(n_blocks,),
        in_specs=[pl.BlockSpec(memory_space=pl.ANY),
                  pl.BlockSpec((1, d, de), lambda i, be, nb, src: (be[i], 0, 0)),
                  pl.BlockSpec((1, d, de), lambda i, be, nb, src: (be[i], 0, 0)),
                  pl.BlockSpec((1, de, d), lambda i, be, nb, src: (be[i], 0, 0))],
        out_specs=pl.BlockSpec((MOE_BLOCK, d), lambda i, be, nb, src: (i, 0)),
        scratch_shapes=[pltpu.VMEM((2, MOE_BLOCK, d), BF16), pltpu.SemaphoreType.DMA((2,)),
                        pltpu.VMEM((d, de), BF16), pltpu.VMEM((d, de), BF16),
                        pltpu.VMEM((de, d), BF16)])
    return pl.pallas_call(
        _moe_kernel, grid_spec=grid_spec,
        out_shape=jax.ShapeDtypeStruct((n_blocks * MOE_BLOCK, d), BF16),
        compiler_params=_cparams("arbitrary"),
    )(blk_e, nb_used, piece_src, xs, w_gate, w_up, w_down)


def _dest_slots(ids, base, tm):
    t = ids.shape[1]
    nt = t // tm
    eid = ids[0:2].reshape(2, nt, tm)
    rank = ids[2:4].reshape(2, nt, tm)
    onehot = eid[..., None] == jnp.arange(N_EXPERTS, dtype=jnp.int32)
    off = jnp.sum(jnp.where(onehot, base[None, :, None, :], 0), axis=-1)
    return (rank + off).reshape(2, t)


def _moe(xs, routed, layer, w_gate, w_up, w_down):
    cnt_tiles = jnp.concatenate([cnt[:, :, 0] for _, cnt, _ in routed], axis=0).astype(jnp.int32)
    cnt_pad = (cnt_tiles + SORT_PAD - 1) // SORT_PAD * SORT_PAD
    run_first = jnp.cumsum(cnt_pad, axis=0) - cnt_pad
    in_tile = jnp.cumsum(cnt_pad, axis=1) - cnt_pad
    region = jnp.sum(cnt_pad, axis=0)
    padded = (region + MOE_BLOCK - 1) // MOE_BLOCK * MOE_BLOCK
    pends = jnp.cumsum(padded)
    pstart = pends - padded
    base = pstart[None, :] + run_first
    dests, tile_row0, row, xrow, worst = [], [], 0, 0, 0
    for ids, cnt, tm in routed:
        nt = cnt.shape[0]
        dests.append(_dest_slots(ids, base[row:row + nt], tm))
        tile_row0.append(xrow + jnp.arange(nt, dtype=jnp.int32) * _sorted_rows(tm))
        row += nt
        xrow += nt * _sorted_rows(tm)
        worst += nt * (2 * tm + N_EXPERTS * (SORT_PAD - 1))
    tile_row0 = jnp.concatenate(tile_row0)
    n_blocks = -(-(worst + N_EXPERTS * (MOE_BLOCK - 1)) // MOE_BLOCK)
    blk_start = jnp.arange(n_blocks, dtype=jnp.int32) * MOE_BLOCK
    blk_x = jnp.minimum(jnp.sum((pends[None, :] <= blk_start[:, None]).astype(jnp.int32), axis=1),
                        N_EXPERTS - 1)
    nb_used = (pends[-1:] // MOE_BLOCK).astype(jnp.int32)
    piece_e = jnp.repeat(blk_x, PIECES)
    piece_rank = jnp.arange(n_blocks * PIECES, dtype=jnp.int32) * SORT_PAD - pstart[piece_e]
    first_e = run_first.T[piece_e]
    piece_tile = jnp.sum((first_e <= piece_rank[:, None]).astype(jnp.int32), axis=1) - 1
    flat = piece_tile * N_EXPERTS + piece_e
    src = tile_row0[piece_tile] + in_tile.reshape(-1)[flat] + piece_rank - run_first.reshape(-1)[flat]
    piece_src = jnp.where(piece_rank < region[piece_e], src, 0).astype(jnp.int32)
    yb = _moe_experts(xs, blk_x + layer * N_EXPERTS, nb_used, piece_src, w_gate, w_up, w_down)
    return [(yb.at[d[0]].get(mode='promise_in_bounds'), yb.at[d[1]].get(mode='promise_in_bounds'))
            for d in dests]


def _final_kernel(x_ref, ya_ref, yb_ref, gt_ref, mv_ref, g_ref, o_ref):
    x = _residual_in(x_ref, (ya_ref, yb_ref, gt_ref), mv_ref[...])
    o_ref[...] = _rms(x, g_ref[...])


def _final(x, comb, modv, g, batch, tm):
    t, d = x.shape
    tpb = t // batch // tm
    tok = pl.BlockSpec((tm, d), lambda i: (i, 0))
    return pl.pallas_call(
        _final_kernel, grid=(t // tm,),
        in_specs=[tok, tok, tok, pl.BlockSpec((tm, 2), lambda i: (i, 0)),
                  pl.BlockSpec((None, 8, d), lambda i: (i // tpb, 0, 0)), _full((1, d))],
        out_specs=tok, out_shape=jax.ShapeDtypeStruct((t, d), F32),
        compiler_params=_cparams("parallel"),
    )(x, *comb, modv, g)


def _prep_a(w_in, b_gate):
    d = w_in.shape[0]
    wqt = w_in[:, :NQ].T.astype(BF16)
    wk = w_in[:, NQ:2 * NQ].astype(BF16)
    wvt = w_in[:, 2 * NQ:2 * NQ + NV].T.astype(BF16)
    wot = w_in[:, 2 * NQ + NV:2 * NQ + NV + d].T.astype(BF16)
    perm = jnp.array([0, 1, 2, 3, 8, 9, 10, 11, 4, 5, 6, 7, 12, 13, 14, 15], jnp.int32)
    wgt = w_in[:, 2 * NQ + NV + d:].T[perm].astype(BF16)
    bg = b_gate.astype(F32)[perm][:, None]
    return wqt, wk, wvt, wot, wgt, bg


def _prep_route(w_group, b_group, w_router, b_router, tm):
    d = w_group.shape[0]
    pad = ROUTE_ROWS - N_EXPERTS - N_GROUPS
    wt = jnp.concatenate([w_router.T, w_group.T, jnp.zeros((pad, d), F32)], axis=0).astype(F32)
    hi = wt.astype(BF16)
    lo = (wt - hi.astype(F32)).astype(BF16)
    rb = jnp.concatenate([b_router, b_group, jnp.zeros((pad,), F32)]).astype(F32)[:, None]
    tri = jnp.triu(jnp.ones((tm, tm), BF16), k=1)
    return hi, lo, rb, tri


def _modv(mod, l, rows):
    depth = mod.shape[0]
    d = mod.shape[-1] // 6
    zero = jnp.zeros((len(rows), 1, d), F32)
    cur = jnp.stack([mod[l, r].reshape(6, d) for r in rows]) if l < depth else jnp.zeros((len(rows), 6, d), F32)
    prev = jnp.stack([mod[l - 1, r].reshape(6, d)[5:6] for r in rows]) if l > 0 else zero
    return jnp.concatenate([cur, prev, zero], axis=1)


def kernel(x, c, ctx, c_ctx, mod_w, mod_b, norm_g, final_g, a_w_in, a_b_gate, a_head_g, a_w_out,
           b_w_in, b_conv_w, b_w_out, moe_w_group, moe_b_group, moe_w_router, moe_b_router,
           moe_w_gate, moe_w_up, moe_w_down):
    batch, seq, d = x.shape
    n_ctx = ctx.shape[1]
    depth = mod_w.shape[0]
    assert batch + 1 <= 8 and seq % MLSTM_CHUNK == 0 and n_ctx % MLSTM_CHUNK == 0
    tm = min(512, seq)
    t_lat = batch * seq
    t_ctx = batch * n_ctx
    assert t_lat % n_ctx == 0 and seq % GRID_W == 0

    cond = jnp.concatenate([c, c_ctx[None, :], jnp.zeros((8 - batch - 1, d), F32)], axis=0)
    mod = _modulation(cond, mod_w, mod_b)
    de = moe_w_gate.shape[-1]
    w_gate = moe_w_gate.reshape(depth * N_EXPERTS, d, de)
    w_up = moe_w_up.reshape(depth * N_EXPERTS, d, de)
    w_down = moe_w_down.reshape(depth * N_EXPERTS, de, d)

    lat = x.reshape(t_lat, d)
    cx = ctx.reshape(t_ctx, d)
    comb_lat, comb_ctx = (), ()
    for l in range(depth):
        kind, j = l % 2, l // 2
        ctx_after = any(i % 2 == 0 for i in range(l + 1, depth))
        mv_lat = _modv(mod, l, list(range(batch)))
        mv_ctx = _modv(mod, l, [batch] * batch)
        route_args = (moe_w_group[l], moe_b_group[l], moe_w_router[l], moe_b_router[l])
        route_lat = _prep_route(*route_args, tm)
        route_ctx = _prep_route(*route_args, n_ctx)
        lat_rows = t_lat // tm * _sorted_rows(tm)
        h2_rows = lat_rows + (t_ctx // n_ctx * _sorted_rows(n_ctx) if ctx_after else 0)
        if kind == 0:
            wa = _prep_a(a_w_in[j], a_b_gate[j])
            state = _mlstm_zero_state(batch, MLSTM_CHUNK)
            cx, (qtc, kc, vtc, oc, actc) = _in_a(cx, comb_ctx, mv_ctx, norm_g[l], wa, batch, n_ctx)
            hfc, hbc, state = _mlstm(qtc, kc, vtc, *_gate_scans(actc, MLSTM_CHUNK), state, batch, MLSTM_CHUNK)
            lat, (qtl, kl, vtl, ol, actl) = _in_a(lat, comb_lat, mv_lat, norm_g[l], wa, batch, tm)
            hfl, hbl, _ = _mlstm(qtl, kl, vtl, *_gate_scans(actl, MLSTM_CHUNK), state, batch, MLSTM_CHUNK)
            w_out = a_w_out[j].T.astype(BF16)
            head_g = a_head_g[j].astype(F32)[:, None]
            lat, h2, idl, gtl, cntl = _out_a(hfl, hbl, ol, lat, mv_lat, norm_g[l], head_g, w_out, route_lat,
                                             batch, tm, None, h2_rows, 0)
            if ctx_after:
                cx, h2, idc, gtc, cntc = _out_a(hfc, hbc, oc, cx, mv_ctx, norm_g[l], head_g, w_out, route_ctx,
                                                batch, n_ctx, h2, h2_rows, lat_rows)
        else:
            w_in = b_w_in[j].astype(BF16)
            w_out = b_w_out[j].astype(BF16)
            conv_w = b_conv_w[j].astype(F32)
            lat, h2, idl, gtl, cntl = _conv_layer(lat, comb_lat, mv_lat, norm_g[l], w_in, conv_w, w_out,
                                                  route_lat, batch, tm, GRID_W, None, h2_rows, 0)
            if ctx_after:
                cx, h2, idc, gtc, cntc = _conv_layer(cx, comb_ctx, mv_ctx, norm_g[l], w_in, conv_w, w_out,
                                                     route_ctx, batch, n_ctx, n_ctx, h2, h2_rows, lat_rows)
        routed = [(idl, cntl, tm)] + ([(idc, cntc, n_ctx)] if ctx_after else [])
        outs = _moe(h2, routed, l, w_gate, w_up, w_down)
        comb_lat = (*outs[0], gtl.T)
        comb_ctx = (*outs[1], gtc.T) if ctx_after else ()
    out = _final(lat, comb_lat, _modv(mod, depth, list(range(batch))), final_g.astype(F32)[None, :], batch, tm)
    return out.reshape(batch, seq, d)
```

```python
import functools

import jax
import jax.numpy as jnp
from jax import lax
from jax.experimental import pallas as pl
from jax.experimental.pallas import tpu as pltpu

F32 = jnp.float32
BF16 = jnp.bfloat16

EPS = 1e-6
HEADS = 4
DQK = 128
DV = 256
NQ = HEADS * DQK
NV = HEADS * DV
IGATE_SOFTCAP = 15.0
GRID_W = 64
N_GROUPS = 4
EPG = 8
N_EXPERTS = N_GROUPS * EPG
ROUTE_ROWS = 40
MOE_BLOCK = 512
GATHER_AHEAD = 2
SORT_PAD = 16
SORT_CHUNK = 256
PIECES = MOE_BLOCK // SORT_PAD
MLSTM_CHUNK = 128
LANES = 128
VMEM_LIMIT = 52 * 1024 * 1024


def _cparams(*sem):
    return pltpu.CompilerParams(dimension_semantics=sem, vmem_limit_bytes=VMEM_LIMIT)


def _nt_dot(a, b):
    return lax.dot_general(a, b, (((1,), (1,)), ((), ())), preferred_element_type=F32)


def _rms(x, g):
    return x * lax.rsqrt(jnp.mean(x * x, axis=-1, keepdims=True) + EPS) * g


def _mod_kernel(c_ref, w_ref, b_ref, o_ref):
    c = c_ref[...]
    s = (c * jax.nn.sigmoid(c)).astype(BF16)
    o_ref[0] = jnp.dot(s, w_ref[0].astype(BF16), preferred_element_type=F32) + b_ref[0]


def _modulation(cond, mod_w, mod_b):
    depth, d, n = mod_w.shape
    tn = 1024
    return pl.pallas_call(
        _mod_kernel,
        grid=(depth, n // tn),
        in_specs=[pl.BlockSpec((8, d), lambda l, j: (0, 0)),
                  pl.BlockSpec((1, d, tn), lambda l, j: (l, 0, j)),
                  pl.BlockSpec((1, 1, tn), lambda l, j: (l, 0, j))],
        out_specs=pl.BlockSpec((1, 8, tn), lambda l, j: (l, 0, j)),
        out_shape=jax.ShapeDtypeStruct((depth, 8, n), F32),
        compiler_params=_cparams("arbitrary", "arbitrary"),
    )(cond, mod_w, mod_b.reshape(depth, 1, n))


def _residual_in(x_ref, comb_refs, mv):
    x = x_ref[...]
    if comb_refs:
        ya_ref, yb_ref, gt_ref = comb_refs
        gt = gt_ref[...]
        x = x + mv[6:7] * (gt[:, 0:1] * ya_ref[...].astype(F32) + gt[:, 1:2] * yb_ref[...].astype(F32))
    return x


def _route(x2, wr_hi_ref, wr_lo_ref, rb_ref, tri_ref, xs_ref, ids_ref, gts_ref, cnt_ref):
    tm = x2.shape[0]
    x_hi = x2.astype(BF16)
    x_lo = (x2 - x_hi.astype(F32)).astype(BF16)
    w_hi = wr_hi_ref[...]
    lg = _nt_dot(w_hi, x_hi) + _nt_dot(wr_lo_ref[...], x_hi) + _nt_dot(w_hi, x_lo)
    lg = lg + rb_ref[...]
    row = lax.broadcasted_iota(jnp.int32, (EPG, tm), 0)
    gl = lg[N_EXPERTS:N_EXPERTS + EPG]
    gl = jnp.where(row < N_GROUPS, gl, -jnp.inf)
    gmx = jnp.max(gl, axis=0, keepdims=True)
    grp = jnp.min(jnp.where(gl == gmx, row, EPG), axis=0, keepdims=True)
    p_grp = 1.0 / jnp.sum(jnp.exp(gl - gmx), axis=0, keepdims=True)
    sel = lg[0:EPG]
    for g in range(1, N_GROUPS):
        sel = jnp.where(grp == g, lg[g * EPG:(g + 1) * EPG], sel)
    mx1 = jnp.max(sel, axis=0, keepdims=True)
    i1 = jnp.min(jnp.where(sel == mx1, row, EPG), axis=0, keepdims=True)
    rest = jnp.where(row == i1, -jnp.inf, sel)
    mx2 = jnp.max(rest, axis=0, keepdims=True)
    i2 = jnp.min(jnp.where(rest == mx2, row, EPG), axis=0, keepdims=True)
    e2 = jnp.exp(mx2 - mx1)
    inv = p_grp / (1.0 + e2)
    eid1 = grp * EPG + i1
    eid2 = grp * EPG + i2
    gts_ref[0:1, :] = inv
    gts_ref[1:2, :] = inv * e2
    erow = lax.broadcasted_iota(jnp.int32, (N_EXPERTS, tm), 0)
    oh1 = erow == eid1
    oh2 = erow == eid2
    member = jnp.where(oh1, 1.0, jnp.where(oh2, 1.0, 0.0)).astype(BF16)
    before = jnp.dot(member, tri_ref[...], preferred_element_type=F32)
    cnt = jnp.dot(member, jnp.ones((tm, LANES), BF16), preferred_element_type=F32)
    cnt_ref[0] = cnt
    ids_ref[0:1, :] = eid1
    ids_ref[1:2, :] = eid2
    ids_ref[2:3, :] = jnp.sum(jnp.where(oh1, before, 0.0), axis=0, keepdims=True).astype(jnp.int32)
    ids_ref[3:4, :] = jnp.sum(jnp.where(oh2, before, 0.0), axis=0, keepdims=True).astype(jnp.int32)
    cnt_pad = jnp.floor((cnt + (SORT_PAD - 1)) * (1.0 / SORT_PAD)) * SORT_PAD
    er = lax.broadcasted_iota(jnp.int32, (N_EXPERTS, N_EXPERTS), 0)
    ec = lax.broadcasted_iota(jnp.int32, (N_EXPERTS, N_EXPERTS), 1)
    lower = jnp.where(ec < er, 1.0, 0.0).astype(BF16)
    first = jnp.dot(lower, cnt_pad.astype(BF16), preferred_element_type=F32)
    place = before + jnp.concatenate([first] * (tm // LANES), axis=1)
    pos1 = jnp.sum(jnp.where(oh1, place, 0.0), axis=0, keepdims=True).astype(jnp.int32)
    pos2 = jnp.sum(jnp.where(oh2, place, 0.0), axis=0, keepdims=True).astype(jnp.int32)
    x2b = x2.astype(BF16)
    for r0 in range(0, xs_ref.shape[0], SORT_CHUNK):
        r = lax.broadcasted_iota(jnp.int32, (SORT_CHUNK, tm), 0) + r0
        perm = jnp.where(r == pos1, 1.0, jnp.where(r == pos2, 1.0, 0.0)).astype(BF16)
        xs_ref[r0:r0 + SORT_CHUNK, :] = jnp.dot(perm, x2b, preferred_element_type=F32).astype(BF16)


def _norm2_and_route(x, mv, ng_ref, route_refs, out_refs):
    xo_ref, xs_ref, ids_ref, gts_ref, cnt_ref = out_refs
    xo_ref[...] = x
    x2 = _rms(x, ng_ref[1:2]) * (1.0 + mv[4:5]) + mv[3:4]
    _route(x2, *route_refs, xs_ref, ids_ref, gts_ref, cnt_ref)


def _full(shape):
    return pl.BlockSpec(shape, lambda i: (0,) * len(shape))


def _tile_specs(tm, d, has_comb, tiles_per_batch):
    tok = pl.BlockSpec((tm, d), lambda i: (i, 0))
    specs = [tok]
    if has_comb:
        specs += [tok, tok, pl.BlockSpec((tm, 2), lambda i: (i, 0))]
    specs.append(pl.BlockSpec((None, 8, d), lambda i: (i // tiles_per_batch, 0, 0)))
    specs.append(_full((2, d)))
    return specs


def _route_specs(d, tm):
    return [_full((ROUTE_ROWS, d)), _full((ROUTE_ROWS, d)), _full((ROUTE_ROWS, 1)), _full((tm, tm))]


def _sorted_rows(tm):
    return 2 * tm + N_EXPERTS * SORT_PAD


def _mixer_out(t, d, tm, h2_rows, h2_row0):
    sr = _sorted_rows(tm)
    assert h2_row0 % sr == 0
    off = h2_row0 // sr
    tok = pl.BlockSpec((tm, d), lambda i: (i, 0))
    specs = [tok, pl.BlockSpec((sr, d), lambda i: (i + off, 0)),
             pl.BlockSpec((4, tm), lambda i: (0, i)), pl.BlockSpec((2, tm), lambda i: (0, i)),
             pl.BlockSpec((1, N_EXPERTS, LANES), lambda i: (i, 0, 0))]
    shapes = [jax.ShapeDtypeStruct((t, d), F32), jax.ShapeDtypeStruct((h2_rows, d), BF16),
              jax.ShapeDtypeStruct((4, t), jnp.int32), jax.ShapeDtypeStruct((2, t), F32),
              jax.ShapeDtypeStruct((t // tm, N_EXPERTS, LANES), F32)]
    return specs, shapes


def _in_a_kernel(*refs, has_comb):
    n_in = 4 if has_comb else 1
    x_ref, comb, refs = refs[0], refs[1:n_in], refs[n_in:]
    mv_ref, ng_ref, wqt_ref, wk_ref, wvt_ref, wot_ref, wgt_ref, bg_ref = refs[:8]
    outs = refs[8:]
    if has_comb:
        xo_ref, outs = outs[0], outs[1:]
    qt_ref, k_ref, vt_ref, ot_ref, a_ref = outs
    mv = mv_ref[...]
    x = _residual_in(x_ref, comb, mv)
    if has_comb:
        xo_ref[...] = x
    h = (_rms(x, ng_ref[0:1]) * (1.0 + mv[1:2]) + mv[0:1]).astype(BF16)
    qt_ref[0] = (_nt_dot(wqt_ref[...], h) * (DQK ** -0.5)).astype(BF16)
    k_ref[...] = jnp.dot(h, wk_ref[...], preferred_element_type=F32).astype(BF16)
    vt_ref[0] = _nt_dot(wvt_ref[...], h).astype(BF16)
    ot_ref[0] = jax.nn.sigmoid(_nt_dot(wot_ref[...], h)).astype(BF16)
    g = _nt_dot(wgt_ref[...], h) + bg_ref[...]
    gi, gf = g[0:8], g[8:16]
    a_ref[0, 0:8, :] = IGATE_SOFTCAP * jnp.tanh(gi / IGATE_SOFTCAP)
    a_ref[0, 8:16, :] = jnp.minimum(gf, 0.0) - jnp.log(1.0 + jnp.exp(-jnp.abs(gf)))


def _in_a(x, comb, modv, ng, w, batch, tm):
    t, d = x.shape
    s = t // batch
    tpb = s // tm
    has_comb = bool(comb)
    in_specs = _tile_specs(tm, d, has_comb, tpb) + [_full(a.shape) for a in w]
    tok = lambda n: pl.BlockSpec((tm, n), lambda i: (i, 0))
    rows = lambda n: pl.BlockSpec((1, n, tm), lambda i: (i // tpb, 0, i % tpb))
    out_specs = [rows(NQ), tok(NQ), rows(NV), rows(d), rows(16)]
    out_shape = [jax.ShapeDtypeStruct((batch, NQ, s), BF16), jax.ShapeDtypeStruct((t, NQ), BF16),
                 jax.ShapeDtypeStruct((batch, NV, s), BF16), jax.ShapeDtypeStruct((batch, d, s), BF16),
                 jax.ShapeDtypeStruct((batch, 16, s), F32)]
    if has_comb:
        out_specs = [tok(d)] + out_specs
        out_shape = [jax.ShapeDtypeStruct((t, d), F32)] + out_shape
    outs = pl.pallas_call(
        functools.partial(_in_a_kernel, has_comb=has_comb),
        grid=(t // tm,), in_specs=in_specs, out_specs=out_specs, out_shape=out_shape,
        compiler_params=_cparams("parallel"),
    )(x, *comb, modv, ng, *w)
    if has_comb:
        return outs[0], outs[1:]
    return x, outs


def _chunk_scan(x, op, fill, forward):
    length = x.shape[-1]
    pos = lax.broadcasted_iota(jnp.int32, x.shape, 1)
    s = 1
    while s < length:
        if forward:
            x = op(x, jnp.where(pos >= s, pltpu.roll(x, s, axis=1), fill))
        else:
            x = op(x, jnp.where(pos < length - s, pltpu.roll(x, length - s, axis=1), fill))
        s *= 2
    return x


def _gate_kernel(a_ref, o_ref):
    for r in range(2 * HEADS):
        fwd = r < HEADS
        b = _chunk_scan(a_ref[0, 8 + r], jnp.add, 0.0, fwd)
        u = a_ref[0, r] - b
        cm = _chunk_scan(u, jnp.maximum, -jnp.inf, fwd)
        end = b.shape[-1] - 1 if fwd else 0
        o_ref[0, r] = u
        o_ref[0, 8 + r] = b
        o_ref[0, 16 + r] = cm
        o_ref[0, 24 + r] = jnp.broadcast_to(cm[:, end:end + 1], cm.shape)
        o_ref[0, 32 + r] = jnp.broadcast_to(b[:, end:end + 1], b.shape)


GATE_ROWS = 40


def _gate_scans(act, chunk):
    batch, _, s = act.shape
    nc = s // chunk
    g = pl.pallas_call(
        _gate_kernel, grid=(batch,),
        in_specs=[pl.BlockSpec((1, 16, nc, chunk), lambda b: (b, 0, 0, 0))],
        out_specs=pl.BlockSpec((1, GATE_ROWS, nc, chunk), lambda b: (b, 0, 0, 0)),
        out_shape=jax.ShapeDtypeStruct((batch, GATE_ROWS, nc, chunk), F32),
        compiler_params=_cparams("parallel"),
    )(act.reshape(batch, 16, nc, chunk))
    rows = jnp.transpose(g, (0, 2, 1, 3))
    cols = jnp.transpose(g[:, 0:8], (0, 2, 3, 1))
    return rows, cols


def _mlstm_kernel(qf_ref, kf_ref, vf_ref, rf_ref, cf_ref, qb_ref, kb_ref, vb_ref, rb_ref, cb_ref,
                  c0_ref, n0_ref, m0_ref, hf_ref, hb_ref, ct_ref, nt_ref, mt_ref,
                  c_s, n_s, m_s, *, chunk):
    step = pl.program_id(1)
    last = pl.num_programs(1) - 1

    @pl.when(step == 0)
    def _():
        c_s[...] = c0_ref[0]
        n_s[...] = n0_ref[0]
        m_s[...] = m0_ref[0]

    si = lax.broadcasted_iota(jnp.int32, (chunk, chunk), 0)
    ji = lax.broadcasted_iota(jnp.int32, (chunk, chunk), 1)
    units = [(d, h, refs) for d, refs in enumerate(
        ((qf_ref, kf_ref, vf_ref, rf_ref, cf_ref, hf_ref),
         (qb_ref, kb_ref, vb_ref, rb_ref, cb_ref, hb_ref))) for h in range(HEADS)]
    scores = []
    for d, h, (q_ref, k_ref, v_ref, r_ref, c_ref, h_ref) in units:
        idx = d * HEADS + h
        qt = q_ref[0, h * DQK:(h + 1) * DQK, :]
        k = k_ref[:, h * DQK:(h + 1) * DQK]
        lhs = jnp.concatenate([k, jnp.broadcast_to(n_s[idx], (16, DQK)).astype(BF16)], axis=0)
        scores.append(jnp.dot(lhs, qt, preferred_element_type=F32))
    for (d, h, (q_ref, k_ref, v_ref, r_ref, c_ref, h_ref)), sx in zip(units, scores):
        mask = (si <= ji) if d == 0 else (si >= ji)
        idx = d * HEADS + h
        qt = q_ref[0, h * DQK:(h + 1) * DQK, :]
        k = k_ref[:, h * DQK:(h + 1) * DQK]
        vt = v_ref[0, h * DV:(h + 1) * DV, :]
        u_col = c_ref[0, 0, :, idx:idx + 1]
        u_row = r_ref[0, 0, idx:idx + 1, :]
        b_row = r_ref[0, 0, 8 + idx:9 + idx, :]
        cm_row = r_ref[0, 0, 16 + idx:17 + idx, :]
        cm_end = r_ref[0, 0, 24 + idx:25 + idx, :]
        b_end = r_ref[0, 0, 32 + idx:33 + idx, :]
        ct_st = c_s[idx]
        n_st = n_s[idx]
        m_st = m_s[idx]
        mm = jnp.maximum(m_st, cm_row)
        dt = jnp.where(mask, jnp.exp(u_col - mm), 0.0)
        pt = sx[:chunk] * dt
        a_int = jnp.exp(m_st - mm)
        den = jnp.sum(pt, axis=0, keepdims=True) + a_int * sx[chunk:chunk + 1]
        inv = 1.0 / jnp.maximum(jnp.abs(den), jnp.exp(-b_row - mm))
        wts = jnp.concatenate([(pt * inv).astype(BF16),
                               (qt.astype(F32) * (a_int * inv)).astype(BF16)], axis=0)
        vals = jnp.concatenate([vt, ct_st.astype(BF16)], axis=1)
        h_ref[0, h * DV:(h + 1) * DV, :] = jnp.dot(vals, wts, preferred_element_type=F32)
        mm_end = jnp.maximum(m_st, cm_end)
        a_end = jnp.exp(u_row - mm_end)
        decay = jnp.exp(m_st - mm_end)
        upd = jnp.concatenate([(vt.astype(F32) * a_end).astype(BF16),
                               jnp.broadcast_to(a_end, (16, chunk)).astype(BF16)], axis=0)
        upd = jnp.dot(upd, k, preferred_element_type=F32)
        c_s[idx] = decay * ct_st + upd[:DV]
        n_s[idx] = decay * n_st + upd[DV:DV + 1]
        m_s[idx] = b_end + mm_end

    @pl.when(step == last)
    def _():
        ct_ref[0] = c_s[...]
        nt_ref[0] = n_s[...]
        mt_ref[0] = m_s[...]


def _mlstm_zero_state(batch, chunk):
    return (jnp.zeros((batch, 8, DV, DQK), F32), jnp.zeros((batch, 8, 1, DQK), F32),
            jnp.zeros((batch, 8, 1, chunk), F32))


def _mlstm(qt, k, vt, rows, cols, state, batch, chunk):
    t = k.shape[0]
    s = t // batch
    nc = s // chunk
    assert chunk == DQK
    c0, n0, m0 = state

    def specs(cidx):
        return [pl.BlockSpec((1, NQ, chunk), lambda b, c: (b, 0, cidx(c))),
                pl.BlockSpec((chunk, NQ), lambda b, c: (b * nc + cidx(c), 0)),
                pl.BlockSpec((1, NV, chunk), lambda b, c: (b, 0, cidx(c))),
                pl.BlockSpec((1, 1, GATE_ROWS, chunk), lambda b, c: (b, cidx(c), 0, 0)),
                pl.BlockSpec((1, 1, chunk, 8), lambda b, c: (b, cidx(c), 0, 0))]

    fwd = lambda c: c
    bwd = lambda c: nc - 1 - c
    st_specs = [pl.BlockSpec((1, 8, DV, DQK), lambda b, c: (b, 0, 0, 0)),
                pl.BlockSpec((1, 8, 1, DQK), lambda b, c: (b, 0, 0, 0)),
                pl.BlockSpec((1, 8, 1, chunk), lambda b, c: (b, 0, 0, 0))]
    st_shape = [jax.ShapeDtypeStruct(a.shape, F32) for a in state]
    h_shape = jax.ShapeDtypeStruct((batch, NV, s), F32)
    args = (qt, k, vt, rows, cols)
    hf, hb, ct, nt, mt = pl.pallas_call(
        functools.partial(_mlstm_kernel, chunk=chunk),
        grid=(batch, nc),
        in_specs=specs(fwd) + specs(bwd) + st_specs,
        out_specs=[pl.BlockSpec((1, NV, chunk), lambda b, c: (b, 0, fwd(c))),
                   pl.BlockSpec((1, NV, chunk), lambda b, c: (b, 0, bwd(c)))] + st_specs,
        out_shape=[h_shape, h_shape] + st_shape,
        scratch_shapes=[pltpu.VMEM((8, DV, DQK), F32), pltpu.VMEM((8, 1, DQK), F32),
                        pltpu.VMEM((8, 1, chunk), F32)],
        compiler_params=_cparams("arbitrary", "arbitrary"),
    )(*args, *args, c0, n0, m0)
    return hf, hb, (ct, nt, mt)


def _out_a_kernel(*refs, aliased):
    hf_ref, hb_ref, o_ref, x_ref, mv_ref, ng_ref, hg_ref, wout_ref = refs[:8]
    route_refs = refs[8:12]
    out_refs = refs[12 + int(aliased):]
    mv = mv_ref[...]
    hsum = hf_ref[0] + hb_ref[0]
    parts = []
    for h in range(HEADS):
        hh = hsum[h * DV:(h + 1) * DV]
        parts.append(hh * lax.rsqrt(jnp.mean(hh * hh, axis=0, keepdims=True) + EPS))
    hn = jnp.concatenate(parts, axis=0) * hg_ref[...] * o_ref[0].astype(F32)
    y = jnp.dot(wout_ref[...], hn.astype(BF16), preferred_element_type=F32).T
    x = x_ref[...] + mv[2:3] * y
    _norm2_and_route(x, mv, ng_ref, route_refs, out_refs)


def _out_a(hf, hb, o, x, modv, ng, head_g, w_out_t, route_w, batch, tm, h2_buf, h2_rows, h2_row0):
    t, d = x.shape
    tpb = t // batch // tm
    rows = lambda n: pl.BlockSpec((1, n, tm), lambda i: (i // tpb, 0, i % tpb))
    in_specs = ([rows(NV), rows(NV), rows(d)] + _tile_specs(tm, d, False, tpb)
                + [_full((NV, 1)), _full(w_out_t.shape)] + _route_specs(d, tm))
    w_out = w_out_t
    args = [hf, hb, o, x, modv, ng, head_g, w_out, *route_w]
    return _mixer_call(functools.partial(_out_a_kernel, aliased=h2_buf is not None),
                       in_specs, args, t, d, tm, h2_buf, h2_rows, h2_row0)


def _mixer_call(body, in_specs, args, t, d, tm, h2_buf, h2_rows, h2_row0):
    out_specs, out_shape = _mixer_out(t, d, tm, h2_rows, h2_row0)
    aliases = {}
    if h2_buf is not None:
        aliases = {len(args): 1}
        in_specs = in_specs + [pl.BlockSpec(memory_space=pl.ANY)]
        args = args + [h2_buf]
    return pl.pallas_call(
        body, grid=(t // tm,), in_specs=in_specs, out_specs=out_specs, out_shape=out_shape,
        input_output_aliases=aliases, compiler_params=_cparams("parallel"),
    )(*args)


def _conv_kernel(*refs, has_comb, row_w, aliased):
    n_in = 4 if has_comb else 1
    x_ref, comb, refs = refs[0], refs[1:n_in], refs[n_in:]
    mv_ref, ng_ref, win_ref, cw_ref, wout_ref = refs[:5]
    route_refs = refs[5:9]
    out_refs = refs[9 + int(aliased):]
    mv = mv_ref[...]
    x = _residual_in(x_ref, comb, mv)
    d = x.shape[1]
    h = (_rms(x, ng_ref[0:1]) * (1.0 + mv[1:2]) + mv[0:1]).astype(BF16)
    p = jnp.dot(h, win_ref[...], preferred_element_type=F32)
    bg = p[:, :d]
    z = p[:, d:2 * d] * p[:, 2 * d:]
    tm = z.shape[0]
    pos = lax.broadcasted_iota(jnp.int32, z.shape, 0) % row_w
    left = jnp.where(pos == 0, 0.0, pltpu.roll(z, 1, axis=0))
    right = jnp.where(pos == row_w - 1, 0.0, pltpu.roll(z, tm - 1, axis=0))
    cw = cw_ref[...]
    y = cw[0:1] * left + cw[1:2] * z + cw[2:3] * right
    yl = jnp.dot((bg * y).astype(BF16), wout_ref[...], preferred_element_type=F32)
    x = x + mv[2:3] * yl
    _norm2_and_route(x, mv, ng_ref, route_refs, out_refs)


def _conv_layer(x, comb, modv, ng, w_in, conv_w, w_out, route_w, batch, tm, row_w, h2_buf, h2_rows, h2_row0):
    t, d = x.shape
    tpb = t // batch // tm
    in_specs = (_tile_specs(tm, d, bool(comb), tpb)
                + [_full(w_in.shape), _full(conv_w.shape), _full(w_out.shape)] + _route_specs(d, tm))
    args = [x, *comb, modv, ng, w_in, conv_w, w_out, *route_w]
    body = functools.partial(_conv_kernel, has_comb=bool(comb), row_w=row_w, aliased=h2_buf is not None)
    return _mixer_call(body, in_specs, args, t, d, tm, h2_buf, h2_rows, h2_row0)


def _moe_kernel(be_ref, nb_ref, src_ref, xs_ref, wg_ref, wu_ref, wd_ref, y_ref,
                xbuf, sem, wg_s, wu_s, wd_s):
    i = pl.program_id(0)
    nb = nb_ref[0]
    n_slots = GATHER_AHEAD + 1

    def piece_copies(blk, slot):
        copies = []
        for j in range(PIECES):
            src = pl.multiple_of(src_ref[blk * PIECES + j], SORT_PAD)
            copies.append(pltpu.make_async_copy(
                xs_ref.at[pl.ds(src, SORT_PAD), :],
                xbuf.at[slot, pl.ds(j * SORT_PAD, SORT_PAD), :], sem.at[slot]))
        return copies

    @pl.when(i < nb)
    def _():
        slot = lax.rem(i, n_slots)

        for b in range(GATHER_AHEAD):
            @pl.when((i == 0) & (b < nb))
            def _(b=b):
                for c in piece_copies(b, b):
                    c.start()

        @pl.when(i + GATHER_AHEAD < nb)
        def _():
            for c in piece_copies(i + GATHER_AHEAD, lax.rem(i + GATHER_AHEAD, n_slots)):
                c.start()

        prev = be_ref[jnp.maximum(i - 1, 0)]

        @pl.when((i == 0) | (be_ref[i] != prev))
        def _():
            wg_s[...] = wg_ref[0].astype(BF16)
            wu_s[...] = wu_ref[0].astype(BF16)
            wd_s[...] = wd_ref[0].astype(BF16)

        for c in piece_copies(i, slot):
            c.wait()
        x = xbuf[slot]
        g = jnp.dot(x, wg_s[...], preferred_element_type=F32)
        u = jnp.dot(x, wu_s[...], preferred_element_type=F32)
        a = (g * jax.nn.sigmoid(g) * u).astype(BF16)
        y_ref[...] = jnp.dot(a, wd_s[...], preferred_element_type=F32).astype(BF16)

    @pl.when(i >= nb)
    def _():
        y_ref[...] = jnp.zeros(y_ref.shape, y_ref.dtype)


def _moe_experts(xs, blk_e, nb_used, piece_src, w_gate, w_up, w_down):
    d = xs.shape[1]
    de = w_gate.shape[-1]
    n_blocks = blk_e.shape[0]
    grid_spec = pltpu.PrefetchScalarGridSpec(
        num_scalar_prefetch=3, grid=(n_blocks,),
        in_specs=[pl.BlockSpec(memory_space=pl.ANY),
                  pl.BlockSpec((1, d, de), lambda i, be, nb, src: (be[i], 0, 0)),
                  pl.BlockSpec((1, d, de), lambda i, be, nb, src: (be[i], 0, 0)),
                  pl.BlockSpec((1, de, d), lambda i, be, nb, src: (be[i], 0, 0))],
        out_specs=pl.BlockSpec((MOE_BLOCK, d), lambda i, be, nb, src: (i, 0)),
        scratch_shapes=[pltpu.VMEM((GATHER_AHEAD + 1, MOE_BLOCK, d), BF16),
                        pltpu.SemaphoreType.DMA((GATHER_AHEAD + 1,)),
                        pltpu.VMEM((d, de), BF16), pltpu.VMEM((d, de), BF16),
                        pltpu.VMEM((de, d), BF16)])
    return pl.pallas_call(
        _moe_kernel, grid_spec=grid_spec,
        out_shape=jax.ShapeDtypeStruct((n_blocks * MOE_BLOCK, d), BF16),
        compiler_params=_cparams("arbitrary"),
    )(blk_e, nb_used, piece_src, xs, w_gate, w_up, w_down)


def _dest_slots(ids, base, tm):
    t = ids.shape[1]
    nt = t // tm
    eid = ids[0:2].reshape(2, nt, tm)
    rank = ids[2:4].reshape(2, nt, tm)
    onehot = eid[..., None] == jnp.arange(N_EXPERTS, dtype=jnp.int32)
    off = jnp.sum(jnp.where(onehot, base[None, :, None, :], 0), axis=-1)
    return (rank + off).reshape(2, t)


def _moe(xs, routed, layer, w_gate, w_up, w_down):
    cnt_tiles = jnp.concatenate([cnt[:, :, 0] for _, cnt, _ in routed], axis=0).astype(jnp.int32)
    cnt_pad = (cnt_tiles + SORT_PAD - 1) // SORT_PAD * SORT_PAD
    run_first = jnp.cumsum(cnt_pad, axis=0) - cnt_pad
    in_tile = jnp.cumsum(cnt_pad, axis=1) - cnt_pad
    region = jnp.sum(cnt_pad, axis=0)
    padded = (region + MOE_BLOCK - 1) // MOE_BLOCK * MOE_BLOCK
    pends = jnp.cumsum(padded)
    pstart = pends - padded
    base = pstart[None, :] + run_first
    dests, tile_row0, row, xrow, worst = [], [], 0, 0, 0
    for ids, cnt, tm in routed:
        nt = cnt.shape[0]
        dests.append(_dest_slots(ids, base[row:row + nt], tm))
        tile_row0.append(xrow + jnp.arange(nt, dtype=jnp.int32) * _sorted_rows(tm))
        row += nt
        xrow += nt * _sorted_rows(tm)
        worst += nt * (2 * tm + N_EXPERTS * (SORT_PAD - 1))
    tile_row0 = jnp.concatenate(tile_row0)
    n_blocks = -(-(worst + N_EXPERTS * (MOE_BLOCK - 1)) // MOE_BLOCK)
    blk_start = jnp.arange(n_blocks, dtype=jnp.int32) * MOE_BLOCK
    blk_x = jnp.minimum(jnp.sum((pends[None, :] <= blk_start[:, None]).astype(jnp.int32), axis=1),
                        N_EXPERTS - 1)
    nb_used = (pends[-1:] // MOE_BLOCK).astype(jnp.int32)
    blk_oh = blk_x[:, None] == jnp.arange(N_EXPERTS, dtype=jnp.int32)[None, :]
    pick = lambda tab: jnp.sum(jnp.where(blk_oh[:, None, :], tab[None], 0), axis=2)
    per_piece = lambda a: jnp.repeat(a, PIECES, axis=0)
    blk_first = per_piece(pick(run_first))
    blk_src0 = per_piece(pick(in_tile - run_first) + tile_row0[None, :])
    blk_scal = per_piece(pick(jnp.stack([pstart, region])))
    piece_rank = jnp.arange(n_blocks * PIECES, dtype=jnp.int32) * SORT_PAD - blk_scal[:, 0]
    piece_tile = jnp.sum((blk_first <= piece_rank[:, None]).astype(jnp.int32), axis=1) - 1
    tile_oh = piece_tile[:, None] == jnp.arange(blk_first.shape[1], dtype=jnp.int32)[None, :]
    src = jnp.sum(jnp.where(tile_oh, blk_src0, 0), axis=1) + piece_rank
    piece_src = jnp.where(piece_rank < blk_scal[:, 1], src, 0).astype(jnp.int32)
    yb = _moe_experts(xs, blk_x + layer * N_EXPERTS, nb_used, piece_src, w_gate, w_up, w_down)
    return [(yb.at[d[0]].get(mode='promise_in_bounds'), yb.at[d[1]].get(mode='promise_in_bounds'))
            for d in dests]


def _final_kernel(x_ref, ya_ref, yb_ref, gt_ref, mv_ref, g_ref, o_ref):
    x = _residual_in(x_ref, (ya_ref, yb_ref, gt_ref), mv_ref[...])
    o_ref[...] = _rms(x, g_ref[...])


def _final(x, comb, modv, g, batch, tm):
    t, d = x.shape
    tpb = t // batch // tm
    tok = pl.BlockSpec((tm, d), lambda i: (i, 0))
    return pl.pallas_call(
        _final_kernel, grid=(t // tm,),
        in_specs=[tok, tok, tok, pl.BlockSpec((tm, 2), lambda i: (i, 0)),
                  pl.BlockSpec((None, 8, d), lambda i: (i // tpb, 0, 0)), _full((1, d))],
        out_specs=tok, out_shape=jax.ShapeDtypeStruct((t, d), F32),
        compiler_params=_cparams("parallel"),
    )(x, *comb, modv, g)


def _prep_a(w_in, b_gate):
    d = w_in.shape[0]
    wqt = w_in[:, :NQ].T.astype(BF16)
    wk = w_in[:, NQ:2 * NQ].astype(BF16)
    wvt = w_in[:, 2 * NQ:2 * NQ + NV].T.astype(BF16)
    wot = w_in[:, 2 * NQ + NV:2 * NQ + NV + d].T.astype(BF16)
    perm = jnp.array([0, 1, 2, 3, 8, 9, 10, 11, 4, 5, 6, 7, 12, 13, 14, 15], jnp.int32)
    wgt = w_in[:, 2 * NQ + NV + d:].T[perm].astype(BF16)
    bg = b_gate.astype(F32)[perm][:, None]
    return wqt, wk, wvt, wot, wgt, bg


def _prep_route(w_group, b_group, w_router, b_router, tm):
    d = w_group.shape[0]
    pad = ROUTE_ROWS - N_EXPERTS - N_GROUPS
    wt = jnp.concatenate([w_router.T, w_group.T, jnp.zeros((pad, d), F32)], axis=0).astype(F32)
    hi = wt.astype(BF16)
    lo = (wt - hi.astype(F32)).astype(BF16)
    rb = jnp.concatenate([b_router, b_group, jnp.zeros((pad,), F32)]).astype(F32)[:, None]
    tri = jnp.triu(jnp.ones((tm, tm), BF16), k=1)
    return hi, lo, rb, tri


def _modv(mod, l, rows):
    depth = mod.shape[0]
    d = mod.shape[-1] // 6
    zero = jnp.zeros((len(rows), 1, d), F32)
    cur = jnp.stack([mod[l, r].reshape(6, d) for r in rows]) if l < depth else jnp.zeros((len(rows), 6, d), F32)
    prev = jnp.stack([mod[l - 1, r].reshape(6, d)[5:6] for r in rows]) if l > 0 else zero
    return jnp.concatenate([cur, prev, zero], axis=1)


def kernel(x, c, ctx, c_ctx, mod_w, mod_b, norm_g, final_g, a_w_in, a_b_gate, a_head_g, a_w_out,
           b_w_in, b_conv_w, b_w_out, moe_w_group, moe_b_group, moe_w_router, moe_b_router,
           moe_w_gate, moe_w_up, moe_w_down):
    batch, seq, d = x.shape
    n_ctx = ctx.shape[1]
    depth = mod_w.shape[0]
    assert batch + 1 <= 8 and seq % MLSTM_CHUNK == 0 and n_ctx % MLSTM_CHUNK == 0
    tm = min(512, seq)
    t_lat = batch * seq
    t_ctx = batch * n_ctx
    assert t_lat % n_ctx == 0 and seq % GRID_W == 0

    cond = jnp.concatenate([c, c_ctx[None, :], jnp.zeros((8 - batch - 1, d), F32)], axis=0)
    mod = _modulation(cond, mod_w, mod_b)
    de = moe_w_gate.shape[-1]
    w_gate = moe_w_gate.reshape(depth * N_EXPERTS, d, de)
    w_up = moe_w_up.reshape(depth * N_EXPERTS, d, de)
    w_down = moe_w_down.reshape(depth * N_EXPERTS, de, d)

    lat = x.reshape(t_lat, d)
    cx = ctx.reshape(t_ctx, d)
    comb_lat, comb_ctx = (), ()
    for l in range(depth):
        kind, j = l % 2, l // 2
        ctx_after = any(i % 2 == 0 for i in range(l + 1, depth))
        mv_lat = _modv(mod, l, list(range(batch)))
        mv_ctx = _modv(mod, l, [batch] * batch)
        route_args = (moe_w_group[l], moe_b_group[l], moe_w_router[l], moe_b_router[l])
        route_lat = _prep_route(*route_args, tm)
        route_ctx = _prep_route(*route_args, n_ctx)
        lat_rows = t_lat // tm * _sorted_rows(tm)
        h2_rows = lat_rows + (t_ctx // n_ctx * _sorted_rows(n_ctx) if ctx_after else 0)
        if kind == 0:
            wa = _prep_a(a_w_in[j], a_b_gate[j])
            state = _mlstm_zero_state(batch, MLSTM_CHUNK)
            cx, (qtc, kc, vtc, oc, actc) = _in_a(cx, comb_ctx, mv_ctx, norm_g[l], wa, batch, n_ctx)
            hfc, hbc, state = _mlstm(qtc, kc, vtc, *_gate_scans(actc, MLSTM_CHUNK), state, batch, MLSTM_CHUNK)
            lat, (qtl, kl, vtl, ol, actl) = _in_a(lat, comb_lat, mv_lat, norm_g[l], wa, batch, tm)
            hfl, hbl, _ = _mlstm(qtl, kl, vtl, *_gate_scans(actl, MLSTM_CHUNK), state, batch, MLSTM_CHUNK)
            w_out = a_w_out[j].T.astype(BF16)
            head_g = a_head_g[j].astype(F32)[:, None]
            lat, h2, idl, gtl, cntl = _out_a(hfl, hbl, ol, lat, mv_lat, norm_g[l], head_g, w_out, route_lat,
                                             batch, tm, None, h2_rows, 0)
            if ctx_after:
                cx, h2, idc, gtc, cntc = _out_a(hfc, hbc, oc, cx, mv_ctx, norm_g[l], head_g, w_out, route_ctx,
                                                batch, n_ctx, h2, h2_rows, lat_rows)
        else:
            w_in = b_w_in[j].astype(BF16)
            w_out = b_w_out[j].astype(BF16)
            conv_w = b_conv_w[j].astype(F32)
            lat, h2, idl, gtl, cntl = _conv_layer(lat, comb_lat, mv_lat, norm_g[l], w_in, conv_w, w_out,
                                                  route_lat, batch, tm, GRID_W, None, h2_rows, 0)
            if ctx_after:
                cx, h2, idc, gtc, cntc = _conv_layer(cx, comb_ctx, mv_ctx, norm_g[l], w_in, conv_w, w_out,
                                                     route_ctx, batch, n_ctx, n_ctx, h2, h2_rows, lat_rows)
        routed = [(idl, cntl, tm)] + ([(idc, cntc, n_ctx)] if ctx_after else [])
        outs = _moe(h2, routed, l, w_gate, w_up, w_down)
        comb_lat = (*outs[0], gtl.T)
        comb_ctx = (*outs[1], gtc.T) if ctx_after else ()
    out = _final(lat, comb_lat, _modv(mod, depth, list(range(batch))), final_g.astype(F32)[None, :], batch, tm)
    return out.reshape(batch, seq, d)
```

```python
import functools

import jax
import jax.numpy as jnp
from jax import lax
from jax.experimental import pallas as pl
from jax.experimental.pallas import tpu as pltpu

F32 = jnp.float32
BF16 = jnp.bfloat16

EPS = 1e-6
HEADS = 4
DQK = 128
DV = 256
NQ = HEADS * DQK
NV = HEADS * DV
IGATE_SOFTCAP = 15.0
GRID_W = 64
N_GROUPS = 4
EPG = 8
N_EXPERTS = N_GROUPS * EPG
ROUTE_ROWS = 40
MOE_BLOCK = 512
GATHER_AHEAD = 2
MOE_GROUPS = 4
SORT_PAD = 16
SORT_CHUNK = 256
PIECES = MOE_BLOCK // SORT_PAD
MLSTM_CHUNK = 128
LANES = 128
VMEM_LIMIT = 52 * 1024 * 1024


def _cparams(*sem):
    return pltpu.CompilerParams(dimension_semantics=sem, vmem_limit_bytes=VMEM_LIMIT)


def _nt_dot(a, b):
    return lax.dot_general(a, b, (((1,), (1,)), ((), ())), preferred_element_type=F32)


def _rms(x, g):
    return x * lax.rsqrt(jnp.mean(x * x, axis=-1, keepdims=True) + EPS) * g


def _mod_kernel(c_ref, w_ref, b_ref, o_ref):
    c = c_ref[...]
    s = (c * jax.nn.sigmoid(c)).astype(BF16)
    o_ref[0] = jnp.dot(s, w_ref[0].astype(BF16), preferred_element_type=F32) + b_ref[0]


def _modulation(cond, mod_w, mod_b):
    depth, d, n = mod_w.shape
    tn = 1024
    return pl.pallas_call(
        _mod_kernel,
        grid=(depth, n // tn),
        in_specs=[pl.BlockSpec((8, d), lambda l, j: (0, 0)),
                  pl.BlockSpec((1, d, tn), lambda l, j: (l, 0, j)),
                  pl.BlockSpec((1, 1, tn), lambda l, j: (l, 0, j))],
        out_specs=pl.BlockSpec((1, 8, tn), lambda l, j: (l, 0, j)),
        out_shape=jax.ShapeDtypeStruct((depth, 8, n), F32),
        compiler_params=_cparams("arbitrary", "arbitrary"),
    )(cond, mod_w, mod_b.reshape(depth, 1, n))


def _residual_in(x_ref, comb_refs, mv):
    x = x_ref[...]
    if comb_refs:
        ya_ref, yb_ref, gt_ref = comb_refs
        gt = gt_ref[...]
        x = x + mv[6:7] * (gt[:, 0:1] * ya_ref[...].astype(F32) + gt[:, 1:2] * yb_ref[...].astype(F32))
    return x


def _route(x2, wr_hi_ref, wr_lo_ref, rb_ref, tri_ref, xs_ref, ids_ref, gts_ref, cnt_ref):
    tm = x2.shape[0]
    x_hi = x2.astype(BF16)
    x_lo = (x2 - x_hi.astype(F32)).astype(BF16)
    w_hi = wr_hi_ref[...]
    lg = _nt_dot(w_hi, x_hi) + _nt_dot(wr_lo_ref[...], x_hi) + _nt_dot(w_hi, x_lo)
    lg = lg + rb_ref[...]
    row = lax.broadcasted_iota(jnp.int32, (EPG, tm), 0)
    gl = lg[N_EXPERTS:N_EXPERTS + EPG]
    gl = jnp.where(row < N_GROUPS, gl, -jnp.inf)
    gmx = jnp.max(gl, axis=0, keepdims=True)
    grp = jnp.min(jnp.where(gl == gmx, row, EPG), axis=0, keepdims=True)
    p_grp = 1.0 / jnp.sum(jnp.exp(gl - gmx), axis=0, keepdims=True)
    sel = lg[0:EPG]
    for g in range(1, N_GROUPS):
        sel = jnp.where(grp == g, lg[g * EPG:(g + 1) * EPG], sel)
    mx1 = jnp.max(sel, axis=0, keepdims=True)
    i1 = jnp.min(jnp.where(sel == mx1, row, EPG), axis=0, keepdims=True)
    rest = jnp.where(row == i1, -jnp.inf, sel)
    mx2 = jnp.max(rest, axis=0, keepdims=True)
    i2 = jnp.min(jnp.where(rest == mx2, row, EPG), axis=0, keepdims=True)
    e2 = jnp.exp(mx2 - mx1)
    inv = p_grp / (1.0 + e2)
    eid1 = grp * EPG + i1
    eid2 = grp * EPG + i2
    gts_ref[0:1, :] = inv
    gts_ref[1:2, :] = inv * e2
    erow = lax.broadcasted_iota(jnp.int32, (N_EXPERTS, tm), 0)
    oh1 = erow == eid1
    oh2 = erow == eid2
    member = jnp.where(oh1, 1.0, jnp.where(oh2, 1.0, 0.0)).astype(BF16)
    before = jnp.dot(member, tri_ref[...], preferred_element_type=F32)
    cnt = jnp.dot(member, jnp.ones((tm, LANES), BF16), preferred_element_type=F32)
    cnt_ref[0] = cnt
    ids_ref[0:1, :] = eid1
    ids_ref[1:2, :] = eid2
    ids_ref[2:3, :] = jnp.sum(jnp.where(oh1, before, 0.0), axis=0, keepdims=True).astype(jnp.int32)
    ids_ref[3:4, :] = jnp.sum(jnp.where(oh2, before, 0.0), axis=0, keepdims=True).astype(jnp.int32)
    cnt_pad = jnp.floor((cnt + (SORT_PAD - 1)) * (1.0 / SORT_PAD)) * SORT_PAD
    er = lax.broadcasted_iota(jnp.int32, (N_EXPERTS, N_EXPERTS), 0)
    ec = lax.broadcasted_iota(jnp.int32, (N_EXPERTS, N_EXPERTS), 1)
    lower = jnp.where(ec < er, 1.0, 0.0).astype(BF16)
    first = jnp.dot(lower, cnt_pad.astype(BF16), preferred_element_type=F32)
    place = before + jnp.concatenate([first] * (tm // LANES), axis=1)
    pos1 = jnp.sum(jnp.where(oh1, place, 0.0), axis=0, keepdims=True).astype(jnp.int32)
    pos2 = jnp.sum(jnp.where(oh2, place, 0.0), axis=0, keepdims=True).astype(jnp.int32)
    x2b = x2.astype(BF16)
    for r0 in range(0, xs_ref.shape[0], SORT_CHUNK):
        r = lax.broadcasted_iota(jnp.int32, (SORT_CHUNK, tm), 0) + r0
        perm = jnp.where(r == pos1, 1.0, jnp.where(r == pos2, 1.0, 0.0)).astype(BF16)
        xs_ref[r0:r0 + SORT_CHUNK, :] = jnp.dot(perm, x2b, preferred_element_type=F32).astype(BF16)


def _norm2_and_route(x, mv, ng_ref, route_refs, out_refs):
    xo_ref, xs_ref, ids_ref, gts_ref, cnt_ref = out_refs
    xo_ref[...] = x
    x2 = _rms(x, ng_ref[1:2]) * (1.0 + mv[4:5]) + mv[3:4]
    _route(x2, *route_refs, xs_ref, ids_ref, gts_ref, cnt_ref)


def _full(shape):
    return pl.BlockSpec(shape, lambda i: (0,) * len(shape))


def _tile_specs(tm, d, has_comb, tiles_per_batch):
    tok = pl.BlockSpec((tm, d), lambda i: (i, 0))
    specs = [tok]
    if has_comb:
        specs += [tok, tok, pl.BlockSpec((tm, 2), lambda i: (i, 0))]
    specs.append(pl.BlockSpec((None, 8, d), lambda i: (i // tiles_per_batch, 0, 0)))
    specs.append(_full((2, d)))
    return specs


def _route_specs(d, tm):
    return [_full((ROUTE_ROWS, d)), _full((ROUTE_ROWS, d)), _full((ROUTE_ROWS, 1)), _full((tm, tm))]


def _sorted_rows(tm):
    return 2 * tm + N_EXPERTS * SORT_PAD


def _mixer_out(t, d, tm, h2_rows, h2_row0):
    sr = _sorted_rows(tm)
    assert h2_row0 % sr == 0
    off = h2_row0 // sr
    tok = pl.BlockSpec((tm, d), lambda i: (i, 0))
    specs = [tok, pl.BlockSpec((sr, d), lambda i: (i + off, 0)),
             pl.BlockSpec((4, tm), lambda i: (0, i)), pl.BlockSpec((2, tm), lambda i: (0, i)),
             pl.BlockSpec((1, N_EXPERTS, LANES), lambda i: (i, 0, 0))]
    shapes = [jax.ShapeDtypeStruct((t, d), F32), jax.ShapeDtypeStruct((h2_rows, d), BF16),
              jax.ShapeDtypeStruct((4, t), jnp.int32), jax.ShapeDtypeStruct((2, t), F32),
              jax.ShapeDtypeStruct((t // tm, N_EXPERTS, LANES), F32)]
    return specs, shapes


def _in_a_kernel(*refs, has_comb):
    n_in = 4 if has_comb else 1
    x_ref, comb, refs = refs[0], refs[1:n_in], refs[n_in:]
    mv_ref, ng_ref, wqt_ref, wk_ref, wvt_ref, wot_ref, wgt_ref, bg_ref = refs[:8]
    outs = refs[8:]
    if has_comb:
        xo_ref, outs = outs[0], outs[1:]
    qt_ref, k_ref, vt_ref, ot_ref, a_ref = outs
    mv = mv_ref[...]
    x = _residual_in(x_ref, comb, mv)
    if has_comb:
        xo_ref[...] = x
    h = (_rms(x, ng_ref[0:1]) * (1.0 + mv[1:2]) + mv[0:1]).astype(BF16)
    qt_ref[0] = (_nt_dot(wqt_ref[...], h) * (DQK ** -0.5)).astype(BF16)
    k_ref[...] = jnp.dot(h, wk_ref[...], preferred_element_type=F32).astype(BF16)
    vt_ref[0] = _nt_dot(wvt_ref[...], h).astype(BF16)
    ot_ref[0] = jax.nn.sigmoid(_nt_dot(wot_ref[...], h)).astype(BF16)
    g = _nt_dot(wgt_ref[...], h) + bg_ref[...]
    gi, gf = g[0:8], g[8:16]
    a_ref[0, 0:8, :] = IGATE_SOFTCAP * jnp.tanh(gi / IGATE_SOFTCAP)
    a_ref[0, 8:16, :] = jnp.minimum(gf, 0.0) - jnp.log(1.0 + jnp.exp(-jnp.abs(gf)))


def _in_a(x, comb, modv, ng, w, batch, tm):
    t, d = x.shape
    s = t // batch
    tpb = s // tm
    has_comb = bool(comb)
    in_specs = _tile_specs(tm, d, has_comb, tpb) + [_full(a.shape) for a in w]
    tok = lambda n: pl.BlockSpec((tm, n), lambda i: (i, 0))
    rows = lambda n: pl.BlockSpec((1, n, tm), lambda i: (i // tpb, 0, i % tpb))
    out_specs = [rows(NQ), tok(NQ), rows(NV), rows(d), rows(16)]
    out_shape = [jax.ShapeDtypeStruct((batch, NQ, s), BF16), jax.ShapeDtypeStruct((t, NQ), BF16),
                 jax.ShapeDtypeStruct((batch, NV, s), BF16), jax.ShapeDtypeStruct((batch, d, s), BF16),
                 jax.ShapeDtypeStruct((batch, 16, s), F32)]
    if has_comb:
        out_specs = [tok(d)] + out_specs
        out_shape = [jax.ShapeDtypeStruct((t, d), F32)] + out_shape
    outs = pl.pallas_call(
        functools.partial(_in_a_kernel, has_comb=has_comb),
        grid=(t // tm,), in_specs=in_specs, out_specs=out_specs, out_shape=out_shape,
        compiler_params=_cparams("parallel"),
    )(x, *comb, modv, ng, *w)
    if has_comb:
        return outs[0], outs[1:]
    return x, outs


def _chunk_scan(x, op, fill, forward):
    length = x.shape[-1]
    pos = lax.broadcasted_iota(jnp.int32, x.shape, 1)
    s = 1
    while s < length:
        if forward:
            x = op(x, jnp.where(pos >= s, pltpu.roll(x, s, axis=1), fill))
        else:
            x = op(x, jnp.where(pos < length - s, pltpu.roll(x, length - s, axis=1), fill))
        s *= 2
    return x


def _gate_kernel(a_ref, o_ref):
    for r in range(2 * HEADS):
        fwd = r < HEADS
        b = _chunk_scan(a_ref[0, 8 + r], jnp.add, 0.0, fwd)
        u = a_ref[0, r] - b
        cm = _chunk_scan(u, jnp.maximum, -jnp.inf, fwd)
        end = b.shape[-1] - 1 if fwd else 0
        o_ref[0, r] = u
        o_ref[0, 8 + r] = b
        o_ref[0, 16 + r] = cm
        o_ref[0, 24 + r] = jnp.broadcast_to(cm[:, end:end + 1], cm.shape)
        o_ref[0, 32 + r] = jnp.broadcast_to(b[:, end:end + 1], b.shape)


GATE_ROWS = 40


def _gate_scans(act, chunk):
    batch, _, s = act.shape
    nc = s // chunk
    g = pl.pallas_call(
        _gate_kernel, grid=(batch,),
        in_specs=[pl.BlockSpec((1, 16, nc, chunk), lambda b: (b, 0, 0, 0))],
        out_specs=pl.BlockSpec((1, GATE_ROWS, nc, chunk), lambda b: (b, 0, 0, 0)),
        out_shape=jax.ShapeDtypeStruct((batch, GATE_ROWS, nc, chunk), F32),
        compiler_params=_cparams("parallel"),
    )(act.reshape(batch, 16, nc, chunk))
    rows = jnp.transpose(g, (0, 2, 1, 3))
    cols = jnp.transpose(g[:, 0:8], (0, 2, 3, 1))
    return rows, cols


def _mlstm_kernel(qf_ref, kf_ref, vf_ref, rf_ref, cf_ref, qb_ref, kb_ref, vb_ref, rb_ref, cb_ref,
                  c0_ref, n0_ref, m0_ref, hf_ref, hb_ref, ct_ref, nt_ref, mt_ref,
                  c_s, n_s, m_s, *, chunk):
    step = pl.program_id(1)
    last = pl.num_programs(1) - 1

    @pl.when(step == 0)
    def _():
        c_s[...] = c0_ref[0]
        n_s[...] = n0_ref[0]
        m_s[...] = m0_ref[0]

    si = lax.broadcasted_iota(jnp.int32, (chunk, chunk), 0)
    ji = lax.broadcasted_iota(jnp.int32, (chunk, chunk), 1)
    units = [(d, h, refs) for d, refs in enumerate(
        ((qf_ref, kf_ref, vf_ref, rf_ref, cf_ref, hf_ref),
         (qb_ref, kb_ref, vb_ref, rb_ref, cb_ref, hb_ref))) for h in range(HEADS)]
    scores = []
    for d, h, (q_ref, k_ref, v_ref, r_ref, c_ref, h_ref) in units:
        idx = d * HEADS + h
        qt = q_ref[0, h * DQK:(h + 1) * DQK, :]
        k = k_ref[:, h * DQK:(h + 1) * DQK]
        lhs = jnp.concatenate([k, jnp.broadcast_to(n_s[idx], (16, DQK)).astype(BF16)], axis=0)
        scores.append(jnp.dot(lhs, qt, preferred_element_type=F32))
    for (d, h, (q_ref, k_ref, v_ref, r_ref, c_ref, h_ref)), sx in zip(units, scores):
        mask = (si <= ji) if d == 0 else (si >= ji)
        idx = d * HEADS + h
        qt = q_ref[0, h * DQK:(h + 1) * DQK, :]
        k = k_ref[:, h * DQK:(h + 1) * DQK]
        vt = v_ref[0, h * DV:(h + 1) * DV, :]
        u_col = c_ref[0, 0, :, idx:idx + 1]
        u_row = r_ref[0, 0, idx:idx + 1, :]
        b_row = r_ref[0, 0, 8 + idx:9 + idx, :]
        cm_row = r_ref[0, 0, 16 + idx:17 + idx, :]
        cm_end = r_ref[0, 0, 24 + idx:25 + idx, :]
        b_end = r_ref[0, 0, 32 + idx:33 + idx, :]
        ct_st = c_s[idx]
        n_st = n_s[idx]
        m_st = m_s[idx]
        mm = jnp.maximum(m_st, cm_row)
        dt = jnp.where(mask, jnp.exp(u_col - mm), 0.0)
        pt = sx[:chunk] * dt
        a_int = jnp.exp(m_st - mm)
        den = jnp.sum(pt, axis=0, keepdims=True) + a_int * sx[chunk:chunk + 1]
        inv = 1.0 / jnp.maximum(jnp.abs(den), jnp.exp(-b_row - mm))
        wts = jnp.concatenate([(pt * inv).astype(BF16),
                               (qt.astype(F32) * (a_int * inv)).astype(BF16)], axis=0)
        vals = jnp.concatenate([vt, ct_st.astype(BF16)], axis=1)
        h_ref[0, h * DV:(h + 1) * DV, :] = jnp.dot(vals, wts, preferred_element_type=F32)
        mm_end = jnp.maximum(m_st, cm_end)
        a_end = jnp.exp(u_row - mm_end)
        decay = jnp.exp(m_st - mm_end)
        upd = jnp.concatenate([(vt.astype(F32) * a_end).astype(BF16),
                               jnp.broadcast_to(a_end, (16, chunk)).astype(BF16)], axis=0)
        upd = jnp.dot(upd, k, preferred_element_type=F32)
        c_s[idx] = decay * ct_st + upd[:DV]
        n_s[idx] = decay * n_st + upd[DV:DV + 1]
        m_s[idx] = b_end + mm_end

    @pl.when(step == last)
    def _():
        ct_ref[0] = c_s[...]
        nt_ref[0] = n_s[...]
        mt_ref[0] = m_s[...]


def _mlstm_zero_state(batch, chunk):
    return (jnp.zeros((batch, 8, DV, DQK), F32), jnp.zeros((batch, 8, 1, DQK), F32),
            jnp.zeros((batch, 8, 1, chunk), F32))


def _mlstm(qt, k, vt, rows, cols, state, batch, chunk):
    t = k.shape[0]
    s = t // batch
    nc = s // chunk
    assert chunk == DQK
    c0, n0, m0 = state

    def specs(cidx):
        return [pl.BlockSpec((1, NQ, chunk), lambda b, c: (b, 0, cidx(c))),
                pl.BlockSpec((chunk, NQ), lambda b, c: (b * nc + cidx(c), 0)),
                pl.BlockSpec((1, NV, chunk), lambda b, c: (b, 0, cidx(c))),
                pl.BlockSpec((1, 1, GATE_ROWS, chunk), lambda b, c: (b, cidx(c), 0, 0)),
                pl.BlockSpec((1, 1, chunk, 8), lambda b, c: (b, cidx(c), 0, 0))]

    fwd = lambda c: c
    bwd = lambda c: nc - 1 - c
    st_specs = [pl.BlockSpec((1, 8, DV, DQK), lambda b, c: (b, 0, 0, 0)),
                pl.BlockSpec((1, 8, 1, DQK), lambda b, c: (b, 0, 0, 0)),
                pl.BlockSpec((1, 8, 1, chunk), lambda b, c: (b, 0, 0, 0))]
    st_shape = [jax.ShapeDtypeStruct(a.shape, F32) for a in state]
    h_shape = jax.ShapeDtypeStruct((batch, NV, s), F32)
    args = (qt, k, vt, rows, cols)
    hf, hb, ct, nt, mt = pl.pallas_call(
        functools.partial(_mlstm_kernel, chunk=chunk),
        grid=(batch, nc),
        in_specs=specs(fwd) + specs(bwd) + st_specs,
        out_specs=[pl.BlockSpec((1, NV, chunk), lambda b, c: (b, 0, fwd(c))),
                   pl.BlockSpec((1, NV, chunk), lambda b, c: (b, 0, bwd(c)))] + st_specs,
        out_shape=[h_shape, h_shape] + st_shape,
        scratch_shapes=[pltpu.VMEM((8, DV, DQK), F32), pltpu.VMEM((8, 1, DQK), F32),
                        pltpu.VMEM((8, 1, chunk), F32)],
        compiler_params=_cparams("arbitrary", "arbitrary"),
    )(*args, *args, c0, n0, m0)
    return hf, hb, (ct, nt, mt)


def _out_a_kernel(*refs, aliased):
    hf_ref, hb_ref, o_ref, x_ref, mv_ref, ng_ref, hg_ref, wout_ref = refs[:8]
    route_refs = refs[8:12]
    out_refs = refs[12 + int(aliased):]
    mv = mv_ref[...]
    hsum = hf_ref[0] + hb_ref[0]
    parts = []
    for h in range(HEADS):
        hh = hsum[h * DV:(h + 1) * DV]
        parts.append(hh * lax.rsqrt(jnp.mean(hh * hh, axis=0, keepdims=True) + EPS))
    hn = jnp.concatenate(parts, axis=0) * hg_ref[...] * o_ref[0].astype(F32)
    y = jnp.dot(wout_ref[...], hn.astype(BF16), preferred_element_type=F32).T
    x = x_ref[...] + mv[2:3] * y
    _norm2_and_route(x, mv, ng_ref, route_refs, out_refs)


def _out_a(hf, hb, o, x, modv, ng, head_g, w_out_t, route_w, batch, tm, h2_buf, h2_rows, h2_row0):
    t, d = x.shape
    tpb = t // batch // tm
    rows = lambda n: pl.BlockSpec((1, n, tm), lambda i: (i // tpb, 0, i % tpb))
    in_specs = ([rows(NV), rows(NV), rows(d)] + _tile_specs(tm, d, False, tpb)
                + [_full((NV, 1)), _full(w_out_t.shape)] + _route_specs(d, tm))
    w_out = w_out_t
    args = [hf, hb, o, x, modv, ng, head_g, w_out, *route_w]
    return _mixer_call(functools.partial(_out_a_kernel, aliased=h2_buf is not None),
                       in_specs, args, t, d, tm, h2_buf, h2_rows, h2_row0)


def _mixer_call(body, in_specs, args, t, d, tm, h2_buf, h2_rows, h2_row0):
    out_specs, out_shape = _mixer_out(t, d, tm, h2_rows, h2_row0)
    aliases = {}
    if h2_buf is not None:
        aliases = {len(args): 1}
        in_specs = in_specs + [pl.BlockSpec(memory_space=pl.ANY)]
        args = args + [h2_buf]
    return pl.pallas_call(
        body, grid=(t // tm,), in_specs=in_specs, out_specs=out_specs, out_shape=out_shape,
        input_output_aliases=aliases, compiler_params=_cparams("parallel"),
    )(*args)


def _conv_kernel(*refs, has_comb, row_w, aliased):
    n_in = 4 if has_comb else 1
    x_ref, comb, refs = refs[0], refs[1:n_in], refs[n_in:]
    mv_ref, ng_ref, win_ref, cw_ref, wout_ref = refs[:5]
    route_refs = refs[5:9]
    out_refs = refs[9 + int(aliased):]
    mv = mv_ref[...]
    x = _residual_in(x_ref, comb, mv)
    d = x.shape[1]
    tm = x.shape[0]
    cw = cw_ref[...]
    groups = 2 if (tm // 2) % row_w == 0 else 1
    gm = tm // groups
    xg = [x[g * gm:(g + 1) * gm] for g in range(groups)]
    hg = [(_rms(a, ng_ref[0:1]) * (1.0 + mv[1:2]) + mv[0:1]).astype(BF16) for a in xg]
    pos = lax.broadcasted_iota(jnp.int32, (gm, d), 0) % row_w

    def in_proj(h):
        return (jnp.dot(h, win_ref[:, :d], preferred_element_type=F32),
                jnp.dot(h, win_ref[:, d:2 * d], preferred_element_type=F32),
                jnp.dot(h, win_ref[:, 2 * d:], preferred_element_type=F32))

    def gate(p):
        bg, cg, u = p
        z = cg * u
        left = jnp.where(pos == 0, 0.0, pltpu.roll(z, 1, axis=0))
        right = jnp.where(pos == row_w - 1, 0.0, pltpu.roll(z, gm - 1, axis=0))
        return (bg * (cw[0:1] * left + cw[1:2] * z + cw[2:3] * right)).astype(BF16)

    pg = [in_proj(h) for h in hg]
    outs = []
    for g in range(groups):
        yl = jnp.dot(gate(pg[g]), wout_ref[...], preferred_element_type=F32)
        outs.append(xg[g] + mv[2:3] * yl)
    x = jnp.concatenate(outs, axis=0) if groups > 1 else outs[0]
    _norm2_and_route(x, mv, ng_ref, route_refs, out_refs)


def _conv_layer(x, comb, modv, ng, w_in, conv_w, w_out, route_w, batch, tm, row_w, h2_buf, h2_rows, h2_row0):
    t, d = x.shape
    tpb = t // batch // tm
    in_specs = (_tile_specs(tm, d, bool(comb), tpb)
                + [_full(w_in.shape), _full(conv_w.shape), _full(w_out.shape)] + _route_specs(d, tm))
    args = [x, *comb, modv, ng, w_in, conv_w, w_out, *route_w]
    body = functools.partial(_conv_kernel, has_comb=bool(comb), row_w=row_w, aliased=h2_buf is not None)
    return _mixer_call(body, in_specs, args, t, d, tm, h2_buf, h2_rows, h2_row0)


def _moe_kernel(be_ref, nb_ref, src_ref, xs_ref, wg_ref, wu_ref, wd_ref, y_ref,
                xbuf, sem, wg_s, wu_s, wd_s):
    i = pl.program_id(0)
    nb = nb_ref[0]
    n_slots = GATHER_AHEAD + 1

    def piece_copies(blk, slot):
        copies = []
        for j in range(PIECES):
            src = pl.multiple_of(src_ref[blk * PIECES + j], SORT_PAD)
            copies.append(pltpu.make_async_copy(
                xs_ref.at[pl.ds(src, SORT_PAD), :],
                xbuf.at[slot, pl.ds(j * SORT_PAD, SORT_PAD), :], sem.at[slot]))
        return copies

    @pl.when(i < nb)
    def _():
        slot = lax.rem(i, n_slots)

        for b in range(GATHER_AHEAD):
            @pl.when((i == 0) & (b < nb))
            def _(b=b):
                for c in piece_copies(b, b):
                    c.start()

        @pl.when(i + GATHER_AHEAD < nb)
        def _():
            for c in piece_copies(i + GATHER_AHEAD, lax.rem(i + GATHER_AHEAD, n_slots)):
                c.start()

        prev = be_ref[jnp.maximum(i - 1, 0)]

        @pl.when((i == 0) | (be_ref[i] != prev))
        def _():
            wg_s[...] = wg_ref[0].astype(BF16)
            wu_s[...] = wu_ref[0].astype(BF16)
            wd_s[...] = wd_ref[0].astype(BF16)

        pltpu.make_async_copy(xs_ref.at[pl.ds(0, MOE_BLOCK), :], xbuf.at[slot], sem.at[slot]).wait()
        gm = MOE_BLOCK // MOE_GROUPS
        gu = []
        for r in range(MOE_GROUPS):
            x = xbuf[slot, r * gm:(r + 1) * gm, :]
            gu.append((jnp.dot(x, wg_s[...], preferred_element_type=F32),
                       jnp.dot(x, wu_s[...], preferred_element_type=F32)))
        for r, (g, u) in enumerate(gu):
            a = (g * jax.nn.sigmoid(g) * u).astype(BF16)
            y_ref[r * gm:(r + 1) * gm, :] = jnp.dot(a, wd_s[...], preferred_element_type=F32).astype(BF16)

    @pl.when(i >= nb)
    def _():
        y_ref[...] = jnp.zeros(y_ref.shape, y_ref.dtype)


def _moe_experts(xs, blk_e, nb_used, piece_src, w_gate, w_up, w_down):
    d = xs.shape[1]
    de = w_gate.shape[-1]
    n_blocks = blk_e.shape[0]
    grid_spec = pltpu.PrefetchScalarGridSpec(
        num_scalar_prefetch=3, grid=(n_blocks,),
        in_specs=[pl.BlockSpec(memory_space=pl.ANY),
                  pl.BlockSpec((1, d, de), lambda i, be, nb, src: (be[i], 0, 0)),
                  pl.BlockSpec((1, d, de), lambda i, be, nb, src: (be[i], 0, 0)),
                  pl.BlockSpec((1, de, d), lambda i, be, nb, src: (be[i], 0, 0))],
        out_specs=pl.BlockSpec((MOE_BLOCK, d), lambda i, be, nb, src: (i, 0)),
        scratch_shapes=[pltpu.VMEM((GATHER_AHEAD + 1, MOE_BLOCK, d), BF16),
                        pltpu.SemaphoreType.DMA((GATHER_AHEAD + 1,)),
                        pltpu.VMEM((d, de), BF16), pltpu.VMEM((d, de), BF16),
                        pltpu.VMEM((de, d), BF16)])
    return pl.pallas_call(
        _moe_kernel, grid_spec=grid_spec,
        out_shape=jax.ShapeDtypeStruct((n_blocks * MOE_BLOCK, d), BF16),
        compiler_params=_cparams("arbitrary"),
    )(blk_e, nb_used, piece_src, xs, w_gate, w_up, w_down)


def _dest_slots(ids, base, tm):
    t = ids.shape[1]
    nt = t // tm
    eid = ids[0:2].reshape(2, nt, tm)
    rank = ids[2:4].reshape(2, nt, tm)
    onehot = eid[..., None] == jnp.arange(N_EXPERTS, dtype=jnp.int32)
    off = jnp.sum(jnp.where(onehot, base[None, :, None, :], 0), axis=-1)
    return (rank + off).reshape(2, t)


def _moe(xs, routed, layer, w_gate, w_up, w_down):
    cnt_tiles = jnp.concatenate([cnt[:, :, 0] for _, cnt, _ in routed], axis=0).astype(jnp.int32)
    cnt_pad = (cnt_tiles + SORT_PAD - 1) // SORT_PAD * SORT_PAD
    run_first = jnp.cumsum(cnt_pad, axis=0) - cnt_pad
    in_tile = jnp.cumsum(cnt_pad, axis=1) - cnt_pad
    region = jnp.sum(cnt_pad, axis=0)
    padded = (region + MOE_BLOCK - 1) // MOE_BLOCK * MOE_BLOCK
    pends = jnp.cumsum(padded)
    pstart = pends - padded
    base = pstart[None, :] + run_first
    dests, tile_row0, row, xrow, worst = [], [], 0, 0, 0
    for ids, cnt, tm in routed:
        nt = cnt.shape[0]
        dests.append(_dest_slots(ids, base[row:row + nt], tm))
        tile_row0.append(xrow + jnp.arange(nt, dtype=jnp.int32) * _sorted_rows(tm))
        row += nt
        xrow += nt * _sorted_rows(tm)
        worst += nt * (2 * tm + N_EXPERTS * (SORT_PAD - 1))
    tile_row0 = jnp.concatenate(tile_row0)
    n_blocks = -(-(worst + N_EXPERTS * (MOE_BLOCK - 1)) // MOE_BLOCK)
    blk_start = jnp.arange(n_blocks, dtype=jnp.int32) * MOE_BLOCK
    blk_x = jnp.minimum(jnp.sum((pends[None, :] <= blk_start[:, None]).astype(jnp.int32), axis=1),
                        N_EXPERTS - 1)
    nb_used = (pends[-1:] // MOE_BLOCK).astype(jnp.int32)
    blk_oh = blk_x[:, None] == jnp.arange(N_EXPERTS, dtype=jnp.int32)[None, :]
    pick = lambda tab: jnp.sum(jnp.where(blk_oh[:, None, :], tab[None], 0), axis=2)
    per_piece = lambda a: jnp.repeat(a, PIECES, axis=0)
    blk_first = per_piece(pick(run_first))
    blk_src0 = per_piece(pick(in_tile - run_first) + tile_row0[None, :])
    blk_scal = per_piece(pick(jnp.stack([pstart, region])))
    piece_rank = jnp.arange(n_blocks * PIECES, dtype=jnp.int32) * SORT_PAD - blk_scal[:, 0]
    piece_tile = jnp.sum((blk_first <= piece_rank[:, None]).astype(jnp.int32), axis=1) - 1
    tile_oh = piece_tile[:, None] == jnp.arange(blk_first.shape[1], dtype=jnp.int32)[None, :]
    src = jnp.sum(jnp.where(tile_oh, blk_src0, 0), axis=1) + piece_rank
    piece_src = jnp.where(piece_rank < blk_scal[:, 1], src, 0).astype(jnp.int32)
    yb = _moe_experts(xs, blk_x + layer * N_EXPERTS, nb_used, piece_src, w_gate, w_up, w_down)
    return [(yb.at[d[0]].get(mode='promise_in_bounds'), yb.at[d[1]].get(mode='promise_in_bounds'))
            for d in dests]


def _final_kernel(x_ref, ya_ref, yb_ref, gt_ref, mv_ref, g_ref, o_ref):
    x = _residual_in(x_ref, (ya_ref, yb_ref, gt_ref), mv_ref[...])
    o_ref[...] = _rms(x, g_ref[...])


def _final(x, comb, modv, g, batch, tm):
    t, d = x.shape
    tpb = t // batch // tm
    tok = pl.BlockSpec((tm, d), lambda i: (i, 0))
    return pl.pallas_call(
        _final_kernel, grid=(t // tm,),
        in_specs=[tok, tok, tok, pl.BlockSpec((tm, 2), lambda i: (i, 0)),
                  pl.BlockSpec((None, 8, d), lambda i: (i // tpb, 0, 0)), _full((1, d))],
        out_specs=tok, out_shape=jax.ShapeDtypeStruct((t, d), F32),
        compiler_params=_cparams("parallel"),
    )(x, *comb, modv, g)


def _prep_a(w_in, b_gate):
    d = w_in.shape[0]
    wqt = w_in[:, :NQ].T.astype(BF16)
    wk = w_in[:, NQ:2 * NQ].astype(BF16)
    wvt = w_in[:, 2 * NQ:2 * NQ + NV].T.astype(BF16)
    wot = w_in[:, 2 * NQ + NV:2 * NQ + NV + d].T.astype(BF16)
    perm = jnp.array([0, 1, 2, 3, 8, 9, 10, 11, 4, 5, 6, 7, 12, 13, 14, 15], jnp.int32)
    wgt = w_in[:, 2 * NQ + NV + d:].T[perm].astype(BF16)
    bg = b_gate.astype(F32)[perm][:, None]
    return wqt, wk, wvt, wot, wgt, bg


def _prep_route(w_group, b_group, w_router, b_router, tm):
    d = w_group.shape[0]
    pad = ROUTE_ROWS - N_EXPERTS - N_GROUPS
    wt = jnp.concatenate([w_router.T, w_group.T, jnp.zeros((pad, d), F32)], axis=0).astype(F32)
    hi = wt.astype(BF16)
    lo = (wt - hi.astype(F32)).astype(BF16)
    rb = jnp.concatenate([b_router, b_group, jnp.zeros((pad,), F32)]).astype(F32)[:, None]
    tri = jnp.triu(jnp.ones((tm, tm), BF16), k=1)
    return hi, lo, rb, tri


def _modv(mod, l, rows):
    depth = mod.shape[0]
    d = mod.shape[-1] // 6
    zero = jnp.zeros((len(rows), 1, d), F32)
    cur = jnp.stack([mod[l, r].reshape(6, d) for r in rows]) if l < depth else jnp.zeros((len(rows), 6, d), F32)
    prev = jnp.stack([mod[l - 1, r].reshape(6, d)[5:6] for r in rows]) if l > 0 else zero
    return jnp.concatenate([cur, prev, zero], axis=1)


def kernel(x, c, ctx, c_ctx, mod_w, mod_b, norm_g, final_g, a_w_in, a_b_gate, a_head_g, a_w_out,
           b_w_in, b_conv_w, b_w_out, moe_w_group, moe_b_group, moe_w_router, moe_b_router,
           moe_w_gate, moe_w_up, moe_w_down):
    batch, seq, d = x.shape
    n_ctx = ctx.shape[1]
    depth = mod_w.shape[0]
    assert batch + 1 <= 8 and seq % MLSTM_CHUNK == 0 and n_ctx % MLSTM_CHUNK == 0
    tm = min(512, seq)
    t_lat = batch * seq
    t_ctx = batch * n_ctx
    assert t_lat % n_ctx == 0 and seq % GRID_W == 0

    cond = jnp.concatenate([c, c_ctx[None, :], jnp.zeros((8 - batch - 1, d), F32)], axis=0)
    mod = _modulation(cond, mod_w, mod_b)
    de = moe_w_gate.shape[-1]
    w_gate = moe_w_gate.reshape(depth * N_EXPERTS, d, de)
    w_up = moe_w_up.reshape(depth * N_EXPERTS, d, de)
    w_down = moe_w_down.reshape(depth * N_EXPERTS, de, d)

    lat = x.reshape(t_lat, d)
    cx = ctx.reshape(t_ctx, d)
    comb_lat, comb_ctx = (), ()
    for l in range(depth):
        kind, j = l % 2, l // 2
        ctx_after = any(i % 2 == 0 for i in range(l + 1, depth))
        mv_lat = _modv(mod, l, list(range(batch)))
        mv_ctx = _modv(mod, l, [batch] * batch)
        route_args = (moe_w_group[l], moe_b_group[l], moe_w_router[l], moe_b_router[l])
        route_lat = _prep_route(*route_args, tm)
        route_ctx = _prep_route(*route_args, n_ctx)
        lat_rows = t_lat // tm * _sorted_rows(tm)
        h2_rows = lat_rows + (t_ctx // n_ctx * _sorted_rows(n_ctx) if ctx_after else 0)
        if kind == 0:
            wa = _prep_a(a_w_in[j], a_b_gate[j])
            state = _mlstm_zero_state(batch, MLSTM_CHUNK)
            cx, (qtc, kc, vtc, oc, actc) = _in_a(cx, comb_ctx, mv_ctx, norm_g[l], wa, batch, n_ctx)
            hfc, hbc, state = _mlstm(qtc, kc, vtc, *_gate_scans(actc, MLSTM_CHUNK), state, batch, MLSTM_CHUNK)
            lat, (qtl, kl, vtl, ol, actl) = _in_a(lat, comb_lat, mv_lat, norm_g[l], wa, batch, tm)
            hfl, hbl, _ = _mlstm(qtl, kl, vtl, *_gate_scans(actl, MLSTM_CHUNK), state, batch, MLSTM_CHUNK)
            w_out = a_w_out[j].T.astype(BF16)
            head_g = a_head_g[j].astype(F32)[:, None]
            lat, h2, idl, gtl, cntl = _out_a(hfl, hbl, ol, lat, mv_lat, norm_g[l], head_g, w_out, route_lat,
                                             batch, tm, None, h2_rows, 0)
            if ctx_after:
                cx, h2, idc, gtc, cntc = _out_a(hfc, hbc, oc, cx, mv_ctx, norm_g[l], head_g, w_out, route_ctx,
                                                batch, n_ctx, h2, h2_rows, lat_rows)
        else:
            w_in = b_w_in[j].astype(BF16)
            w_out = b_w_out[j].astype(BF16)
            conv_w = b_conv_w[j].astype(F32)
            lat, h2, idl, gtl, cntl = _conv_layer(lat, comb_lat, mv_lat, norm_g[l], w_in, conv_w, w_out,
                                                  route_lat, batch, tm, GRID_W, None, h2_rows, 0)
            if ctx_after:
                cx, h2, idc, gtc, cntc = _conv_layer(cx, comb_ctx, mv_ctx, norm_g[l], w_in, conv_w, w_out,
                                                     route_ctx, batch, n_ctx, n_ctx, h2, h2_rows, lat_rows)
        routed = [(idl, cntl, tm)] + ([(idc, cntc, n_ctx)] if ctx_after else [])
        outs = _moe(h2, routed, l, w_gate, w_up, w_down)
        comb_lat = (*outs[0], gtl.T)
        comb_ctx = (*outs[1], gtc.T) if ctx_after else ()
    out = _final(lat, comb_lat, _modv(mod, depth, list(range(batch))), final_g.astype(F32)[None, :], batch, tm)
    return out.reshape(batch, seq, d)
```

```python
import functools

import jax
import jax.numpy as jnp
from jax import lax
from jax.experimental import pallas as pl
from jax.experimental.pallas import tpu as pltpu

F32 = jnp.float32
BF16 = jnp.bfloat16

EPS = 1e-6
HEADS = 4
DQK = 128
DV = 256
NQ = HEADS * DQK
NV = HEADS * DV
IGATE_SOFTCAP = 15.0
GRID_W = 64
N_GROUPS = 4
EPG = 8
N_EXPERTS = N_GROUPS * EPG
ROUTE_ROWS = 40
MOE_BLOCK = 512
GATHER_AHEAD = 2
MOE_GROUPS = 4
SORT_PAD = 16
SORT_CHUNK = 256
PIECES = MOE_BLOCK // SORT_PAD
MLSTM_CHUNK = 128
MLSTM_SUB = 4
LANES = 128
VMEM_LIMIT = 52 * 1024 * 1024


def _cparams(*sem):
    return pltpu.CompilerParams(dimension_semantics=sem, vmem_limit_bytes=VMEM_LIMIT)


def _nt_dot(a, b):
    return lax.dot_general(a, b, (((1,), (1,)), ((), ())), preferred_element_type=F32)


def _rms(x, g):
    return x * lax.rsqrt(jnp.mean(x * x, axis=-1, keepdims=True) + EPS) * g


def _mod_kernel(c_ref, w_ref, b_ref, o_ref):
    c = c_ref[...]
    s = (c * jax.nn.sigmoid(c)).astype(BF16)
    o_ref[0] = jnp.dot(s, w_ref[0].astype(BF16), preferred_element_type=F32) + b_ref[0]


def _modulation(cond, mod_w, mod_b):
    depth, d, n = mod_w.shape
    tn = 1024
    return pl.pallas_call(
        _mod_kernel,
        grid=(depth, n // tn),
        in_specs=[pl.BlockSpec((8, d), lambda l, j: (0, 0)),
                  pl.BlockSpec((1, d, tn), lambda l, j: (l, 0, j)),
                  pl.BlockSpec((1, 1, tn), lambda l, j: (l, 0, j))],
        out_specs=pl.BlockSpec((1, 8, tn), lambda l, j: (l, 0, j)),
        out_shape=jax.ShapeDtypeStruct((depth, 8, n), F32),
        compiler_params=_cparams("arbitrary", "arbitrary"),
    )(cond, mod_w, mod_b.reshape(depth, 1, n))


def _residual_in(x_ref, comb_refs, mv):
    x = x_ref[...]
    if comb_refs:
        ya_ref, yb_ref, gt_ref = comb_refs
        gt = gt_ref[...]
        x = x + mv[6:7] * (gt[:, 0:1] * ya_ref[...].astype(F32) + gt[:, 1:2] * yb_ref[...].astype(F32))
    return x


def _route(x2, wr_hi_ref, wr_lo_ref, rb_ref, tri_ref, xs_ref, ids_ref, gts_ref, cnt_ref):
    tm = x2.shape[0]
    x_hi = x2.astype(BF16)
    x_lo = (x2 - x_hi.astype(F32)).astype(BF16)
    w_hi = wr_hi_ref[...]
    lg = _nt_dot(w_hi, x_hi) + _nt_dot(wr_lo_ref[...], x_hi) + _nt_dot(w_hi, x_lo)
    lg = lg + rb_ref[...]
    row = lax.broadcasted_iota(jnp.int32, (EPG, tm), 0)
    gl = lg[N_EXPERTS:N_EXPERTS + EPG]
    gl = jnp.where(row < N_GROUPS, gl, -jnp.inf)
    gmx = jnp.max(gl, axis=0, keepdims=True)
    grp = jnp.min(jnp.where(gl == gmx, row, EPG), axis=0, keepdims=True)
    p_grp = 1.0 / jnp.sum(jnp.exp(gl - gmx), axis=0, keepdims=True)
    sel = lg[0:EPG]
    for g in range(1, N_GROUPS):
        sel = jnp.where(grp == g, lg[g * EPG:(g + 1) * EPG], sel)
    mx1 = jnp.max(sel, axis=0, keepdims=True)
    i1 = jnp.min(jnp.where(sel == mx1, row, EPG), axis=0, keepdims=True)
    rest = jnp.where(row == i1, -jnp.inf, sel)
    mx2 = jnp.max(rest, axis=0, keepdims=True)
    i2 = jnp.min(jnp.where(rest == mx2, row, EPG), axis=0, keepdims=True)
    e2 = jnp.exp(mx2 - mx1)
    inv = p_grp / (1.0 + e2)
    eid1 = grp * EPG + i1
    eid2 = grp * EPG + i2
    gts_ref[0:1, :] = inv
    gts_ref[1:2, :] = inv * e2
    erow = lax.broadcasted_iota(jnp.int32, (N_EXPERTS, tm), 0)
    oh1 = erow == eid1
    oh2 = erow == eid2
    member = jnp.where(oh1, 1.0, jnp.where(oh2, 1.0, 0.0)).astype(BF16)
    before = jnp.dot(member, tri_ref[...], preferred_element_type=F32)
    cnt = jnp.dot(member, jnp.ones((tm, LANES), BF16), preferred_element_type=F32)
    cnt_ref[0] = cnt
    ids_ref[0:1, :] = eid1
    ids_ref[1:2, :] = eid2
    ids_ref[2:3, :] = jnp.sum(jnp.where(oh1, before, 0.0), axis=0, keepdims=True).astype(jnp.int32)
    ids_ref[3:4, :] = jnp.sum(jnp.where(oh2, before, 0.0), axis=0, keepdims=True).astype(jnp.int32)
    cnt_pad = jnp.floor((cnt + (SORT_PAD - 1)) * (1.0 / SORT_PAD)) * SORT_PAD
    er = lax.broadcasted_iota(jnp.int32, (N_EXPERTS, N_EXPERTS), 0)
    ec = lax.broadcasted_iota(jnp.int32, (N_EXPERTS, N_EXPERTS), 1)
    lower = jnp.where(ec < er, 1.0, 0.0).astype(BF16)
    first = jnp.dot(lower, cnt_pad.astype(BF16), preferred_element_type=F32)
    place = before + jnp.concatenate([first] * (tm // LANES), axis=1)
    pos1 = jnp.sum(jnp.where(oh1, place, 0.0), axis=0, keepdims=True).astype(jnp.int32)
    pos2 = jnp.sum(jnp.where(oh2, place, 0.0), axis=0, keepdims=True).astype(jnp.int32)
    x2b = x2.astype(BF16)
    for r0 in range(0, xs_ref.shape[0], SORT_CHUNK):
        r = lax.broadcasted_iota(jnp.int32, (SORT_CHUNK, tm), 0) + r0
        perm = jnp.where(r == pos1, 1.0, jnp.where(r == pos2, 1.0, 0.0)).astype(BF16)
        xs_ref[r0:r0 + SORT_CHUNK, :] = jnp.dot(perm, x2b, preferred_element_type=F32).astype(BF16)


def _norm2_and_route(x, mv, ng_ref, route_refs, out_refs):
    xo_ref, xs_ref, ids_ref, gts_ref, cnt_ref = out_refs
    xo_ref[...] = x
    x2 = _rms(x, ng_ref[1:2]) * (1.0 + mv[4:5]) + mv[3:4]
    _route(x2, *route_refs, xs_ref, ids_ref, gts_ref, cnt_ref)


def _full(shape):
    return pl.BlockSpec(shape, lambda i: (0,) * len(shape))


def _tile_specs(tm, d, has_comb, tiles_per_batch):
    tok = pl.BlockSpec((tm, d), lambda i: (i, 0))
    specs = [tok]
    if has_comb:
        specs += [tok, tok, pl.BlockSpec((tm, 2), lambda i: (i, 0))]
    specs.append(pl.BlockSpec((None, 8, d), lambda i: (i // tiles_per_batch, 0, 0)))
    specs.append(_full((2, d)))
    return specs


def _route_specs(d, tm):
    return [_full((ROUTE_ROWS, d)), _full((ROUTE_ROWS, d)), _full((ROUTE_ROWS, 1)), _full((tm, tm))]


def _sorted_rows(tm):
    return 2 * tm + N_EXPERTS * SORT_PAD


def _mixer_out(t, d, tm, h2_rows, h2_row0):
    sr = _sorted_rows(tm)
    assert h2_row0 % sr == 0
    off = h2_row0 // sr
    tok = pl.BlockSpec((tm, d), lambda i: (i, 0))
    specs = [tok, pl.BlockSpec((sr, d), lambda i: (i + off, 0)),
             pl.BlockSpec((4, tm), lambda i: (0, i)), pl.BlockSpec((2, tm), lambda i: (0, i)),
             pl.BlockSpec((1, N_EXPERTS, LANES), lambda i: (i, 0, 0))]
    shapes = [jax.ShapeDtypeStruct((t, d), F32), jax.ShapeDtypeStruct((h2_rows, d), BF16),
              jax.ShapeDtypeStruct((4, t), jnp.int32), jax.ShapeDtypeStruct((2, t), F32),
              jax.ShapeDtypeStruct((t // tm, N_EXPERTS, LANES), F32)]
    return specs, shapes


def _in_a_kernel(*refs, has_comb):
    n_in = 4 if has_comb else 1
    x_ref, comb, refs = refs[0], refs[1:n_in], refs[n_in:]
    mv_ref, ng_ref, wqt_ref, wk_ref, wvt_ref, wot_ref, wgt_ref, bg_ref = refs[:8]
    outs = refs[8:]
    if has_comb:
        xo_ref, outs = outs[0], outs[1:]
    qt_ref, k_ref, vt_ref, ot_ref, a_ref = outs
    mv = mv_ref[...]
    x = _residual_in(x_ref, comb, mv)
    if has_comb:
        xo_ref[...] = x
    h = (_rms(x, ng_ref[0:1]) * (1.0 + mv[1:2]) + mv[0:1]).astype(BF16)
    qt_ref[0] = (_nt_dot(wqt_ref[...], h) * (DQK ** -0.5)).astype(BF16)
    k_ref[...] = jnp.dot(h, wk_ref[...], preferred_element_type=F32).astype(BF16)
    vt_ref[0] = _nt_dot(wvt_ref[...], h).astype(BF16)
    ot_ref[0] = jax.nn.sigmoid(_nt_dot(wot_ref[...], h)).astype(BF16)
    g = _nt_dot(wgt_ref[...], h) + bg_ref[...]
    gi, gf = g[0:8], g[8:16]
    a_ref[0, 0:8, :] = IGATE_SOFTCAP * jnp.tanh(gi / IGATE_SOFTCAP)
    a_ref[0, 8:16, :] = jnp.minimum(gf, 0.0) - jnp.log(1.0 + jnp.exp(-jnp.abs(gf)))


def _in_a(x, comb, modv, ng, w, batch, tm):
    t, d = x.shape
    s = t // batch
    tpb = s // tm
    has_comb = bool(comb)
    in_specs = _tile_specs(tm, d, has_comb, tpb) + [_full(a.shape) for a in w]
    tok = lambda n: pl.BlockSpec((tm, n), lambda i: (i, 0))
    rows = lambda n: pl.BlockSpec((1, n, tm), lambda i: (i // tpb, 0, i % tpb))
    out_specs = [rows(NQ), tok(NQ), rows(NV), rows(d), rows(16)]
    out_shape = [jax.ShapeDtypeStruct((batch, NQ, s), BF16), jax.ShapeDtypeStruct((t, NQ), BF16),
                 jax.ShapeDtypeStruct((batch, NV, s), BF16), jax.ShapeDtypeStruct((batch, d, s), BF16),
                 jax.ShapeDtypeStruct((batch, 16, s), F32)]
    if has_comb:
        out_specs = [tok(d)] + out_specs
        out_shape = [jax.ShapeDtypeStruct((t, d), F32)] + out_shape
    outs = pl.pallas_call(
        functools.partial(_in_a_kernel, has_comb=has_comb),
        grid=(t // tm,), in_specs=in_specs, out_specs=out_specs, out_shape=out_shape,
        compiler_params=_cparams("parallel"),
    )(x, *comb, modv, ng, *w)
    if has_comb:
        return outs[0], outs[1:]
    return x, outs


def _chunk_scan(x, op, fill, forward):
    length = x.shape[-1]
    pos = lax.broadcasted_iota(jnp.int32, x.shape, 1)
    s = 1
    while s < length:
        if forward:
            x = op(x, jnp.where(pos >= s, pltpu.roll(x, s, axis=1), fill))
        else:
            x = op(x, jnp.where(pos < length - s, pltpu.roll(x, length - s, axis=1), fill))
        s *= 2
    return x


def _gate_kernel(a_ref, o_ref):
    for r in range(2 * HEADS):
        fwd = r < HEADS
        b = _chunk_scan(a_ref[0, 8 + r], jnp.add, 0.0, fwd)
        u = a_ref[0, r] - b
        cm = _chunk_scan(u, jnp.maximum, -jnp.inf, fwd)
        end = b.shape[-1] - 1 if fwd else 0
        o_ref[0, r] = u
        o_ref[0, 8 + r] = b
        o_ref[0, 16 + r] = cm
        o_ref[0, 24 + r] = jnp.broadcast_to(cm[:, end:end + 1], cm.shape)
        o_ref[0, 32 + r] = jnp.broadcast_to(b[:, end:end + 1], b.shape)


GATE_ROWS = 40


def _gate_scans(act, chunk):
    batch, _, s = act.shape
    nc = s // chunk
    g = pl.pallas_call(
        _gate_kernel, grid=(batch,),
        in_specs=[pl.BlockSpec((1, 16, nc, chunk), lambda b: (b, 0, 0, 0))],
        out_specs=pl.BlockSpec((1, GATE_ROWS, nc, chunk), lambda b: (b, 0, 0, 0)),
        out_shape=jax.ShapeDtypeStruct((batch, GATE_ROWS, nc, chunk), F32),
        compiler_params=_cparams("parallel"),
    )(act.reshape(batch, 16, nc, chunk))
    rows = jnp.transpose(g, (0, 2, 1, 3))
    cols = jnp.transpose(g[:, 0:8], (0, 2, 3, 1))
    return rows, cols


def _mlstm_kernel(qf_ref, kf_ref, vf_ref, rf_ref, cf_ref, qb_ref, kb_ref, vb_ref, rb_ref, cb_ref,
                  c0_ref, n0_ref, m0_ref, hf_ref, hb_ref, ct_ref, nt_ref, mt_ref,
                  c_s, n_s, m_s, *, chunk):
    step = pl.program_id(1)
    last = pl.num_programs(1) - 1

    @pl.when(step == 0)
    def _():
        c_s[...] = c0_ref[0]
        n_s[...] = n0_ref[0]
        m_s[...] = m0_ref[0]

    si = lax.broadcasted_iota(jnp.int32, (chunk, chunk), 0)
    ji = lax.broadcasted_iota(jnp.int32, (chunk, chunk), 1)
    sub = qf_ref.shape[2] // chunk
    dirs = ((qf_ref, kf_ref, vf_ref, rf_ref, cf_ref, hf_ref),
            (qb_ref, kb_ref, vb_ref, rb_ref, cb_ref, hb_ref))
    units = [(j if d == 0 else sub - 1 - j, d, h, dirs[d])
             for j in range(sub) for d in range(2) for h in range(HEADS)]
    per_sub = 2 * HEADS
    for u0, (c, d, h, (q_ref, k_ref, v_ref, r_ref, c_ref, h_ref)) in enumerate(units):
        if u0 % per_sub == 0:
            scores = []
            for c2, d2, h2, (q2_ref, k2_ref, _, _, _, _) in units[u0:u0 + per_sub]:
                qt2 = q2_ref[0, h2 * DQK:(h2 + 1) * DQK, c2 * chunk:(c2 + 1) * chunk]
                k2 = k2_ref[c2 * chunk:(c2 + 1) * chunk, h2 * DQK:(h2 + 1) * DQK]
                n16 = jnp.broadcast_to(n_s[d2 * HEADS + h2], (16, DQK)).astype(BF16)
                scores.append(jnp.dot(jnp.concatenate([k2, n16], axis=0), qt2,
                                      preferred_element_type=F32))
        sx = scores[u0 % per_sub]
        mask = (si <= ji) if d == 0 else (si >= ji)
        idx = d * HEADS + h
        lanes = slice(c * chunk, (c + 1) * chunk)
        qt = q_ref[0, h * DQK:(h + 1) * DQK, lanes]
        k = k_ref[c * chunk:(c + 1) * chunk, h * DQK:(h + 1) * DQK]
        vt = v_ref[0, h * DV:(h + 1) * DV, lanes]
        u_col = c_ref[0, c, :, idx:idx + 1]
        u_row = r_ref[0, c, idx:idx + 1, :]
        b_row = r_ref[0, c, 8 + idx:9 + idx, :]
        cm_row = r_ref[0, c, 16 + idx:17 + idx, :]
        cm_end = r_ref[0, c, 24 + idx:25 + idx, :]
        b_end = r_ref[0, c, 32 + idx:33 + idx, :]
        ct_st = c_s[idx]
        n_st = n_s[idx]
        m_st = m_s[idx]
        mm = jnp.maximum(m_st, cm_row)
        dt = jnp.where(mask, jnp.exp(u_col - mm), 0.0)
        pt = sx[:chunk] * dt
        a_int = jnp.exp(m_st - mm)
        den = jnp.sum(pt, axis=0, keepdims=True) + a_int * sx[chunk:chunk + 1]
        inv = 1.0 / jnp.maximum(jnp.abs(den), jnp.exp(-b_row - mm))
        wts = jnp.concatenate([(pt * inv).astype(BF16),
                               (qt.astype(F32) * (a_int * inv)).astype(BF16)], axis=0)
        vals = jnp.concatenate([vt, ct_st.astype(BF16)], axis=1)
        h_ref[0, h * DV:(h + 1) * DV, lanes] = jnp.dot(vals, wts, preferred_element_type=F32)
        mm_end = jnp.maximum(m_st, cm_end)
        a_end = jnp.exp(u_row - mm_end)
        decay = jnp.exp(m_st - mm_end)
        upd = jnp.concatenate([(vt.astype(F32) * a_end).astype(BF16),
                               jnp.broadcast_to(a_end, (16, chunk)).astype(BF16)], axis=0)
        upd = jnp.dot(upd, k, preferred_element_type=F32)
        c_s[idx] = decay * ct_st + upd[:DV]
        n_s[idx] = decay * n_st + upd[DV:DV + 1]
        m_s[idx] = b_end + mm_end

    @pl.when(step == last)
    def _():
        ct_ref[0] = c_s[...]
        nt_ref[0] = n_s[...]
        mt_ref[0] = m_s[...]


def _mlstm_zero_state(batch, chunk):
    return (jnp.zeros((batch, 8, DV, DQK), F32), jnp.zeros((batch, 8, 1, DQK), F32),
            jnp.zeros((batch, 8, 1, chunk), F32))


def _mlstm(qt, k, vt, rows, cols, state, batch, chunk):
    t = k.shape[0]
    s = t // batch
    nc = s // chunk
    assert chunk == DQK
    c0, n0, m0 = state
    sub = MLSTM_SUB if nc % MLSTM_SUB == 0 else 1
    span = sub * chunk
    ns = nc // sub

    def specs(cidx):
        return [pl.BlockSpec((1, NQ, span), lambda b, c: (b, 0, cidx(c))),
                pl.BlockSpec((span, NQ), lambda b, c: (b * ns + cidx(c), 0)),
                pl.BlockSpec((1, NV, span), lambda b, c: (b, 0, cidx(c))),
                pl.BlockSpec((1, sub, GATE_ROWS, chunk), lambda b, c: (b, cidx(c), 0, 0)),
                pl.BlockSpec((1, sub, chunk, 8), lambda b, c: (b, cidx(c), 0, 0))]

    fwd = lambda c: c
    bwd = lambda c: ns - 1 - c
    st_specs = [pl.BlockSpec((1, 8, DV, DQK), lambda b, c: (b, 0, 0, 0)),
                pl.BlockSpec((1, 8, 1, DQK), lambda b, c: (b, 0, 0, 0)),
                pl.BlockSpec((1, 8, 1, chunk), lambda b, c: (b, 0, 0, 0))]
    st_shape = [jax.ShapeDtypeStruct(a.shape, F32) for a in state]
    h_shape = jax.ShapeDtypeStruct((batch, NV, s), F32)
    args = (qt, k, vt, rows, cols)
    hf, hb, ct, nt, mt = pl.pallas_call(
        functools.partial(_mlstm_kernel, chunk=chunk),
        grid=(batch, ns),
        in_specs=specs(fwd) + specs(bwd) + st_specs,
        out_specs=[pl.BlockSpec((1, NV, span), lambda b, c: (b, 0, fwd(c))),
                   pl.BlockSpec((1, NV, span), lambda b, c: (b, 0, bwd(c)))] + st_specs,
        out_shape=[h_shape, h_shape] + st_shape,
        scratch_shapes=[pltpu.VMEM((8, DV, DQK), F32), pltpu.VMEM((8, 1, DQK), F32),
                        pltpu.VMEM((8, 1, chunk), F32)],
        compiler_params=_cparams("arbitrary", "arbitrary"),
    )(*args, *args, c0, n0, m0)
    return hf, hb, (ct, nt, mt)


def _out_a_kernel(*refs, aliased):
    hf_ref, hb_ref, o_ref, x_ref, mv_ref, ng_ref, hg_ref, wout_ref = refs[:8]
    route_refs = refs[8:12]
    out_refs = refs[12 + int(aliased):]
    mv = mv_ref[...]
    hsum = hf_ref[0] + hb_ref[0]
    parts = []
    for h in range(HEADS):
        hh = hsum[h * DV:(h + 1) * DV]
        parts.append(hh * lax.rsqrt(jnp.mean(hh * hh, axis=0, keepdims=True) + EPS))
    hn = jnp.concatenate(parts, axis=0) * hg_ref[...] * o_ref[0].astype(F32)
    y = jnp.dot(wout_ref[...], hn.astype(BF16), preferred_element_type=F32).T
    x = x_ref[...] + mv[2:3] * y
    _norm2_and_route(x, mv, ng_ref, route_refs, out_refs)


def _out_a(hf, hb, o, x, modv, ng, head_g, w_out_t, route_w, batch, tm, h2_buf, h2_rows, h2_row0):
    t, d = x.shape
    tpb = t // batch // tm
    rows = lambda n: pl.BlockSpec((1, n, tm), lambda i: (i // tpb, 0, i % tpb))
    in_specs = ([rows(NV), rows(NV), rows(d)] + _tile_specs(tm, d, False, tpb)
                + [_full((NV, 1)), _full(w_out_t.shape)] + _route_specs(d, tm))
    w_out = w_out_t
    args = [hf, hb, o, x, modv, ng, head_g, w_out, *route_w]
    return _mixer_call(functools.partial(_out_a_kernel, aliased=h2_buf is not None),
                       in_specs, args, t, d, tm, h2_buf, h2_rows, h2_row0)


def _mixer_call(body, in_specs, args, t, d, tm, h2_buf, h2_rows, h2_row0):
    out_specs, out_shape = _mixer_out(t, d, tm, h2_rows, h2_row0)
    aliases = {}
    if h2_buf is not None:
        aliases = {len(args): 1}
        in_specs = in_specs + [pl.BlockSpec(memory_space=pl.ANY)]
        args = args + [h2_buf]
    return pl.pallas_call(
        body, grid=(t // tm,), in_specs=in_specs, out_specs=out_specs, out_shape=out_shape,
        input_output_aliases=aliases, compiler_params=_cparams("parallel"),
    )(*args)


def _conv_kernel(*refs, has_comb, row_w, aliased):
    n_in = 4 if has_comb else 1
    x_ref, comb, refs = refs[0], refs[1:n_in], refs[n_in:]
    mv_ref, ng_ref, win_ref, cw_ref, wout_ref = refs[:5]
    route_refs = refs[5:9]
    out_refs = refs[9 + int(aliased):]
    mv = mv_ref[...]
    x = _residual_in(x_ref, comb, mv)
    d = x.shape[1]
    tm = x.shape[0]
    cw = cw_ref[...]
    groups = 2 if (tm // 2) % row_w == 0 else 1
    gm = tm // groups
    xg = [x[g * gm:(g + 1) * gm] for g in range(groups)]
    hg = [(_rms(a, ng_ref[0:1]) * (1.0 + mv[1:2]) + mv[0:1]).astype(BF16) for a in xg]
    pos = lax.broadcasted_iota(jnp.int32, (gm, d), 0) % row_w

    def in_proj(h):
        return (jnp.dot(h, win_ref[:, :d], preferred_element_type=F32),
                jnp.dot(h, win_ref[:, d:2 * d], preferred_element_type=F32),
                jnp.dot(h, win_ref[:, 2 * d:], preferred_element_type=F32))

    def gate(p):
        bg, cg, u = p
        z = cg * u
        left = jnp.where(pos == 0, 0.0, pltpu.roll(z, 1, axis=0))
        right = jnp.where(pos == row_w - 1, 0.0, pltpu.roll(z, gm - 1, axis=0))
        return (bg * (cw[0:1] * left + cw[1:2] * z + cw[2:3] * right)).astype(BF16)

    pg = [in_proj(h) for h in hg]
    outs = []
    for g in range(groups):
        yl = jnp.dot(gate(pg[g]), wout_ref[...], preferred_element_type=F32)
        outs.append(xg[g] + mv[2:3] * yl)
    x = jnp.concatenate(outs, axis=0) if groups > 1 else outs[0]
    _norm2_and_route(x, mv, ng_ref, route_refs, out_refs)


def _conv_layer(x, comb, modv, ng, w_in, conv_w, w_out, route_w, batch, tm, row_w, h2_buf, h2_rows, h2_row0):
    t, d = x.shape
    tpb = t // batch // tm
    in_specs = (_tile_specs(tm, d, bool(comb), tpb)
                + [_full(w_in.shape), _full(conv_w.shape), _full(w_out.shape)] + _route_specs(d, tm))
    args = [x, *comb, modv, ng, w_in, conv_w, w_out, *route_w]
    body = functools.partial(_conv_kernel, has_comb=bool(comb), row_w=row_w, aliased=h2_buf is not None)
    return _mixer_call(body, in_specs, args, t, d, tm, h2_buf, h2_rows, h2_row0)


def _moe_kernel(be_ref, nb_ref, src_ref, xs_ref, wg_ref, wu_ref, wd_ref, y_ref,
                xbuf, sem, wg_s, wu_s, wd_s):
    i = pl.program_id(0)
    nb = nb_ref[0]
    n_slots = GATHER_AHEAD + 1

    def piece_copies(blk, slot):
        copies = []
        for j in range(PIECES):
            src = pl.multiple_of(src_ref[blk * PIECES + j], SORT_PAD)
            copies.append(pltpu.make_async_copy(
                xs_ref.at[pl.ds(src, SORT_PAD), :],
                xbuf.at[slot, pl.ds(j * SORT_PAD, SORT_PAD), :], sem.at[slot]))
        return copies

    def wait_slot(slot):
        pltpu.make_async_copy(xs_ref.at[pl.ds(0, MOE_BLOCK), :], xbuf.at[slot], sem.at[slot]).wait()

    @pl.when(i < nb)
    def _():
        slot = lax.rem(i, n_slots)

        for b in range(GATHER_AHEAD):
            @pl.when((i == 0) & (b < nb))
            def _(b=b):
                for c in piece_copies(b, b):
                    c.start()

        @pl.when(i + GATHER_AHEAD < nb)
        def _():
            for c in piece_copies(i + GATHER_AHEAD, lax.rem(i + GATHER_AHEAD, n_slots)):
                c.start()

        prev = be_ref[jnp.maximum(i - 1, 0)]

        @pl.when((i == 0) | (be_ref[i] != prev))
        def _():
            wg_s[...] = wg_ref[0].astype(BF16)
            wu_s[...] = wu_ref[0].astype(BF16)
            wd_s[...] = wd_ref[0].astype(BF16)

        wait_slot(slot)
        gm = MOE_BLOCK // MOE_GROUPS
        gu = []
        for r in range(MOE_GROUPS):
            x = xbuf[slot, r * gm:(r + 1) * gm, :]
            gu.append((jnp.dot(x, wg_s[...], preferred_element_type=F32),
                       jnp.dot(x, wu_s[...], preferred_element_type=F32)))
        for r, (g, u) in enumerate(gu):
            a = (g * jax.nn.sigmoid(g) * u).astype(BF16)
            y_ref[r * gm:(r + 1) * gm, :] = jnp.dot(a, wd_s[...], preferred_element_type=F32).astype(BF16)


def _moe_experts(xs, blk_e, nb_used, piece_src, w_gate, w_up, w_down):
    d = xs.shape[1]
    de = w_gate.shape[-1]
    n_blocks = blk_e.shape[0]
    grid_spec = pltpu.PrefetchScalarGridSpec(
        num_scalar_prefetch=3, grid=(n_blocks,),
        in_specs=[pl.BlockSpec(memory_space=pl.ANY),
                  pl.BlockSpec((1, d, de), lambda i, be, nb, src: (be[jnp.minimum(i, nb[0] - 1)], 0, 0)),
                  pl.BlockSpec((1, d, de), lambda i, be, nb, src: (be[jnp.minimum(i, nb[0] - 1)], 0, 0)),
                  pl.BlockSpec((1, de, d), lambda i, be, nb, src: (be[jnp.minimum(i, nb[0] - 1)], 0, 0))],
        out_specs=pl.BlockSpec((MOE_BLOCK, d), lambda i, be, nb, src: (jnp.minimum(i, nb[0] - 1), 0)),
        scratch_shapes=[pltpu.VMEM((GATHER_AHEAD + 1, MOE_BLOCK, d), BF16),
                        pltpu.SemaphoreType.DMA((GATHER_AHEAD + 1,)),
                        pltpu.VMEM((d, de), BF16), pltpu.VMEM((d, de), BF16),
                        pltpu.VMEM((de, d), BF16)])
    return pl.pallas_call(
        _moe_kernel, grid_spec=grid_spec,
        out_shape=jax.ShapeDtypeStruct((n_blocks * MOE_BLOCK, d), BF16),
        compiler_params=_cparams("arbitrary"),
    )(blk_e, nb_used, piece_src, xs, w_gate, w_up, w_down)


def _dest_slots(ids, base, tm):
    t = ids.shape[1]
    nt = t // tm
    eid = ids[0:2].reshape(2, nt, tm)
    rank = ids[2:4].reshape(2, nt, tm)
    onehot = eid[..., None] == jnp.arange(N_EXPERTS, dtype=jnp.int32)
    off = jnp.sum(jnp.where(onehot, base[None, :, None, :], 0), axis=-1)
    return (rank + off).reshape(2, t)


def _moe(xs, routed, layer, w_gate, w_up, w_down):
    cnt_tiles = jnp.concatenate([cnt[:, :, 0] for _, cnt, _ in routed], axis=0).astype(jnp.int32)
    cnt_pad = (cnt_tiles + SORT_PAD - 1) // SORT_PAD * SORT_PAD
    run_first = jnp.cumsum(cnt_pad, axis=0) - cnt_pad
    in_tile = jnp.cumsum(cnt_pad, axis=1) - cnt_pad
    region = jnp.sum(cnt_pad, axis=0)
    padded = (region + MOE_BLOCK - 1) // MOE_BLOCK * MOE_BLOCK
    pends = jnp.cumsum(padded)
    pstart = pends - padded
    base = pstart[None, :] + run_first
    dests, tile_row0, row, xrow, worst = [], [], 0, 0, 0
    for ids, cnt, tm in routed:
        nt = cnt.shape[0]
        dests.append(_dest_slots(ids, base[row:row + nt], tm))
        tile_row0.append(xrow + jnp.arange(nt, dtype=jnp.int32) * _sorted_rows(tm))
        row += nt
        xrow += nt * _sorted_rows(tm)
        worst += nt * (2 * tm + N_EXPERTS * (SORT_PAD - 1))
    tile_row0 = jnp.concatenate(tile_row0)
    n_blocks = -(-(worst + N_EXPERTS * (MOE_BLOCK - 1)) // MOE_BLOCK)
    blk_start = jnp.arange(n_blocks, dtype=jnp.int32) * MOE_BLOCK
    blk_x = jnp.minimum(jnp.sum((pends[None, :] <= blk_start[:, None]).astype(jnp.int32), axis=1),
                        N_EXPERTS - 1)
    nb_used = (pends[-1:] // MOE_BLOCK).astype(jnp.int32)
    blk_oh = blk_x[:, None] == jnp.arange(N_EXPERTS, dtype=jnp.int32)[None, :]
    pick = lambda tab: jnp.sum(jnp.where(blk_oh[:, None, :], tab[None], 0), axis=2)
    per_piece = lambda a: jnp.repeat(a, PIECES, axis=0)
    blk_first = per_piece(pick(run_first))
    blk_src0 = per_piece(pick(in_tile - run_first) + tile_row0[None, :])
    blk_scal = per_piece(pick(jnp.stack([pstart, region])))
    piece_rank = jnp.arange(n_blocks * PIECES, dtype=jnp.int32) * SORT_PAD - blk_scal[:, 0]
    piece_tile = jnp.sum((blk_first <= piece_rank[:, None]).astype(jnp.int32), axis=1) - 1
    tile_oh = piece_tile[:, None] == jnp.arange(blk_first.shape[1], dtype=jnp.int32)[None, :]
    src = jnp.sum(jnp.where(tile_oh, blk_src0, 0), axis=1) + piece_rank
    piece_src = jnp.where(piece_rank < blk_scal[:, 1], src, 0).astype(jnp.int32)
    yb = _moe_experts(xs, blk_x + layer * N_EXPERTS, nb_used, piece_src, w_gate, w_up, w_down)
    return [(yb.at[d[0]].get(mode='promise_in_bounds'), yb.at[d[1]].get(mode='promise_in_bounds'))
            for d in dests]


def _final_kernel(x_ref, ya_ref, yb_ref, gt_ref, mv_ref, g_ref, o_ref):
    x = _residual_in(x_ref, (ya_ref, yb_ref, gt_ref), mv_ref[...])
    o_ref[...] = _rms(x, g_ref[...])


def _final(x, comb, modv, g, batch, tm):
    t, d = x.shape
    tpb = t // batch // tm
    tok = pl.BlockSpec((tm, d), lambda i: (i, 0))
    return pl.pallas_call(
        _final_kernel, grid=(t // tm,),
        in_specs=[tok, tok, tok, pl.BlockSpec((tm, 2), lambda i: (i, 0)),
                  pl.BlockSpec((None, 8, d), lambda i: (i // tpb, 0, 0)), _full((1, d))],
        out_specs=tok, out_shape=jax.ShapeDtypeStruct((t, d), F32),
        compiler_params=_cparams("parallel"),
    )(x, *comb, modv, g)


def _prep_a(w_in, b_gate):
    d = w_in.shape[0]
    wqt = w_in[:, :NQ].T.astype(BF16)
    wk = w_in[:, NQ:2 * NQ].astype(BF16)
    wvt = w_in[:, 2 * NQ:2 * NQ + NV].T.astype(BF16)
    wot = w_in[:, 2 * NQ + NV:2 * NQ + NV + d].T.astype(BF16)
    perm = jnp.array([0, 1, 2, 3, 8, 9, 10, 11, 4, 5, 6, 7, 12, 13, 14, 15], jnp.int32)
    wgt = w_in[:, 2 * NQ + NV + d:].T[perm].astype(BF16)
    bg = b_gate.astype(F32)[perm][:, None]
    return wqt, wk, wvt, wot, wgt, bg


def _prep_route(w_group, b_group, w_router, b_router, tm):
    d = w_group.shape[0]
    pad = ROUTE_ROWS - N_EXPERTS - N_GROUPS
    wt = jnp.concatenate([w_router.T, w_group.T, jnp.zeros((pad, d), F32)], axis=0).astype(F32)
    hi = wt.astype(BF16)
    lo = (wt - hi.astype(F32)).astype(BF16)
    rb = jnp.concatenate([b_router, b_group, jnp.zeros((pad,), F32)]).astype(F32)[:, None]
    tri = jnp.triu(jnp.ones((tm, tm), BF16), k=1)
    return hi, lo, rb, tri


def _modv(mod, l, rows):
    depth = mod.shape[0]
    d = mod.shape[-1] // 6
    zero = jnp.zeros((len(rows), 1, d), F32)
    cur = jnp.stack([mod[l, r].reshape(6, d) for r in rows]) if l < depth else jnp.zeros((len(rows), 6, d), F32)
    prev = jnp.stack([mod[l - 1, r].reshape(6, d)[5:6] for r in rows]) if l > 0 else zero
    return jnp.concatenate([cur, prev, zero], axis=1)


def kernel(x, c, ctx, c_ctx, mod_w, mod_b, norm_g, final_g, a_w_in, a_b_gate, a_head_g, a_w_out,
           b_w_in, b_conv_w, b_w_out, moe_w_group, moe_b_group, moe_w_router, moe_b_router,
           moe_w_gate, moe_w_up, moe_w_down):
    batch, seq, d = x.shape
    n_ctx = ctx.shape[1]
    depth = mod_w.shape[0]
    assert batch + 1 <= 8 and seq % MLSTM_CHUNK == 0 and n_ctx % MLSTM_CHUNK == 0
    tm = min(512, seq)
    t_lat = batch * seq
    t_ctx = batch * n_ctx
    assert t_lat % n_ctx == 0 and seq % GRID_W == 0

    cond = jnp.concatenate([c, c_ctx[None, :], jnp.zeros((8 - batch - 1, d), F32)], axis=0)
    mod = _modulation(cond, mod_w, mod_b)
    de = moe_w_gate.shape[-1]
    w_gate = moe_w_gate.reshape(depth * N_EXPERTS, d, de)
    w_up = moe_w_up.reshape(depth * N_EXPERTS, d, de)
    w_down = moe_w_down.reshape(depth * N_EXPERTS, de, d)

    lat = x.reshape(t_lat, d)
    cx = ctx.reshape(t_ctx, d)
    comb_lat, comb_ctx = (), ()
    for l in range(depth):
        kind, j = l % 2, l // 2
        ctx_after = any(i % 2 == 0 for i in range(l + 1, depth))
        mv_lat = _modv(mod, l, list(range(batch)))
        mv_ctx = _modv(mod, l, [batch] * batch)
        route_args = (moe_w_group[l], moe_b_group[l], moe_w_router[l], moe_b_router[l])
        route_lat = _prep_route(*route_args, tm)
        route_ctx = _prep_route(*route_args, n_ctx)
        lat_rows = t_lat // tm * _sorted_rows(tm)
        h2_rows = lat_rows + (t_ctx // n_ctx * _sorted_rows(n_ctx) if ctx_after else 0)
        if kind == 0:
            wa = _prep_a(a_w_in[j], a_b_gate[j])
            state = _mlstm_zero_state(batch, MLSTM_CHUNK)
            cx, (qtc, kc, vtc, oc, actc) = _in_a(cx, comb_ctx, mv_ctx, norm_g[l], wa, batch, n_ctx)
            hfc, hbc, state = _mlstm(qtc, kc, vtc, *_gate_scans(actc, MLSTM_CHUNK), state, batch, MLSTM_CHUNK)
            lat, (qtl, kl, vtl, ol, actl) = _in_a(lat, comb_lat, mv_lat, norm_g[l], wa, batch, tm)
            hfl, hbl, _ = _mlstm(qtl, kl, vtl, *_gate_scans(actl, MLSTM_CHUNK), state, batch, MLSTM_CHUNK)
            w_out = a_w_out[j].T.astype(BF16)
            head_g = a_head_g[j].astype(F32)[:, None]
            lat, h2, idl, gtl, cntl = _out_a(hfl, hbl, ol, lat, mv_lat, norm_g[l], head_g, w_out, route_lat,
                                             batch, tm, None, h2_rows, 0)
            if ctx_after:
                cx, h2, idc, gtc, cntc = _out_a(hfc, hbc, oc, cx, mv_ctx, norm_g[l], head_g, w_out, route_ctx,
                                                batch, n_ctx, h2, h2_rows, lat_rows)
        else:
            w_in = b_w_in[j].astype(BF16)
            w_out = b_w_out[j].astype(BF16)
            conv_w = b_conv_w[j].astype(F32)
            lat, h2, idl, gtl, cntl = _conv_layer(lat, comb_lat, mv_lat, norm_g[l], w_in, conv_w, w_out,
                                                  route_lat, batch, tm, GRID_W, None, h2_rows, 0)
            if ctx_after:
                cx, h2, idc, gtc, cntc = _conv_layer(cx, comb_ctx, mv_ctx, norm_g[l], w_in, conv_w, w_out,
                                                     route_ctx, batch, n_ctx, n_ctx, h2, h2_rows, lat_rows)
        routed = [(idl, cntl, tm)] + ([(idc, cntc, n_ctx)] if ctx_after else [])
        outs = _moe(h2, routed, l, w_gate, w_up, w_down)
        comb_lat = (*outs[0], gtl.T)
        comb_ctx = (*outs[1], gtc.T) if ctx_after else ()
    out = _final(lat, comb_lat, _modv(mod, depth, list(range(batch))), final_g.astype(F32)[None, :], batch, tm)
    return out.reshape(batch, seq, d)
```

```python
import functools

import jax
import jax.numpy as jnp
from jax import lax
from jax.experimental import pallas as pl
from jax.experimental.pallas import tpu as pltpu

F32 = jnp.float32
BF16 = jnp.bfloat16

EPS = 1e-6
HEADS = 4
DQK = 128
DV = 256
NQ = HEADS * DQK
NV = HEADS * DV
IGATE_SOFTCAP = 15.0
GRID_W = 64
N_GROUPS = 4
EPG = 8
N_EXPERTS = N_GROUPS * EPG
ROUTE_ROWS = 40
MOE_BLOCK = 512
GATHER_AHEAD = 3
MOE_GROUPS = 4
SORT_PAD = 16
SORT_CHUNK = 256
PIECES = MOE_BLOCK // SORT_PAD
MLSTM_CHUNK = 128
MLSTM_SUB = 4
LANES = 128
VMEM_LIMIT = 52 * 1024 * 1024


def _cparams(*sem):
    return pltpu.CompilerParams(dimension_semantics=sem, vmem_limit_bytes=VMEM_LIMIT)


def _nt_dot(a, b):
    return lax.dot_general(a, b, (((1,), (1,)), ((), ())), preferred_element_type=F32)


def _rms(x, g):
    return x * lax.rsqrt(jnp.mean(x * x, axis=-1, keepdims=True) + EPS) * g


def _mod_kernel(c_ref, w_ref, b_ref, o_ref):
    c = c_ref[...]
    s = (c * jax.nn.sigmoid(c)).astype(BF16)
    o_ref[0] = jnp.dot(s, w_ref[0].astype(BF16), preferred_element_type=F32) + b_ref[0]


def _modulation(cond, mod_w, mod_b):
    depth, d, n = mod_w.shape
    tn = 1024
    return pl.pallas_call(
        _mod_kernel,
        grid=(depth, n // tn),
        in_specs=[pl.BlockSpec((8, d), lambda l, j: (0, 0)),
                  pl.BlockSpec((1, d, tn), lambda l, j: (l, 0, j)),
                  pl.BlockSpec((1, 1, tn), lambda l, j: (l, 0, j))],
        out_specs=pl.BlockSpec((1, 8, tn), lambda l, j: (l, 0, j)),
        out_shape=jax.ShapeDtypeStruct((depth, 8, n), F32),
        compiler_params=_cparams("arbitrary", "arbitrary"),
    )(cond, mod_w, mod_b.reshape(depth, 1, n))


def _residual_in(x_ref, comb_refs, mv):
    x = x_ref[...]
    if comb_refs:
        ya_ref, yb_ref, gt_ref = comb_refs
        gt = gt_ref[...]
        x = x + mv[6:7] * (gt[:, 0:1] * ya_ref[...].astype(F32) + gt[:, 1:2] * yb_ref[...].astype(F32))
    return x


def _route(x2, wr_hi_ref, wr_lo_ref, rb_ref, tri_ref, xs_ref, ids_ref, gts_ref, cnt_ref):
    tm = x2.shape[0]
    x_hi = x2.astype(BF16)
    x_lo = (x2 - x_hi.astype(F32)).astype(BF16)
    w_hi = wr_hi_ref[...]
    lg = _nt_dot(w_hi, x_hi) + _nt_dot(wr_lo_ref[...], x_hi) + _nt_dot(w_hi, x_lo)
    lg = lg + rb_ref[...]
    row = lax.broadcasted_iota(jnp.int32, (EPG, tm), 0)
    gl = lg[N_EXPERTS:N_EXPERTS + EPG]
    gl = jnp.where(row < N_GROUPS, gl, -jnp.inf)
    gmx = jnp.max(gl, axis=0, keepdims=True)
    grp = jnp.min(jnp.where(gl == gmx, row, EPG), axis=0, keepdims=True)
    p_grp = 1.0 / jnp.sum(jnp.exp(gl - gmx), axis=0, keepdims=True)
    sel = lg[0:EPG]
    for g in range(1, N_GROUPS):
        sel = jnp.where(grp == g, lg[g * EPG:(g + 1) * EPG], sel)
    mx1 = jnp.max(sel, axis=0, keepdims=True)
    i1 = jnp.min(jnp.where(sel == mx1, row, EPG), axis=0, keepdims=True)
    rest = jnp.where(row == i1, -jnp.inf, sel)
    mx2 = jnp.max(rest, axis=0, keepdims=True)
    i2 = jnp.min(jnp.where(rest == mx2, row, EPG), axis=0, keepdims=True)
    e2 = jnp.exp(mx2 - mx1)
    inv = p_grp / (1.0 + e2)
    eid1 = grp * EPG + i1
    eid2 = grp * EPG + i2
    gts_ref[0:1, :] = inv
    gts_ref[1:2, :] = inv * e2
    erow = lax.broadcasted_iota(jnp.int32, (N_EXPERTS, tm), 0)
    oh1 = erow == eid1
    oh2 = erow == eid2
    member = jnp.where(oh1, 1.0, jnp.where(oh2, 1.0, 0.0)).astype(BF16)
    before = jnp.dot(member, tri_ref[...], preferred_element_type=F32)
    cnt = jnp.dot(member, jnp.ones((tm, LANES), BF16), preferred_element_type=F32)
    cnt_ref[0] = cnt
    ids_ref[0:1, :] = eid1
    ids_ref[1:2, :] = eid2
    ids_ref[2:3, :] = jnp.sum(jnp.where(oh1, before, 0.0), axis=0, keepdims=True).astype(jnp.int32)
    ids_ref[3:4, :] = jnp.sum(jnp.where(oh2, before, 0.0), axis=0, keepdims=True).astype(jnp.int32)
    cnt_pad = jnp.floor((cnt + (SORT_PAD - 1)) * (1.0 / SORT_PAD)) * SORT_PAD
    er = lax.broadcasted_iota(jnp.int32, (N_EXPERTS, N_EXPERTS), 0)
    ec = lax.broadcasted_iota(jnp.int32, (N_EXPERTS, N_EXPERTS), 1)
    lower = jnp.where(ec < er, 1.0, 0.0).astype(BF16)
    first = jnp.dot(lower, cnt_pad.astype(BF16), preferred_element_type=F32)
    place = before + jnp.concatenate([first] * (tm // LANES), axis=1)
    pos1 = jnp.sum(jnp.where(oh1, place, 0.0), axis=0, keepdims=True).astype(jnp.int32)
    pos2 = jnp.sum(jnp.where(oh2, place, 0.0), axis=0, keepdims=True).astype(jnp.int32)
    x2b = x2.astype(BF16)
    for r0 in range(0, xs_ref.shape[0], SORT_CHUNK):
        r = lax.broadcasted_iota(jnp.int32, (SORT_CHUNK, tm), 0) + r0
        perm = jnp.where(r == pos1, 1.0, jnp.where(r == pos2, 1.0, 0.0)).astype(BF16)
        xs_ref[r0:r0 + SORT_CHUNK, :] = jnp.dot(perm, x2b, preferred_element_type=F32).astype(BF16)


def _norm2_and_route(x, mv, ng_ref, route_refs, out_refs):
    xo_ref, xs_ref, ids_ref, gts_ref, cnt_ref = out_refs
    xo_ref[...] = x
    x2 = _rms(x, ng_ref[1:2]) * (1.0 + mv[4:5]) + mv[3:4]
    _route(x2, *route_refs, xs_ref, ids_ref, gts_ref, cnt_ref)


def _full(shape):
    return pl.BlockSpec(shape, lambda i: (0,) * len(shape))


def _tile_specs(tm, d, has_comb, tiles_per_batch):
    tok = pl.BlockSpec((tm, d), lambda i: (i, 0))
    specs = [tok]
    if has_comb:
        specs += [tok, tok, pl.BlockSpec((tm, 2), lambda i: (i, 0))]
    specs.append(pl.BlockSpec((None, 8, d), lambda i: (i // tiles_per_batch, 0, 0)))
    specs.append(_full((2, d)))
    return specs


def _route_specs(d, tm):
    return [_full((ROUTE_ROWS, d)), _full((ROUTE_ROWS, d)), _full((ROUTE_ROWS, 1)), _full((tm, tm))]


def _sorted_rows(tm):
    return 2 * tm + N_EXPERTS * SORT_PAD


def _mixer_out(t, d, tm, h2_rows, h2_row0):
    sr = _sorted_rows(tm)
    assert h2_row0 % sr == 0
    off = h2_row0 // sr
    tok = pl.BlockSpec((tm, d), lambda i: (i, 0))
    specs = [tok, pl.BlockSpec((sr, d), lambda i: (i + off, 0)),
             pl.BlockSpec((4, tm), lambda i: (0, i)), pl.BlockSpec((2, tm), lambda i: (0, i)),
             pl.BlockSpec((1, N_EXPERTS, LANES), lambda i: (i, 0, 0))]
    shapes = [jax.ShapeDtypeStruct((t, d), F32), jax.ShapeDtypeStruct((h2_rows, d), BF16),
              jax.ShapeDtypeStruct((4, t), jnp.int32), jax.ShapeDtypeStruct((2, t), F32),
              jax.ShapeDtypeStruct((t // tm, N_EXPERTS, LANES), F32)]
    return specs, shapes


def _in_a_kernel(*refs, has_comb):
    n_in = 4 if has_comb else 1
    x_ref, comb, refs = refs[0], refs[1:n_in], refs[n_in:]
    mv_ref, ng_ref, wqt_ref, wk_ref, wvt_ref, wot_ref, wgt_ref, bg_ref = refs[:8]
    outs = refs[8:]
    if has_comb:
        xo_ref, outs = outs[0], outs[1:]
    qt_ref, k_ref, vt_ref, ot_ref, a_ref = outs
    mv = mv_ref[...]
    x = _residual_in(x_ref, comb, mv)
    if has_comb:
        xo_ref[...] = x
    h = (_rms(x, ng_ref[0:1]) * (1.0 + mv[1:2]) + mv[0:1]).astype(BF16)
    qt_ref[0] = (_nt_dot(wqt_ref[...], h) * (DQK ** -0.5)).astype(BF16)
    k_ref[...] = jnp.dot(h, wk_ref[...], preferred_element_type=F32).astype(BF16)
    vt_ref[0] = _nt_dot(wvt_ref[...], h).astype(BF16)
    ot_ref[0] = jax.nn.sigmoid(_nt_dot(wot_ref[...], h)).astype(BF16)
    g = _nt_dot(wgt_ref[...], h) + bg_ref[...]
    gi, gf = g[0:8], g[8:16]
    a_ref[0, 0:8, :] = IGATE_SOFTCAP * jnp.tanh(gi / IGATE_SOFTCAP)
    a_ref[0, 8:16, :] = jnp.minimum(gf, 0.0) - jnp.log(1.0 + jnp.exp(-jnp.abs(gf)))


def _in_a(x, comb, modv, ng, w, batch, tm):
    t, d = x.shape
    s = t // batch
    tpb = s // tm
    has_comb = bool(comb)
    in_specs = _tile_specs(tm, d, has_comb, tpb) + [_full(a.shape) for a in w]
    tok = lambda n: pl.BlockSpec((tm, n), lambda i: (i, 0))
    rows = lambda n: pl.BlockSpec((1, n, tm), lambda i: (i // tpb, 0, i % tpb))
    out_specs = [rows(NQ), tok(NQ), rows(NV), rows(d), rows(16)]
    out_shape = [jax.ShapeDtypeStruct((batch, NQ, s), BF16), jax.ShapeDtypeStruct((t, NQ), BF16),
                 jax.ShapeDtypeStruct((batch, NV, s), BF16), jax.ShapeDtypeStruct((batch, d, s), BF16),
                 jax.ShapeDtypeStruct((batch, 16, s), F32)]
    if has_comb:
        out_specs = [tok(d)] + out_specs
        out_shape = [jax.ShapeDtypeStruct((t, d), F32)] + out_shape
    outs = pl.pallas_call(
        functools.partial(_in_a_kernel, has_comb=has_comb),
        grid=(t // tm,), in_specs=in_specs, out_specs=out_specs, out_shape=out_shape,
        compiler_params=_cparams("parallel"),
    )(x, *comb, modv, ng, *w)
    if has_comb:
        return outs[0], outs[1:]
    return x, outs


def _chunk_scan(x, op, fill, forward):
    length = x.shape[-1]
    pos = lax.broadcasted_iota(jnp.int32, x.shape, 1)
    s = 1
    while s < length:
        if forward:
            x = op(x, jnp.where(pos >= s, pltpu.roll(x, s, axis=1), fill))
        else:
            x = op(x, jnp.where(pos < length - s, pltpu.roll(x, length - s, axis=1), fill))
        s *= 2
    return x


def _gate_kernel(a_ref, o_ref):
    for r in range(2 * HEADS):
        fwd = r < HEADS
        b = _chunk_scan(a_ref[0, 8 + r], jnp.add, 0.0, fwd)
        u = a_ref[0, r] - b
        cm = _chunk_scan(u, jnp.maximum, -jnp.inf, fwd)
        end = b.shape[-1] - 1 if fwd else 0
        o_ref[0, r] = u
        o_ref[0, 8 + r] = b
        o_ref[0, 16 + r] = cm
        o_ref[0, 24 + r] = jnp.broadcast_to(cm[:, end:end + 1], cm.shape)
        o_ref[0, 32 + r] = jnp.broadcast_to(b[:, end:end + 1], b.shape)


GATE_ROWS = 40


def _gate_scans(act, chunk):
    batch, _, s = act.shape
    nc = s // chunk
    g = pl.pallas_call(
        _gate_kernel, grid=(batch,),
        in_specs=[pl.BlockSpec((1, 16, nc, chunk), lambda b: (b, 0, 0, 0))],
        out_specs=pl.BlockSpec((1, GATE_ROWS, nc, chunk), lambda b: (b, 0, 0, 0)),
        out_shape=jax.ShapeDtypeStruct((batch, GATE_ROWS, nc, chunk), F32),
        compiler_params=_cparams("parallel"),
    )(act.reshape(batch, 16, nc, chunk))
    rows = jnp.transpose(g, (0, 2, 1, 3))
    cols = jnp.transpose(g[:, 0:8], (0, 2, 3, 1))
    return rows, cols


def _mlstm_kernel(qf_ref, kf_ref, vf_ref, rf_ref, cf_ref, qb_ref, kb_ref, vb_ref, rb_ref, cb_ref,
                  c0_ref, n0_ref, m0_ref, hf_ref, hb_ref, ct_ref, nt_ref, mt_ref,
                  c_s, n_s, m_s, *, chunk):
    step = pl.program_id(1)
    last = pl.num_programs(1) - 1

    @pl.when(step == 0)
    def _():
        c_s[...] = c0_ref[0]
        n_s[...] = n0_ref[0]
        m_s[...] = m0_ref[0]

    si = lax.broadcasted_iota(jnp.int32, (chunk, chunk), 0)
    ji = lax.broadcasted_iota(jnp.int32, (chunk, chunk), 1)
    sub = qf_ref.shape[2] // chunk
    dirs = ((qf_ref, kf_ref, vf_ref, rf_ref, cf_ref, hf_ref),
            (qb_ref, kb_ref, vb_ref, rb_ref, cb_ref, hb_ref))
    units = [(j if d == 0 else sub - 1 - j, d, h, dirs[d])
             for j in range(sub) for d in range(2) for h in range(HEADS)]
    per_sub = 2 * HEADS
    for u0, (c, d, h, (q_ref, k_ref, v_ref, r_ref, c_ref, h_ref)) in enumerate(units):
        if u0 % per_sub == 0:
            scores = []
            for c2, d2, h2, (q2_ref, k2_ref, _, _, _, _) in units[u0:u0 + per_sub]:
                qt2 = q2_ref[0, h2 * DQK:(h2 + 1) * DQK, c2 * chunk:(c2 + 1) * chunk]
                k2 = k2_ref[c2 * chunk:(c2 + 1) * chunk, h2 * DQK:(h2 + 1) * DQK]
                n16 = jnp.broadcast_to(n_s[d2 * HEADS + h2], (16, DQK)).astype(BF16)
                scores.append(jnp.dot(jnp.concatenate([k2, n16], axis=0), qt2,
                                      preferred_element_type=F32))
        sx = scores[u0 % per_sub]
        mask = (si <= ji) if d == 0 else (si >= ji)
        idx = d * HEADS + h
        lanes = slice(c * chunk, (c + 1) * chunk)
        qt = q_ref[0, h * DQK:(h + 1) * DQK, lanes]
        k = k_ref[c * chunk:(c + 1) * chunk, h * DQK:(h + 1) * DQK]
        vt = v_ref[0, h * DV:(h + 1) * DV, lanes]
        u_col = c_ref[0, c, :, idx:idx + 1]
        u_row = r_ref[0, c, idx:idx + 1, :]
        b_row = r_ref[0, c, 8 + idx:9 + idx, :]
        cm_row = r_ref[0, c, 16 + idx:17 + idx, :]
        cm_end = r_ref[0, c, 24 + idx:25 + idx, :]
        b_end = r_ref[0, c, 32 + idx:33 + idx, :]
        ct_st = c_s[idx]
        n_st = n_s[idx]
        m_st = m_s[idx]
        mm = jnp.maximum(m_st, cm_row)
        dt = jnp.where(mask, jnp.exp(u_col - mm), 0.0)
        pt = sx[:chunk] * dt
        a_int = jnp.exp(m_st - mm)
        den = jnp.sum(pt, axis=0, keepdims=True) + a_int * sx[chunk:chunk + 1]
        inv = 1.0 / jnp.maximum(jnp.abs(den), jnp.exp(-b_row - mm))
        wts = jnp.concatenate([(pt * inv).astype(BF16),
                               (qt.astype(F32) * (a_int * inv)).astype(BF16)], axis=0)
        vals = jnp.concatenate([vt, ct_st.astype(BF16)], axis=1)
        h_ref[0, h * DV:(h + 1) * DV, lanes] = jnp.dot(vals, wts, preferred_element_type=F32).astype(BF16)
        mm_end = jnp.maximum(m_st, cm_end)
        a_end = jnp.exp(u_row - mm_end)
        decay = jnp.exp(m_st - mm_end)
        upd = jnp.concatenate([(vt.astype(F32) * a_end).astype(BF16),
                               jnp.broadcast_to(a_end, (16, chunk)).astype(BF16)], axis=0)
        upd = jnp.dot(upd, k, preferred_element_type=F32)
        c_s[idx] = decay * ct_st + upd[:DV]
        n_s[idx] = decay * n_st + upd[DV:DV + 1]
        m_s[idx] = b_end + mm_end

    @pl.when(step == last)
    def _():
        ct_ref[0] = c_s[...]
        nt_ref[0] = n_s[...]
        mt_ref[0] = m_s[...]


def _mlstm_zero_state(batch, chunk):
    return (jnp.zeros((batch, 8, DV, DQK), F32), jnp.zeros((batch, 8, 1, DQK), F32),
            jnp.zeros((batch, 8, 1, chunk), F32))


def _mlstm(qt, k, vt, rows, cols, state, batch, chunk):
    t = k.shape[0]
    s = t // batch
    nc = s // chunk
    assert chunk == DQK
    c0, n0, m0 = state
    sub = MLSTM_SUB if nc % MLSTM_SUB == 0 else 1
    span = sub * chunk
    ns = nc // sub

    def specs(cidx):
        return [pl.BlockSpec((1, NQ, span), lambda b, c: (b, 0, cidx(c))),
                pl.BlockSpec((span, NQ), lambda b, c: (b * ns + cidx(c), 0)),
                pl.BlockSpec((1, NV, span), lambda b, c: (b, 0, cidx(c))),
                pl.BlockSpec((1, sub, GATE_ROWS, chunk), lambda b, c: (b, cidx(c), 0, 0)),
                pl.BlockSpec((1, sub, chunk, 8), lambda b, c: (b, cidx(c), 0, 0))]

    fwd = lambda c: c
    bwd = lambda c: ns - 1 - c
    st_specs = [pl.BlockSpec((1, 8, DV, DQK), lambda b, c: (b, 0, 0, 0)),
                pl.BlockSpec((1, 8, 1, DQK), lambda b, c: (b, 0, 0, 0)),
                pl.BlockSpec((1, 8, 1, chunk), lambda b, c: (b, 0, 0, 0))]
    st_shape = [jax.ShapeDtypeStruct(a.shape, F32) for a in state]
    h_shape = jax.ShapeDtypeStruct((batch, NV, s), BF16)
    args = (qt, k, vt, rows, cols)
    hf, hb, ct, nt, mt = pl.pallas_call(
        functools.partial(_mlstm_kernel, chunk=chunk),
        grid=(batch, ns),
        in_specs=specs(fwd) + specs(bwd) + st_specs,
        out_specs=[pl.BlockSpec((1, NV, span), lambda b, c: (b, 0, fwd(c))),
                   pl.BlockSpec((1, NV, span), lambda b, c: (b, 0, bwd(c)))] + st_specs,
        out_shape=[h_shape, h_shape] + st_shape,
        scratch_shapes=[pltpu.VMEM((8, DV, DQK), F32), pltpu.VMEM((8, 1, DQK), F32),
                        pltpu.VMEM((8, 1, chunk), F32)],
        compiler_params=_cparams("arbitrary", "arbitrary"),
    )(*args, *args, c0, n0, m0)
    return hf, hb, (ct, nt, mt)


def _out_a_kernel(*refs, aliased):
    hf_ref, hb_ref, o_ref, x_ref, mv_ref, ng_ref, hg_ref, wout_ref = refs[:8]
    route_refs = refs[8:12]
    out_refs = refs[12 + int(aliased):]
    mv = mv_ref[...]
    hsum = hf_ref[0].astype(F32) + hb_ref[0].astype(F32)
    parts = []
    for h in range(HEADS):
        hh = hsum[h * DV:(h + 1) * DV]
        parts.append(hh * lax.rsqrt(jnp.mean(hh * hh, axis=0, keepdims=True) + EPS))
    hn = jnp.concatenate(parts, axis=0) * hg_ref[...] * o_ref[0].astype(F32)
    y = jnp.dot(wout_ref[...], hn.astype(BF16), preferred_element_type=F32).T
    x = x_ref[...] + mv[2:3] * y
    _norm2_and_route(x, mv, ng_ref, route_refs, out_refs)


def _out_a(hf, hb, o, x, modv, ng, head_g, w_out_t, route_w, batch, tm, h2_buf, h2_rows, h2_row0):
    t, d = x.shape
    tpb = t // batch // tm
    rows = lambda n: pl.BlockSpec((1, n, tm), lambda i: (i // tpb, 0, i % tpb))
    in_specs = ([rows(NV), rows(NV), rows(d)] + _tile_specs(tm, d, False, tpb)
                + [_full((NV, 1)), _full(w_out_t.shape)] + _route_specs(d, tm))
    w_out = w_out_t
    args = [hf, hb, o, x, modv, ng, head_g, w_out, *route_w]
    return _mixer_call(functools.partial(_out_a_kernel, aliased=h2_buf is not None),
                       in_specs, args, t, d, tm, h2_buf, h2_rows, h2_row0)


def _mixer_call(body, in_specs, args, t, d, tm, h2_buf, h2_rows, h2_row0):
    out_specs, out_shape = _mixer_out(t, d, tm, h2_rows, h2_row0)
    aliases = {}
    if h2_buf is not None:
        aliases = {len(args): 1}
        in_specs = in_specs + [pl.BlockSpec(memory_space=pl.ANY)]
        args = args + [h2_buf]
    return pl.pallas_call(
        body, grid=(t // tm,), in_specs=in_specs, out_specs=out_specs, out_shape=out_shape,
        input_output_aliases=aliases, compiler_params=_cparams("parallel"),
    )(*args)


def _conv_kernel(*refs, has_comb, row_w, aliased):
    n_in = 4 if has_comb else 1
    x_ref, comb, refs = refs[0], refs[1:n_in], refs[n_in:]
    mv_ref, ng_ref, win_ref, cw_ref, wout_ref = refs[:5]
    route_refs = refs[5:9]
    out_refs = refs[9 + int(aliased):]
    mv = mv_ref[...]
    x = _residual_in(x_ref, comb, mv)
    d = x.shape[1]
    tm = x.shape[0]
    cw = cw_ref[...]
    groups = 2 if (tm // 2) % row_w == 0 else 1
    gm = tm // groups
    xg = [x[g * gm:(g + 1) * gm] for g in range(groups)]
    hg = [(_rms(a, ng_ref[0:1]) * (1.0 + mv[1:2]) + mv[0:1]).astype(BF16) for a in xg]
    pos = lax.broadcasted_iota(jnp.int32, (gm, d), 0) % row_w

    def in_proj(h):
        return (jnp.dot(h, win_ref[:, :d], preferred_element_type=F32),
                jnp.dot(h, win_ref[:, d:2 * d], preferred_element_type=F32),
                jnp.dot(h, win_ref[:, 2 * d:], preferred_element_type=F32))

    def gate(p):
        bg, cg, u = p
        z = cg * u
        left = jnp.where(pos == 0, 0.0, pltpu.roll(z, 1, axis=0))
        right = jnp.where(pos == row_w - 1, 0.0, pltpu.roll(z, gm - 1, axis=0))
        return (bg * (cw[0:1] * left + cw[1:2] * z + cw[2:3] * right)).astype(BF16)

    pg = [in_proj(h) for h in hg]
    outs = []
    for g in range(groups):
        yl = jnp.dot(gate(pg[g]), wout_ref[...], preferred_element_type=F32)
        outs.append(xg[g] + mv[2:3] * yl)
    x = jnp.concatenate(outs, axis=0) if groups > 1 else outs[0]
    _norm2_and_route(x, mv, ng_ref, route_refs, out_refs)


def _conv_layer(x, comb, modv, ng, w_in, conv_w, w_out, route_w, batch, tm, row_w, h2_buf, h2_rows, h2_row0):
    t, d = x.shape
    tpb = t // batch // tm
    in_specs = (_tile_specs(tm, d, bool(comb), tpb)
                + [_full(w_in.shape), _full(conv_w.shape), _full(w_out.shape)] + _route_specs(d, tm))
    args = [x, *comb, modv, ng, w_in, conv_w, w_out, *route_w]
    body = functools.partial(_conv_kernel, has_comb=bool(comb), row_w=row_w, aliased=h2_buf is not None)
    return _mixer_call(body, in_specs, args, t, d, tm, h2_buf, h2_rows, h2_row0)


def _moe_kernel(be_ref, nb_ref, src_ref, xs_ref, wg_ref, wu_ref, wd_ref, y_ref,
                xbuf, sem, wg_s, wu_s, wd_s):
    i = pl.program_id(0)
    nb = nb_ref[0]
    n_slots = GATHER_AHEAD + 1

    def piece_copies(blk, slot):
        copies = []
        for j in range(PIECES):
            src = pl.multiple_of(src_ref[blk * PIECES + j], SORT_PAD)
            copies.append(pltpu.make_async_copy(
                xs_ref.at[pl.ds(src, SORT_PAD), :],
                xbuf.at[slot, pl.ds(j * SORT_PAD, SORT_PAD), :], sem.at[slot]))
        return copies

    def wait_slot(slot):
        pltpu.make_async_copy(xs_ref.at[pl.ds(0, MOE_BLOCK), :], xbuf.at[slot], sem.at[slot]).wait()

    @pl.when(i < nb)
    def _():
        slot = lax.rem(i, n_slots)

        for b in range(GATHER_AHEAD):
            @pl.when((i == 0) & (b < nb))
            def _(b=b):
                for c in piece_copies(b, b):
                    c.start()

        @pl.when(i + GATHER_AHEAD < nb)
        def _():
            for c in piece_copies(i + GATHER_AHEAD, lax.rem(i + GATHER_AHEAD, n_slots)):
                c.start()

        prev = be_ref[jnp.maximum(i - 1, 0)]

        @pl.when((i == 0) | (be_ref[i] != prev))
        def _():
            wg_s[...] = wg_ref[0].astype(BF16)
            wu_s[...] = wu_ref[0].astype(BF16)
            wd_s[...] = wd_ref[0].astype(BF16)

        wait_slot(slot)
        gm = MOE_BLOCK // MOE_GROUPS
        gu = []
        for r in range(MOE_GROUPS):
            x = xbuf[slot, r * gm:(r + 1) * gm, :]
            gu.append((jnp.dot(x, wg_s[...], preferred_element_type=F32),
                       jnp.dot(x, wu_s[...], preferred_element_type=F32)))
        for r, (g, u) in enumerate(gu):
            a = (g * jax.nn.sigmoid(g) * u).astype(BF16)
            y_ref[r * gm:(r + 1) * gm, :] = jnp.dot(a, wd_s[...], preferred_element_type=F32).astype(BF16)


def _moe_experts(xs, blk_e, nb_used, piece_src, w_gate, w_up, w_down):
    d = xs.shape[1]
    de = w_gate.shape[-1]
    n_blocks = blk_e.shape[0]
    grid_spec = pltpu.PrefetchScalarGridSpec(
        num_scalar_prefetch=3, grid=(n_blocks,),
        in_specs=[pl.BlockSpec(memory_space=pl.ANY),
                  pl.BlockSpec((1, d, de), lambda i, be, nb, src: (be[jnp.minimum(i, nb[0] - 1)], 0, 0)),
                  pl.BlockSpec((1, d, de), lambda i, be, nb, src: (be[jnp.minimum(i, nb[0] - 1)], 0, 0)),
                  pl.BlockSpec((1, de, d), lambda i, be, nb, src: (be[jnp.minimum(i, nb[0] - 1)], 0, 0))],
        out_specs=pl.BlockSpec((MOE_BLOCK, d), lambda i, be, nb, src: (jnp.minimum(i, nb[0] - 1), 0)),
        scratch_shapes=[pltpu.VMEM((GATHER_AHEAD + 1, MOE_BLOCK, d), BF16),
                        pltpu.SemaphoreType.DMA((GATHER_AHEAD + 1,)),
                        pltpu.VMEM((d, de), BF16), pltpu.VMEM((d, de), BF16),
                        pltpu.VMEM((de, d), BF16)])
    return pl.pallas_call(
        _moe_kernel, grid_spec=grid_spec,
        out_shape=jax.ShapeDtypeStruct((n_blocks * MOE_BLOCK, d), BF16),
        compiler_params=_cparams("arbitrary"),
    )(blk_e, nb_used, piece_src, xs, w_gate, w_up, w_down)


def _dest_slots(ids, base, tm):
    t = ids.shape[1]
    nt = t // tm
    eid = ids[0:2].reshape(2, nt, tm)
    rank = ids[2:4].reshape(2, nt, tm)
    onehot = eid[..., None] == jnp.arange(N_EXPERTS, dtype=jnp.int32)
    off = jnp.sum(jnp.where(onehot, base[None, :, None, :], 0), axis=-1)
    return (rank + off).reshape(2, t)


def _moe(xs, routed, layer, w_gate, w_up, w_down):
    cnt_tiles = jnp.concatenate([cnt[:, :, 0] for _, cnt, _ in routed], axis=0).astype(jnp.int32)
    cnt_pad = (cnt_tiles + SORT_PAD - 1) // SORT_PAD * SORT_PAD
    run_first = jnp.cumsum(cnt_pad, axis=0) - cnt_pad
    in_tile = jnp.cumsum(cnt_pad, axis=1) - cnt_pad
    region = jnp.sum(cnt_pad, axis=0)
    padded = (region + MOE_BLOCK - 1) // MOE_BLOCK * MOE_BLOCK
    pends = jnp.cumsum(padded)
    pstart = pends - padded
    base = pstart[None, :] + run_first
    dests, tile_row0, row, xrow, worst = [], [], 0, 0, 0
    for ids, cnt, tm in routed:
        nt = cnt.shape[0]
        dests.append(_dest_slots(ids, base[row:row + nt], tm))
        tile_row0.append(xrow + jnp.arange(nt, dtype=jnp.int32) * _sorted_rows(tm))
        row += nt
        xrow += nt * _sorted_rows(tm)
        worst += nt * (2 * tm + N_EXPERTS * (SORT_PAD - 1))
    tile_row0 = jnp.concatenate(tile_row0)
    n_blocks = -(-(worst + N_EXPERTS * (MOE_BLOCK - 1)) // MOE_BLOCK)
    blk_start = jnp.arange(n_blocks, dtype=jnp.int32) * MOE_BLOCK
    blk_x = jnp.minimum(jnp.sum((pends[None, :] <= blk_start[:, None]).astype(jnp.int32), axis=1),
                        N_EXPERTS - 1)
    nb_used = (pends[-1:] // MOE_BLOCK).astype(jnp.int32)
    blk_oh = blk_x[:, None] == jnp.arange(N_EXPERTS, dtype=jnp.int32)[None, :]
    pick = lambda tab: jnp.sum(jnp.where(blk_oh[:, None, :], tab[None], 0), axis=2)
    per_piece = lambda a: jnp.repeat(a, PIECES, axis=0)
    blk_first = per_piece(pick(run_first))
    blk_src0 = per_piece(pick(in_tile - run_first) + tile_row0[None, :])
    blk_scal = per_piece(pick(jnp.stack([pstart, region])))
    piece_rank = jnp.arange(n_blocks * PIECES, dtype=jnp.int32) * SORT_PAD - blk_scal[:, 0]
    piece_tile = jnp.sum((blk_first <= piece_rank[:, None]).astype(jnp.int32), axis=1) - 1
    tile_oh = piece_tile[:, None] == jnp.arange(blk_first.shape[1], dtype=jnp.int32)[None, :]
    src = jnp.sum(jnp.where(tile_oh, blk_src0, 0), axis=1) + piece_rank
    piece_src = jnp.where(piece_rank < blk_scal[:, 1], src, 0).astype(jnp.int32)
    yb = _moe_experts(xs, blk_x + layer * N_EXPERTS, nb_used, piece_src, w_gate, w_up, w_down)
    return [(yb.at[d[0]].get(mode='promise_in_bounds'), yb.at[d[1]].get(mode='promise_in_bounds'))
            for d in dests]


def _final_kernel(x_ref, ya_ref, yb_ref, gt_ref, mv_ref, g_ref, o_ref):
    x = _residual_in(x_ref, (ya_ref, yb_ref, gt_ref), mv_ref[...])
    o_ref[...] = _rms(x, g_ref[...])


def _final(x, comb, modv, g, batch, tm):
    t, d = x.shape
    tpb = t // batch // tm
    tok = pl.BlockSpec((tm, d), lambda i: (i, 0))
    return pl.pallas_call(
        _final_kernel, grid=(t // tm,),
        in_specs=[tok, tok, tok, pl.BlockSpec((tm, 2), lambda i: (i, 0)),
                  pl.BlockSpec((None, 8, d), lambda i: (i // tpb, 0, 0)), _full((1, d))],
        out_specs=tok, out_shape=jax.ShapeDtypeStruct((t, d), F32),
        compiler_params=_cparams("parallel"),
    )(x, *comb, modv, g)


def _prep_a(w_in, b_gate):
    d = w_in.shape[0]
    wqt = w_in[:, :NQ].T.astype(BF16)
    wk = w_in[:, NQ:2 * NQ].astype(BF16)
    wvt = w_in[:, 2 * NQ:2 * NQ + NV].T.astype(BF16)
    wot = w_in[:, 2 * NQ + NV:2 * NQ + NV + d].T.astype(BF16)
    perm = jnp.array([0, 1, 2, 3, 8, 9, 10, 11, 4, 5, 6, 7, 12, 13, 14, 15], jnp.int32)
    wgt = w_in[:, 2 * NQ + NV + d:].T[perm].astype(BF16)
    bg = b_gate.astype(F32)[perm][:, None]
    return wqt, wk, wvt, wot, wgt, bg


def _prep_route(w_group, b_group, w_router, b_router, tm):
    d = w_group.shape[0]
    pad = ROUTE_ROWS - N_EXPERTS - N_GROUPS
    wt = jnp.concatenate([w_router.T, w_group.T, jnp.zeros((pad, d), F32)], axis=0).astype(F32)
    hi = wt.astype(BF16)
    lo = (wt - hi.astype(F32)).astype(BF16)
    rb = jnp.concatenate([b_router, b_group, jnp.zeros((pad,), F32)]).astype(F32)[:, None]
    tri = jnp.triu(jnp.ones((tm, tm), BF16), k=1)
    return hi, lo, rb, tri


def _modv(mod, l, rows):
    depth = mod.shape[0]
    d = mod.shape[-1] // 6
    zero = jnp.zeros((len(rows), 1, d), F32)
    cur = jnp.stack([mod[l, r].reshape(6, d) for r in rows]) if l < depth else jnp.zeros((len(rows), 6, d), F32)
    prev = jnp.stack([mod[l - 1, r].reshape(6, d)[5:6] for r in rows]) if l > 0 else zero
    return jnp.concatenate([cur, prev, zero], axis=1)


def kernel(x, c, ctx, c_ctx, mod_w, mod_b, norm_g, final_g, a_w_in, a_b_gate, a_head_g, a_w_out,
           b_w_in, b_conv_w, b_w_out, moe_w_group, moe_b_group, moe_w_router, moe_b_router,
           moe_w_gate, moe_w_up, moe_w_down):
    batch, seq, d = x.shape
    n_ctx = ctx.shape[1]
    depth = mod_w.shape[0]
    assert batch + 1 <= 8 and seq % MLSTM_CHUNK == 0 and n_ctx % MLSTM_CHUNK == 0
    tm = min(512, seq)
    t_lat = batch * seq
    t_ctx = batch * n_ctx
    assert t_lat % n_ctx == 0 and seq % GRID_W == 0

    cond = jnp.concatenate([c, c_ctx[None, :], jnp.zeros((8 - batch - 1, d), F32)], axis=0)
    mod = _modulation(cond, mod_w, mod_b)
    de = moe_w_gate.shape[-1]
    w_gate = moe_w_gate.reshape(depth * N_EXPERTS, d, de)
    w_up = moe_w_up.reshape(depth * N_EXPERTS, d, de)
    w_down = moe_w_down.reshape(depth * N_EXPERTS, de, d)

    lat = x.reshape(t_lat, d)
    cx = ctx.reshape(t_ctx, d)
    comb_lat, comb_ctx = (), ()
    for l in range(depth):
        kind, j = l % 2, l // 2
        ctx_after = any(i % 2 == 0 for i in range(l + 1, depth))
        mv_lat = _modv(mod, l, list(range(batch)))
        mv_ctx = _modv(mod, l, [batch] * batch)
        route_args = (moe_w_group[l], moe_b_group[l], moe_w_router[l], moe_b_router[l])
        route_lat = _prep_route(*route_args, tm)
        route_ctx = _prep_route(*route_args, n_ctx)
        lat_rows = t_lat // tm * _sorted_rows(tm)
        h2_rows = lat_rows + (t_ctx // n_ctx * _sorted_rows(n_ctx) if ctx_after else 0)
        if kind == 0:
            wa = _prep_a(a_w_in[j], a_b_gate[j])
            state = _mlstm_zero_state(batch, MLSTM_CHUNK)
            cx, (qtc, kc, vtc, oc, actc) = _in_a(cx, comb_ctx, mv_ctx, norm_g[l], wa, batch, n_ctx)
            hfc, hbc, state = _mlstm(qtc, kc, vtc, *_gate_scans(actc, MLSTM_CHUNK), state, batch, MLSTM_CHUNK)
            lat, (qtl, kl, vtl, ol, actl) = _in_a(lat, comb_lat, mv_lat, norm_g[l], wa, batch, tm)
            hfl, hbl, _ = _mlstm(qtl, kl, vtl, *_gate_scans(actl, MLSTM_CHUNK), state, batch, MLSTM_CHUNK)
            w_out = a_w_out[j].T.astype(BF16)
            head_g = a_head_g[j].astype(F32)[:, None]
            lat, h2, idl, gtl, cntl = _out_a(hfl, hbl, ol, lat, mv_lat, norm_g[l], head_g, w_out, route_lat,
                                             batch, tm, None, h2_rows, 0)
            if ctx_after:
                cx, h2, idc, gtc, cntc = _out_a(hfc, hbc, oc, cx, mv_ctx, norm_g[l], head_g, w_out, route_ctx,
                                                batch, n_ctx, h2, h2_rows, lat_rows)
        else:
            w_in = b_w_in[j].astype(BF16)
            w_out = b_w_out[j].astype(BF16)
            conv_w = b_conv_w[j].astype(F32)
            lat, h2, idl, gtl, cntl = _conv_layer(lat, comb_lat, mv_lat, norm_g[l], w_in, conv_w, w_out,
                                                  route_lat, batch, tm, GRID_W, None, h2_rows, 0)
            if ctx_after:
                cx, h2, idc, gtc, cntc = _conv_layer(cx, comb_ctx, mv_ctx, norm_g[l], w_in, conv_w, w_out,
                                                     route_ctx, batch, n_ctx, n_ctx, h2, h2_rows, lat_rows)
        routed = [(idl, cntl, tm)] + ([(idc, cntc, n_ctx)] if ctx_after else [])
        outs = _moe(h2, routed, l, w_gate, w_up, w_down)
        comb_lat = (*outs[0], gtl.T)
        comb_ctx = (*outs[1], gtc.T) if ctx_after else ()
    out = _final(lat, comb_lat, _modv(mod, depth, list(range(batch))), final_g.astype(F32)[None, :], batch, tm)
    return out.reshape(batch, seq, d)
```

```python
import functools

import jax
import jax.numpy as jnp
from jax import lax
from jax.experimental import pallas as pl
from jax.experimental.pallas import tpu as pltpu

F32 = jnp.float32
BF16 = jnp.bfloat16

EPS = 1e-6
HEADS = 4
DQK = 128
DV = 256
NQ = HEADS * DQK
NV = HEADS * DV
IGATE_SOFTCAP = 15.0
GRID_W = 64
N_GROUPS = 4
EPG = 8
N_EXPERTS = N_GROUPS * EPG
ROUTE_ROWS = 40
MOE_BLOCK = 512
GATHER_AHEAD = 2
MOE_GROUPS = 4
SORT_PAD = 16
SORT_CHUNK = 256
PIECES = MOE_BLOCK // SORT_PAD
MLSTM_CHUNK = 128
MLSTM_SUB = 4
LANES = 128
VMEM_LIMIT = 52 * 1024 * 1024


def _cparams(*sem):
    return pltpu.CompilerParams(dimension_semantics=sem, vmem_limit_bytes=VMEM_LIMIT)


def _nt_dot(a, b):
    return lax.dot_general(a, b, (((1,), (1,)), ((), ())), preferred_element_type=F32)


def _rms(x, g):
    return x * lax.rsqrt(jnp.mean(x * x, axis=-1, keepdims=True) + EPS) * g


def _mod_kernel(c_ref, w_ref, b_ref, o_ref):
    c = c_ref[...]
    s = (c * jax.nn.sigmoid(c)).astype(BF16)
    o_ref[0] = jnp.dot(s, w_ref[0].astype(BF16), preferred_element_type=F32) + b_ref[0]


def _modulation(cond, mod_w, mod_b):
    depth, d, n = mod_w.shape
    tn = 1024
    return pl.pallas_call(
        _mod_kernel,
        grid=(depth, n // tn),
        in_specs=[pl.BlockSpec((8, d), lambda l, j: (0, 0)),
                  pl.BlockSpec((1, d, tn), lambda l, j: (l, 0, j)),
                  pl.BlockSpec((1, 1, tn), lambda l, j: (l, 0, j))],
        out_specs=pl.BlockSpec((1, 8, tn), lambda l, j: (l, 0, j)),
        out_shape=jax.ShapeDtypeStruct((depth, 8, n), F32),
        compiler_params=_cparams("arbitrary", "arbitrary"),
    )(cond, mod_w, mod_b.reshape(depth, 1, n))


def _residual_in(x_ref, comb_refs, mv):
    x = x_ref[...]
    if comb_refs:
        ya_ref, yb_ref, gt_ref = comb_refs
        gt = gt_ref[...]
        x = x + mv[6:7] * (gt[:, 0:1] * ya_ref[...].astype(F32) + gt[:, 1:2] * yb_ref[...].astype(F32))
    return x


def _route(x2, wr_hi_ref, wr_lo_ref, rb_ref, tri_ref, ids_ref, gts_ref, cnt_ref):
    tm = x2.shape[0]
    x_hi = x2.astype(BF16)
    x_lo = (x2 - x_hi.astype(F32)).astype(BF16)
    w_hi = wr_hi_ref[...]
    lg = _nt_dot(w_hi, x_hi) + _nt_dot(wr_lo_ref[...], x_hi) + _nt_dot(w_hi, x_lo)
    lg = lg + rb_ref[...]
    row = lax.broadcasted_iota(jnp.int32, (EPG, tm), 0)
    gl = lg[N_EXPERTS:N_EXPERTS + EPG]
    gl = jnp.where(row < N_GROUPS, gl, -jnp.inf)
    gmx = jnp.max(gl, axis=0, keepdims=True)
    grp = jnp.min(jnp.where(gl == gmx, row, EPG), axis=0, keepdims=True)
    p_grp = 1.0 / jnp.sum(jnp.exp(gl - gmx), axis=0, keepdims=True)
    sel = lg[0:EPG]
    for g in range(1, N_GROUPS):
        sel = jnp.where(grp == g, lg[g * EPG:(g + 1) * EPG], sel)
    mx1 = jnp.max(sel, axis=0, keepdims=True)
    i1 = jnp.min(jnp.where(sel == mx1, row, EPG), axis=0, keepdims=True)
    rest = jnp.where(row == i1, -jnp.inf, sel)
    mx2 = jnp.max(rest, axis=0, keepdims=True)
    i2 = jnp.min(jnp.where(rest == mx2, row, EPG), axis=0, keepdims=True)
    e2 = jnp.exp(mx2 - mx1)
    inv = p_grp / (1.0 + e2)
    eid1 = grp * EPG + i1
    eid2 = grp * EPG + i2
    gts_ref[0:1, :] = inv
    gts_ref[1:2, :] = inv * e2
    erow = lax.broadcasted_iota(jnp.int32, (N_EXPERTS, tm), 0)
    oh1 = erow == eid1
    oh2 = erow == eid2
    member = jnp.where(oh1, 1.0, jnp.where(oh2, 1.0, 0.0)).astype(BF16)
    before = jnp.dot(member, tri_ref[...], preferred_element_type=F32)
    cnt = jnp.dot(member, jnp.ones((tm, LANES), BF16), preferred_element_type=F32)
    cnt_ref[0] = cnt
    ids_ref[0:1, :] = eid1
    ids_ref[1:2, :] = eid2
    ids_ref[2:3, :] = jnp.sum(jnp.where(oh1, before, 0.0), axis=0, keepdims=True).astype(jnp.int32)
    ids_ref[3:4, :] = jnp.sum(jnp.where(oh2, before, 0.0), axis=0, keepdims=True).astype(jnp.int32)
    cnt_pad = jnp.floor((cnt + (SORT_PAD - 1)) * (1.0 / SORT_PAD)) * SORT_PAD
    er = lax.broadcasted_iota(jnp.int32, (N_EXPERTS, N_EXPERTS), 0)
    ec = lax.broadcasted_iota(jnp.int32, (N_EXPERTS, N_EXPERTS), 1)
    lower = jnp.where(ec < er, 1.0, 0.0).astype(BF16)
    first = jnp.dot(lower, cnt_pad.astype(BF16), preferred_element_type=F32)
    place = before + jnp.concatenate([first] * (tm // LANES), axis=1)
    pos1 = jnp.sum(jnp.where(oh1, place, 0.0), axis=0, keepdims=True).astype(jnp.int32)
    pos2 = jnp.sum(jnp.where(oh2, place, 0.0), axis=0, keepdims=True).astype(jnp.int32)
    return pos1, pos2, x_hi


def _sort_rows(pos1, pos2, x2b, xs_ref):
    tm = x2b.shape[0]
    for r0 in range(0, xs_ref.shape[0], SORT_CHUNK):
        r = lax.broadcasted_iota(jnp.int32, (SORT_CHUNK, tm), 0) + r0
        perm = jnp.where(r == pos1, 1.0, jnp.where(r == pos2, 1.0, 0.0)).astype(BF16)
        xs_ref[r0:r0 + SORT_CHUNK, :] = jnp.dot(perm, x2b, preferred_element_type=F32).astype(BF16)


def _norm2(x, mv, ng_ref):
    return _rms(x, ng_ref[1:2]) * (1.0 + mv[4:5]) + mv[3:4]


def _norm2_and_route(xg, mv, ng_ref, route_refs, out_refs):
    xo_ref, xs_ref, ids_ref, gts_ref, cnt_ref = out_refs
    x2g = [_norm2(x, mv, ng_ref) for x in xg]
    xo_ref[...] = jnp.concatenate(xg, axis=0) if len(xg) > 1 else xg[0]
    x2 = jnp.concatenate(x2g, axis=0) if len(x2g) > 1 else x2g[0]
    _sort_rows(*_route(x2, *route_refs, ids_ref, gts_ref, cnt_ref), xs_ref)


def _full(shape):
    return pl.BlockSpec(shape, lambda i: (0,) * len(shape))


def _tile_specs(tm, d, has_comb, tiles_per_batch):
    tok = pl.BlockSpec((tm, d), lambda i: (i, 0))
    specs = [tok]
    if has_comb:
        specs += [tok, tok, pl.BlockSpec((tm, 2), lambda i: (i, 0))]
    specs.append(pl.BlockSpec((None, 8, d), lambda i: (i // tiles_per_batch, 0, 0)))
    specs.append(_full((2, d)))
    return specs


def _route_specs(d, tm):
    return [_full((ROUTE_ROWS, d)), _full((ROUTE_ROWS, d)), _full((ROUTE_ROWS, 1)), _full((tm, tm))]


def _sorted_rows(tm):
    return 2 * tm + N_EXPERTS * SORT_PAD


def _mixer_out(t, d, tm, h2_rows, h2_row0):
    sr = _sorted_rows(tm)
    assert h2_row0 % sr == 0
    off = h2_row0 // sr
    tok = pl.BlockSpec((tm, d), lambda i: (i, 0))
    specs = [tok, pl.BlockSpec((sr, d), lambda i: (i + off, 0)),
             pl.BlockSpec((4, tm), lambda i: (0, i)), pl.BlockSpec((2, tm), lambda i: (0, i)),
             pl.BlockSpec((1, N_EXPERTS, LANES), lambda i: (i, 0, 0))]
    shapes = [jax.ShapeDtypeStruct((t, d), F32), jax.ShapeDtypeStruct((h2_rows, d), BF16),
              jax.ShapeDtypeStruct((4, t), jnp.int32), jax.ShapeDtypeStruct((2, t), F32),
              jax.ShapeDtypeStruct((t // tm, N_EXPERTS, LANES), F32)]
    return specs, shapes


def _in_a_kernel(*refs, has_comb):
    n_in = 4 if has_comb else 1
    x_ref, comb, refs = refs[0], refs[1:n_in], refs[n_in:]
    mv_ref, ng_ref, wqt_ref, wk_ref, wvt_ref, wot_ref, wgt_ref, bg_ref = refs[:8]
    outs = refs[8:]
    if has_comb:
        xo_ref, outs = outs[0], outs[1:]
    qt_ref, k_ref, vt_ref, ot_ref, a_ref = outs
    mv = mv_ref[...]
    x = _residual_in(x_ref, comb, mv)
    if has_comb:
        xo_ref[...] = x
    tm = x.shape[0]
    gm = min(tm, 256)
    for r in range(tm // gm):
        tok = slice(r * gm, (r + 1) * gm)
        h = (_rms(x[tok], ng_ref[0:1]) * (1.0 + mv[1:2]) + mv[0:1]).astype(BF16)
        qt_ref[0, :, tok] = (_nt_dot(wqt_ref[...], h) * (DQK ** -0.5)).astype(BF16)
        k_ref[tok, :] = jnp.dot(h, wk_ref[...], preferred_element_type=F32).astype(BF16)
        vt_ref[0, :, tok] = _nt_dot(wvt_ref[...], h).astype(BF16)
        ot_ref[0, :, tok] = jax.nn.sigmoid(_nt_dot(wot_ref[...], h)).astype(BF16)
        g = _nt_dot(wgt_ref[...], h) + bg_ref[...]
        gi, gf = g[0:8], g[8:16]
        a_ref[0, 0:8, tok] = IGATE_SOFTCAP * jnp.tanh(gi / IGATE_SOFTCAP)
        a_ref[0, 8:16, tok] = jnp.minimum(gf, 0.0) - jnp.log(1.0 + jnp.exp(-jnp.abs(gf)))


def _in_a(x, comb, modv, ng, w, batch, tm):
    t, d = x.shape
    s = t // batch
    tpb = s // tm
    has_comb = bool(comb)
    in_specs = _tile_specs(tm, d, has_comb, tpb) + [_full(a.shape) for a in w]
    tok = lambda n: pl.BlockSpec((tm, n), lambda i: (i, 0))
    rows = lambda n: pl.BlockSpec((1, n, tm), lambda i: (i // tpb, 0, i % tpb))
    out_specs = [rows(NQ), tok(NQ), rows(NV), rows(d), rows(16)]
    out_shape = [jax.ShapeDtypeStruct((batch, NQ, s), BF16), jax.ShapeDtypeStruct((t, NQ), BF16),
                 jax.ShapeDtypeStruct((batch, NV, s), BF16), jax.ShapeDtypeStruct((batch, d, s), BF16),
                 jax.ShapeDtypeStruct((batch, 16, s), F32)]
    if has_comb:
        out_specs = [tok(d)] + out_specs
        out_shape = [jax.ShapeDtypeStruct((t, d), F32)] + out_shape
    outs = pl.pallas_call(
        functools.partial(_in_a_kernel, has_comb=has_comb),
        grid=(t // tm,), in_specs=in_specs, out_specs=out_specs, out_shape=out_shape,
        compiler_params=_cparams("parallel"),
    )(x, *comb, modv, ng, *w)
    if has_comb:
        return outs[0], outs[1:]
    return x, outs


def _chunk_scan(x, op, fill, forward):
    length = x.shape[-1]
    pos = lax.broadcasted_iota(jnp.int32, x.shape, 1)
    s = 1
    while s < length:
        if forward:
            x = op(x, jnp.where(pos >= s, pltpu.roll(x, s, axis=1), fill))
        else:
            x = op(x, jnp.where(pos < length - s, pltpu.roll(x, length - s, axis=1), fill))
        s *= 2
    return x


def _gate_kernel(a_ref, o_ref):
    for r in range(2 * HEADS):
        fwd = r < HEADS
        b = _chunk_scan(a_ref[0, 8 + r], jnp.add, 0.0, fwd)
        u = a_ref[0, r] - b
        cm = _chunk_scan(u, jnp.maximum, -jnp.inf, fwd)
        end = b.shape[-1] - 1 if fwd else 0
        o_ref[0, r] = u
        o_ref[0, 8 + r] = b
        o_ref[0, 16 + r] = cm
        o_ref[0, 24 + r] = jnp.broadcast_to(cm[:, end:end + 1], cm.shape)
        o_ref[0, 32 + r] = jnp.broadcast_to(b[:, end:end + 1], b.shape)


GATE_ROWS = 40


def _gate_scans(act, chunk):
    batch, _, s = act.shape
    nc = s // chunk
    g = pl.pallas_call(
        _gate_kernel, grid=(batch,),
        in_specs=[pl.BlockSpec((1, 16, nc, chunk), lambda b: (b, 0, 0, 0))],
        out_specs=pl.BlockSpec((1, GATE_ROWS, nc, chunk), lambda b: (b, 0, 0, 0)),
        out_shape=jax.ShapeDtypeStruct((batch, GATE_ROWS, nc, chunk), F32),
        compiler_params=_cparams("parallel"),
    )(act.reshape(batch, 16, nc, chunk))
    rows = jnp.transpose(g, (0, 2, 1, 3))
    cols = jnp.transpose(g[:, 0:8], (0, 2, 3, 1))
    return rows, cols


def _mlstm_kernel(qf_ref, kf_ref, vf_ref, rf_ref, cf_ref, qb_ref, kb_ref, vb_ref, rb_ref, cb_ref,
                  c0_ref, n0_ref, m0_ref, hf_ref, hb_ref, ct_ref, nt_ref, mt_ref,
                  c_s, n_s, m_s, *, chunk):
    step = pl.program_id(1)
    last = pl.num_programs(1) - 1

    @pl.when(step == 0)
    def _():
        c_s[...] = c0_ref[0]
        n_s[...] = n0_ref[0]
        m_s[...] = m0_ref[0]

    si = lax.broadcasted_iota(jnp.int32, (chunk, chunk), 0)
    ji = lax.broadcasted_iota(jnp.int32, (chunk, chunk), 1)
    sub = qf_ref.shape[2] // chunk
    dirs = ((qf_ref, kf_ref, vf_ref, rf_ref, cf_ref, hf_ref),
            (qb_ref, kb_ref, vb_ref, rb_ref, cb_ref, hb_ref))
    units = [(j if d == 0 else sub - 1 - j, d, h, dirs[d])
             for j in range(sub) for d in range(2) for h in range(HEADS)]
    per_sub = 2 * HEADS
    for u0, (c, d, h, (q_ref, k_ref, v_ref, r_ref, c_ref, h_ref)) in enumerate(units):
        if u0 % per_sub == 0:
            scores = []
            for c2, d2, h2, (q2_ref, k2_ref, _, _, _, _) in units[u0:u0 + per_sub]:
                qt2 = q2_ref[0, h2 * DQK:(h2 + 1) * DQK, c2 * chunk:(c2 + 1) * chunk]
                k2 = k2_ref[c2 * chunk:(c2 + 1) * chunk, h2 * DQK:(h2 + 1) * DQK]
                n16 = jnp.broadcast_to(n_s[d2 * HEADS + h2], (16, DQK)).astype(BF16)
                scores.append(jnp.dot(jnp.concatenate([k2, n16], axis=0), qt2,
                                      preferred_element_type=F32))
        sx = scores[u0 % per_sub]
        mask = (si <= ji) if d == 0 else (si >= ji)
        idx = d * HEADS + h
        lanes = slice(c * chunk, (c + 1) * chunk)
        qt = q_ref[0, h * DQK:(h + 1) * DQK, lanes]
        k = k_ref[c * chunk:(c + 1) * chunk, h * DQK:(h + 1) * DQK]
        vt = v_ref[0, h * DV:(h + 1) * DV, lanes]
        u_col = c_ref[0, c, :, idx:idx + 1]
        u_row = r_ref[0, c, idx:idx + 1, :]
        b_row = r_ref[0, c, 8 + idx:9 + idx, :]
        cm_row = r_ref[0, c, 16 + idx:17 + idx, :]
        cm_end = r_ref[0, c, 24 + idx:25 + idx, :]
        b_end = r_ref[0, c, 32 + idx:33 + idx, :]
        ct_st = c_s[idx]
        n_st = n_s[idx]
        m_st = m_s[idx]
        mm = jnp.maximum(m_st, cm_row)
        dt = jnp.where(mask, jnp.exp(u_col - mm), 0.0)
        pt = sx[:chunk] * dt
        a_int = jnp.exp(m_st - mm)
        den = jnp.sum(pt, axis=0, keepdims=True) + a_int * sx[chunk:chunk + 1]
        inv = 1.0 / jnp.maximum(jnp.abs(den), jnp.exp(-b_row - mm))
        wts = jnp.concatenate([(pt * inv).astype(BF16),
                               (qt.astype(F32) * (a_int * inv)).astype(BF16)], axis=0)
        vals = jnp.concatenate([vt, ct_st.astype(BF16)], axis=1)
        h_ref[0, h * DV:(h + 1) * DV, lanes] = jnp.dot(vals, wts, preferred_element_type=F32)
        mm_end = jnp.maximum(m_st, cm_end)
        a_end = jnp.exp(u_row - mm_end)
        decay = jnp.exp(m_st - mm_end)
        upd = jnp.concatenate([(vt.astype(F32) * a_end).astype(BF16),
                               jnp.broadcast_to(a_end, (16, chunk)).astype(BF16)], axis=0)
        upd = jnp.dot(upd, k, preferred_element_type=F32)
        c_s[idx] = decay * ct_st + upd[:DV]
        n_s[idx] = decay * n_st + upd[DV:DV + 1]
        m_s[idx] = b_end + mm_end

    @pl.when(step == last)
    def _():
        ct_ref[0] = c_s[...]
        nt_ref[0] = n_s[...]
        mt_ref[0] = m_s[...]


def _mlstm_zero_state(batch, chunk):
    return (jnp.zeros((batch, 8, DV, DQK), F32), jnp.zeros((batch, 8, 1, DQK), F32),
            jnp.zeros((batch, 8, 1, chunk), F32))


def _mlstm(qt, k, vt, rows, cols, state, batch, chunk):
    t = k.shape[0]
    s = t // batch
    nc = s // chunk
    assert chunk == DQK
    c0, n0, m0 = state
    sub = MLSTM_SUB if nc % MLSTM_SUB == 0 else 1
    span = sub * chunk
    ns = nc // sub

    def specs(cidx):
        return [pl.BlockSpec((1, NQ, span), lambda b, c: (b, 0, cidx(c))),
                pl.BlockSpec((span, NQ), lambda b, c: (b * ns + cidx(c), 0)),
                pl.BlockSpec((1, NV, span), lambda b, c: (b, 0, cidx(c))),
                pl.BlockSpec((1, sub, GATE_ROWS, chunk), lambda b, c: (b, cidx(c), 0, 0)),
                pl.BlockSpec((1, sub, chunk, 8), lambda b, c: (b, cidx(c), 0, 0))]

    fwd = lambda c: c
    bwd = lambda c: ns - 1 - c
    st_specs = [pl.BlockSpec((1, 8, DV, DQK), lambda b, c: (b, 0, 0, 0)),
                pl.BlockSpec((1, 8, 1, DQK), lambda b, c: (b, 0, 0, 0)),
                pl.BlockSpec((1, 8, 1, chunk), lambda b, c: (b, 0, 0, 0))]
    st_shape = [jax.ShapeDtypeStruct(a.shape, F32) for a in state]
    h_shape = jax.ShapeDtypeStruct((batch, NV, s), F32)
    args = (qt, k, vt, rows, cols)
    hf, hb, ct, nt, mt = pl.pallas_call(
        functools.partial(_mlstm_kernel, chunk=chunk),
        grid=(batch, ns),
        in_specs=specs(fwd) + specs(bwd) + st_specs,
        out_specs=[pl.BlockSpec((1, NV, span), lambda b, c: (b, 0, fwd(c))),
                   pl.BlockSpec((1, NV, span), lambda b, c: (b, 0, bwd(c)))] + st_specs,
        out_shape=[h_shape, h_shape] + st_shape,
        scratch_shapes=[pltpu.VMEM((8, DV, DQK), F32), pltpu.VMEM((8, 1, DQK), F32),
                        pltpu.VMEM((8, 1, chunk), F32)],
        compiler_params=_cparams("arbitrary", "arbitrary"),
    )(*args, *args, c0, n0, m0)
    return hf, hb, (ct, nt, mt)


def _out_a_kernel(*refs, aliased):
    hf_ref, hb_ref, o_ref, x_ref, mv_ref, ng_ref, hg_ref, wout_ref = refs[:8]
    route_refs = refs[8:12]
    out_refs = refs[12 + int(aliased):]
    mv = mv_ref[...]
    hsum = hf_ref[0] + hb_ref[0]
    parts = []
    for h in range(HEADS):
        hh = hsum[h * DV:(h + 1) * DV]
        parts.append(hh * lax.rsqrt(jnp.mean(hh * hh, axis=0, keepdims=True) + EPS))
    hn = jnp.concatenate(parts, axis=0) * hg_ref[...] * o_ref[0].astype(F32)
    y = jnp.dot(wout_ref[...], hn.astype(BF16), preferred_element_type=F32).T
    x = x_ref[...] + mv[2:3] * y
    _norm2_and_route([x], mv, ng_ref, route_refs, out_refs)


def _out_a(hf, hb, o, x, modv, ng, head_g, w_out_t, route_w, batch, tm, h2_buf, h2_rows, h2_row0):
    t, d = x.shape
    tpb = t // batch // tm
    rows = lambda n: pl.BlockSpec((1, n, tm), lambda i: (i // tpb, 0, i % tpb))
    in_specs = ([rows(NV), rows(NV), rows(d)] + _tile_specs(tm, d, False, tpb)
                + [_full((NV, 1)), _full(w_out_t.shape)] + _route_specs(d, tm))
    w_out = w_out_t
    args = [hf, hb, o, x, modv, ng, head_g, w_out, *route_w]
    return _mixer_call(functools.partial(_out_a_kernel, aliased=h2_buf is not None),
                       in_specs, args, t, d, tm, h2_buf, h2_rows, h2_row0)


def _mixer_call(body, in_specs, args, t, d, tm, h2_buf, h2_rows, h2_row0):
    out_specs, out_shape = _mixer_out(t, d, tm, h2_rows, h2_row0)
    aliases = {}
    if h2_buf is not None:
        aliases = {len(args): 1}
        in_specs = in_specs + [pl.BlockSpec(memory_space=pl.ANY)]
        args = args + [h2_buf]
    return pl.pallas_call(
        body, grid=(t // tm,), in_specs=in_specs, out_specs=out_specs, out_shape=out_shape,
        input_output_aliases=aliases, compiler_params=_cparams("parallel"),
    )(*args)


def _conv_kernel(*refs, has_comb, row_w, aliased):
    n_in = 4 if has_comb else 1
    x_ref, comb, refs = refs[0], refs[1:n_in], refs[n_in:]
    mv_ref, ng_ref, win_ref, cw_ref, wout_ref = refs[:5]
    route_refs = refs[5:9]
    out_refs = refs[9 + int(aliased):]
    mv = mv_ref[...]
    x = _residual_in(x_ref, comb, mv)
    d = x.shape[1]
    tm = x.shape[0]
    cw = cw_ref[...]
    groups = 4 if (tm // 4) % row_w == 0 else 1
    gm = tm // groups
    xg = [x[g * gm:(g + 1) * gm] for g in range(groups)]
    hg = [(_rms(a, ng_ref[0:1]) * (1.0 + mv[1:2]) + mv[0:1]).astype(BF16) for a in xg]
    pos = lax.broadcasted_iota(jnp.int32, (gm, d), 0) % row_w

    def in_proj(h):
        return (jnp.dot(h, win_ref[:, :d], preferred_element_type=F32),
                jnp.dot(h, win_ref[:, d:2 * d], preferred_element_type=F32),
                jnp.dot(h, win_ref[:, 2 * d:], preferred_element_type=F32))

    def gate(p):
        bg, cg, u = p
        z = cg * u
        left = jnp.where(pos == 0, 0.0, pltpu.roll(z, 1, axis=0))
        right = jnp.where(pos == row_w - 1, 0.0, pltpu.roll(z, gm - 1, axis=0))
        return (bg * (cw[0:1] * left + cw[1:2] * z + cw[2:3] * right)).astype(BF16)

    pg = [in_proj(h) for h in hg]
    outs = []
    for g in range(groups):
        yl = jnp.dot(gate(pg[g]), wout_ref[...], preferred_element_type=F32)
        outs.append(xg[g] + mv[2:3] * yl)
    _norm2_and_route(outs, mv, ng_ref, route_refs, out_refs)


def _conv_layer(x, comb, modv, ng, w_in, conv_w, w_out, route_w, batch, tm, row_w, h2_buf, h2_rows, h2_row0):
    t, d = x.shape
    tpb = t // batch // tm
    in_specs = (_tile_specs(tm, d, bool(comb), tpb)
                + [_full(w_in.shape), _full(conv_w.shape), _full(w_out.shape)] + _route_specs(d, tm))
    args = [x, *comb, modv, ng, w_in, conv_w, w_out, *route_w]
    body = functools.partial(_conv_kernel, has_comb=bool(comb), row_w=row_w, aliased=h2_buf is not None)
    return _mixer_call(body, in_specs, args, t, d, tm, h2_buf, h2_rows, h2_row0)


def _moe_kernel(be_ref, nb_ref, src_ref, xs_ref, wg_ref, wu_ref, wd_ref, y_ref,
                xbuf, sem, wg_s, wu_s, wd_s):
    i = pl.program_id(0)
    nb = nb_ref[0]
    n_slots = GATHER_AHEAD + 1

    def piece_copies(blk, slot):
        copies = []
        for j in range(PIECES):
            src = pl.multiple_of(src_ref[blk * PIECES + j], SORT_PAD)
            copies.append(pltpu.make_async_copy(
                xs_ref.at[pl.ds(src, SORT_PAD), :],
                xbuf.at[slot, pl.ds(j * SORT_PAD, SORT_PAD), :], sem.at[slot]))
        return copies

    def wait_slot(slot):
        pltpu.make_async_copy(xs_ref.at[pl.ds(0, MOE_BLOCK), :], xbuf.at[slot], sem.at[slot]).wait()

    @pl.when(i < nb)
    def _():
        slot = lax.rem(i, n_slots)

        for b in range(GATHER_AHEAD):
            @pl.when((i == 0) & (b < nb))
            def _(b=b):
                for c in piece_copies(b, b):
                    c.start()

        @pl.when(i + GATHER_AHEAD < nb)
        def _():
            for c in piece_copies(i + GATHER_AHEAD, lax.rem(i + GATHER_AHEAD, n_slots)):
                c.start()

        prev = be_ref[jnp.maximum(i - 1, 0)]

        @pl.when((i == 0) | (be_ref[i] != prev))
        def _():
            wg_s[...] = wg_ref[0].astype(BF16)
            wu_s[...] = wu_ref[0].astype(BF16)
            wd_s[...] = wd_ref[0].astype(BF16)

        wait_slot(slot)
        gm = MOE_BLOCK // MOE_GROUPS
        gu = []
        for r in range(MOE_GROUPS):
            x = xbuf[slot, r * gm:(r + 1) * gm, :]
            gu.append((jnp.dot(x, wg_s[...], preferred_element_type=F32),
                       jnp.dot(x, wu_s[...], preferred_element_type=F32)))
        for r, (g, u) in enumerate(gu):
            a = (g * jax.nn.sigmoid(g) * u).astype(BF16)
            y_ref[r * gm:(r + 1) * gm, :] = jnp.dot(a, wd_s[...], preferred_element_type=F32).astype(BF16)


def _moe_experts(xs, blk_e, nb_used, piece_src, w_gate, w_up, w_down):
    d = xs.shape[1]
    de = w_gate.shape[-1]
    n_blocks = blk_e.shape[0]
    grid_spec = pltpu.PrefetchScalarGridSpec(
        num_scalar_prefetch=3, grid=(n_blocks,),
        in_specs=[pl.BlockSpec(memory_space=pl.ANY),
                  pl.BlockSpec((1, d, de), lambda i, be, nb, src: (be[jnp.minimum(i, nb[0] - 1)], 0, 0)),
                  pl.BlockSpec((1, d, de), lambda i, be, nb, src: (be[jnp.minimum(i, nb[0] - 1)], 0, 0)),
                  pl.BlockSpec((1, de, d), lambda i, be, nb, src: (be[jnp.minimum(i, nb[0] - 1)], 0, 0))],
        out_specs=pl.BlockSpec((MOE_BLOCK, d), lambda i, be, nb, src: (jnp.minimum(i, nb[0] - 1), 0)),
        scratch_shapes=[pltpu.VMEM((GATHER_AHEAD + 1, MOE_BLOCK, d), BF16),
                        pltpu.SemaphoreType.DMA((GATHER_AHEAD + 1,)),
                        pltpu.VMEM((d, de), BF16), pltpu.VMEM((d, de), BF16),
                        pltpu.VMEM((de, d), BF16)])
    return pl.pallas_call(
        _moe_kernel, grid_spec=grid_spec,
        out_shape=jax.ShapeDtypeStruct((n_blocks * MOE_BLOCK, d), BF16),
        compiler_params=_cparams("arbitrary"),
    )(blk_e, nb_used, piece_src, xs, w_gate, w_up, w_down)


def _dest_slots(ids, base, tm):
    t = ids.shape[1]
    nt = t // tm
    eid = ids[0:2].reshape(2, nt, tm)
    rank = ids[2:4].reshape(2, nt, tm)
    onehot = eid[..., None] == jnp.arange(N_EXPERTS, dtype=jnp.int32)
    off = jnp.sum(jnp.where(onehot, base[None, :, None, :], 0), axis=-1)
    return (rank + off).reshape(2, t)


def _moe(xs, routed, layer, w_gate, w_up, w_down):
    cnt_tiles = jnp.concatenate([cnt[:, :, 0] for _, cnt, _ in routed], axis=0).astype(jnp.int32)
    cnt_pad = (cnt_tiles + SORT_PAD - 1) // SORT_PAD * SORT_PAD
    run_first = jnp.cumsum(cnt_pad, axis=0) - cnt_pad
    in_tile = jnp.cumsum(cnt_pad, axis=1) - cnt_pad
    region = jnp.sum(cnt_pad, axis=0)
    padded = (region + MOE_BLOCK - 1) // MOE_BLOCK * MOE_BLOCK
    pends = jnp.cumsum(padded)
    pstart = pends - padded
    base = pstart[None, :] + run_first
    dests, tile_row0, row, xrow, worst = [], [], 0, 0, 0
    for ids, cnt, tm in routed:
        nt = cnt.shape[0]
        dests.append(_dest_slots(ids, base[row:row + nt], tm))
        tile_row0.append(xrow + jnp.arange(nt, dtype=jnp.int32) * _sorted_rows(tm))
        row += nt
        xrow += nt * _sorted_rows(tm)
        worst += nt * (2 * tm + N_EXPERTS * (SORT_PAD - 1))
    tile_row0 = jnp.concatenate(tile_row0)
    n_blocks = -(-(worst + N_EXPERTS * (MOE_BLOCK - 1)) // MOE_BLOCK)
    blk_start = jnp.arange(n_blocks, dtype=jnp.int32) * MOE_BLOCK
    blk_x = jnp.minimum(jnp.sum((pends[None, :] <= blk_start[:, None]).astype(jnp.int32), axis=1),
                        N_EXPERTS - 1)
    nb_used = (pends[-1:] // MOE_BLOCK).astype(jnp.int32)
    blk_oh = blk_x[:, None] == jnp.arange(N_EXPERTS, dtype=jnp.int32)[None, :]
    pick = lambda tab: jnp.sum(jnp.where(blk_oh[:, None, :], tab[None], 0), axis=2)
    per_piece = lambda a: jnp.repeat(a, PIECES, axis=0)
    blk_first = per_piece(pick(run_first))
    blk_src0 = per_piece(pick(in_tile - run_first) + tile_row0[None, :])
    blk_scal = per_piece(pick(jnp.stack([pstart, region])))
    piece_rank = jnp.arange(n_blocks * PIECES, dtype=jnp.int32) * SORT_PAD - blk_scal[:, 0]
    piece_tile = jnp.sum((blk_first <= piece_rank[:, None]).astype(jnp.int32), axis=1) - 1
    tile_oh = piece_tile[:, None] == jnp.arange(blk_first.shape[1], dtype=jnp.int32)[None, :]
    src = jnp.sum(jnp.where(tile_oh, blk_src0, 0), axis=1) + piece_rank
    piece_src = jnp.where(piece_rank < blk_scal[:, 1], src, 0).astype(jnp.int32)
    yb = _moe_experts(xs, blk_x + layer * N_EXPERTS, nb_used, piece_src, w_gate, w_up, w_down)
    return [(yb.at[d[0]].get(mode='promise_in_bounds'), yb.at[d[1]].get(mode='promise_in_bounds'))
            for d in dests]


def _final_kernel(x_ref, ya_ref, yb_ref, gt_ref, mv_ref, g_ref, o_ref):
    x = _residual_in(x_ref, (ya_ref, yb_ref, gt_ref), mv_ref[...])
    o_ref[...] = _rms(x, g_ref[...])


def _final(x, comb, modv, g, batch, tm):
    t, d = x.shape
    tpb = t // batch // tm
    tok = pl.BlockSpec((tm, d), lambda i: (i, 0))
    return pl.pallas_call(
        _final_kernel, grid=(t // tm,),
        in_specs=[tok, tok, tok, pl.BlockSpec((tm, 2), lambda i: (i, 0)),
                  pl.BlockSpec((None, 8, d), lambda i: (i // tpb, 0, 0)), _full((1, d))],
        out_specs=tok, out_shape=jax.ShapeDtypeStruct((t, d), F32),
        compiler_params=_cparams("parallel"),
    )(x, *comb, modv, g)


def _prep_a(w_in, b_gate):
    d = w_in.shape[0]
    wqt = w_in[:, :NQ].T.astype(BF16)
    wk = w_in[:, NQ:2 * NQ].astype(BF16)
    wvt = w_in[:, 2 * NQ:2 * NQ + NV].T.astype(BF16)
    wot = w_in[:, 2 * NQ + NV:2 * NQ + NV + d].T.astype(BF16)
    perm = jnp.array([0, 1, 2, 3, 8, 9, 10, 11, 4, 5, 6, 7, 12, 13, 14, 15], jnp.int32)
    wgt = w_in[:, 2 * NQ + NV + d:].T[perm].astype(BF16)
    bg = b_gate.astype(F32)[perm][:, None]
    return wqt, wk, wvt, wot, wgt, bg


def _prep_route(w_group, b_group, w_router, b_router, tm):
    d = w_group.shape[0]
    pad = ROUTE_ROWS - N_EXPERTS - N_GROUPS
    wt = jnp.concatenate([w_router.T, w_group.T, jnp.zeros((pad, d), F32)], axis=0).astype(F32)
    hi = wt.astype(BF16)
    lo = (wt - hi.astype(F32)).astype(BF16)
    rb = jnp.concatenate([b_router, b_group, jnp.zeros((pad,), F32)]).astype(F32)[:, None]
    tri = jnp.triu(jnp.ones((tm, tm), BF16), k=1)
    return hi, lo, rb, tri


def _modv(mod, l, rows):
    depth = mod.shape[0]
    d = mod.shape[-1] // 6
    zero = jnp.zeros((len(rows), 1, d), F32)
    cur = jnp.stack([mod[l, r].reshape(6, d) for r in rows]) if l < depth else jnp.zeros((len(rows), 6, d), F32)
    prev = jnp.stack([mod[l - 1, r].reshape(6, d)[5:6] for r in rows]) if l > 0 else zero
    return jnp.concatenate([cur, prev, zero], axis=1)


def kernel(x, c, ctx, c_ctx, mod_w, mod_b, norm_g, final_g, a_w_in, a_b_gate, a_head_g, a_w_out,
           b_w_in, b_conv_w, b_w_out, moe_w_group, moe_b_group, moe_w_router, moe_b_router,
           moe_w_gate, moe_w_up, moe_w_down):
    batch, seq, d = x.shape
    n_ctx = ctx.shape[1]
    depth = mod_w.shape[0]
    assert batch + 1 <= 8 and seq % MLSTM_CHUNK == 0 and n_ctx % MLSTM_CHUNK == 0
    tm = min(512, seq)
    t_lat = batch * seq
    t_ctx = batch * n_ctx
    assert t_lat % n_ctx == 0 and seq % GRID_W == 0

    cond = jnp.concatenate([c, c_ctx[None, :], jnp.zeros((8 - batch - 1, d), F32)], axis=0)
    mod = _modulation(cond, mod_w, mod_b)
    de = moe_w_gate.shape[-1]
    w_gate = moe_w_gate.reshape(depth * N_EXPERTS, d, de)
    w_up = moe_w_up.reshape(depth * N_EXPERTS, d, de)
    w_down = moe_w_down.reshape(depth * N_EXPERTS, de, d)

    lat = x.reshape(t_lat, d)
    cx = ctx.reshape(t_ctx, d)
    comb_lat, comb_ctx = (), ()
    for l in range(depth):
        kind, j = l % 2, l // 2
        ctx_after = any(i % 2 == 0 for i in range(l + 1, depth))
        mv_lat = _modv(mod, l, list(range(batch)))
        mv_ctx = _modv(mod, l, [batch] * batch)
        route_args = (moe_w_group[l], moe_b_group[l], moe_w_router[l], moe_b_router[l])
        route_lat = _prep_route(*route_args, tm)
        route_ctx = _prep_route(*route_args, n_ctx)
        lat_rows = t_lat // tm * _sorted_rows(tm)
        h2_rows = lat_rows + (t_ctx // n_ctx * _sorted_rows(n_ctx) if ctx_after else 0)
        if kind == 0:
            wa = _prep_a(a_w_in[j], a_b_gate[j])
            state = _mlstm_zero_state(batch, MLSTM_CHUNK)
            cx, (qtc, kc, vtc, oc, actc) = _in_a(cx, comb_ctx, mv_ctx, norm_g[l], wa, batch, n_ctx)
            hfc, hbc, state = _mlstm(qtc, kc, vtc, *_gate_scans(actc, MLSTM_CHUNK), state, batch, MLSTM_CHUNK)
            lat, (qtl, kl, vtl, ol, actl) = _in_a(lat, comb_lat, mv_lat, norm_g[l], wa, batch, tm)
            hfl, hbl, _ = _mlstm(qtl, kl, vtl, *_gate_scans(actl, MLSTM_CHUNK), state, batch, MLSTM_CHUNK)
            w_out = a_w_out[j].T.astype(BF16)
            head_g = a_head_g[j].astype(F32)[:, None]
            lat, h2, idl, gtl, cntl = _out_a(hfl, hbl, ol, lat, mv_lat, norm_g[l], head_g, w_out, route_lat,
                                             batch, tm, None, h2_rows, 0)
            if ctx_after:
                cx, h2, idc, gtc, cntc = _out_a(hfc, hbc, oc, cx, mv_ctx, norm_g[l], head_g, w_out, route_ctx,
                                                batch, n_ctx, h2, h2_rows, lat_rows)
        else:
            w_in = b_w_in[j].astype(BF16)
            w_out = b_w_out[j].astype(BF16)
            conv_w = b_conv_w[j].astype(F32)
            lat, h2, idl, gtl, cntl = _conv_layer(lat, comb_lat, mv_lat, norm_g[l], w_in, conv_w, w_out,
                                                  route_lat, batch, tm, GRID_W, None, h2_rows, 0)
            if ctx_after:
                cx, h2, idc, gtc, cntc = _conv_layer(cx, comb_ctx, mv_ctx, norm_g[l], w_in, conv_w, w_out,
                                                     route_ctx, batch, n_ctx, n_ctx, h2, h2_rows, lat_rows)
        routed = [(idl, cntl, tm)] + ([(idc, cntc, n_ctx)] if ctx_after else [])
        outs = _moe(h2, routed, l, w_gate, w_up, w_down)
        comb_lat = (*outs[0], gtl.T)
        comb_ctx = (*outs[1], gtc.T) if ctx_after else ()
    out = _final(lat, comb_lat, _modv(mod, depth, list(range(batch))), final_g.astype(F32)[None, :], batch, tm)
    return out.reshape(batch, seq, d)
```

```python
import functools

import jax
import jax.numpy as jnp
from jax import lax
from jax.experimental import pallas as pl
from jax.experimental.pallas import tpu as pltpu

F32 = jnp.float32
BF16 = jnp.bfloat16

EPS = 1e-6
HEADS = 4
DQK = 128
DV = 256
NQ = HEADS * DQK
NV = HEADS * DV
IGATE_SOFTCAP = 15.0
GRID_W = 64
N_GROUPS = 4
EPG = 8
N_EXPERTS = N_GROUPS * EPG
ROUTE_ROWS = 40
MOE_BLOCK = 512
GATHER_AHEAD = 2
MOE_GROUPS = 4
SORT_PAD = 16
SORT_CHUNK = 256
PIECES = MOE_BLOCK // SORT_PAD
MLSTM_CHUNK = 128
MLSTM_SUB = 4
LANES = 128
VMEM_LIMIT = 52 * 1024 * 1024


def _cparams(*sem):
    return pltpu.CompilerParams(dimension_semantics=sem, vmem_limit_bytes=VMEM_LIMIT)


def _nt_dot(a, b):
    return lax.dot_general(a, b, (((1,), (1,)), ((), ())), preferred_element_type=F32)


def _rms(x, g):
    return x * lax.rsqrt(jnp.mean(x * x, axis=-1, keepdims=True) + EPS) * g


def _mod_kernel(c_ref, w_ref, b_ref, o_ref):
    c = c_ref[...]
    s = (c * jax.nn.sigmoid(c)).astype(BF16)
    o_ref[0] = jnp.dot(s, w_ref[0].astype(BF16), preferred_element_type=F32) + b_ref[0]


def _modulation(cond, mod_w, mod_b):
    depth, d, n = mod_w.shape
    tn = 1024
    return pl.pallas_call(
        _mod_kernel,
        grid=(depth, n // tn),
        in_specs=[pl.BlockSpec((8, d), lambda l, j: (0, 0)),
                  pl.BlockSpec((1, d, tn), lambda l, j: (l, 0, j)),
                  pl.BlockSpec((1, 1, tn), lambda l, j: (l, 0, j))],
        out_specs=pl.BlockSpec((1, 8, tn), lambda l, j: (l, 0, j)),
        out_shape=jax.ShapeDtypeStruct((depth, 8, n), F32),
        compiler_params=_cparams("arbitrary", "arbitrary"),
    )(cond, mod_w, mod_b.reshape(depth, 1, n))


def _residual_in(x_ref, comb_refs, mv):
    x = x_ref[...]
    if comb_refs:
        ya_ref, yb_ref, gt_ref = comb_refs
        gt = gt_ref[...]
        x = x + mv[6:7] * (gt[:, 0:1] * ya_ref[...].astype(F32) + gt[:, 1:2] * yb_ref[...].astype(F32))
    return x


def _route(x2, wr_hi_ref, wr_lo_ref, rb_ref, tri_ref, ids_ref, gts_ref, cnt_ref):
    tm = x2.shape[0]
    x_hi = x2.astype(BF16)
    x_lo = (x2 - x_hi.astype(F32)).astype(BF16)
    w_hi = wr_hi_ref[...]
    lg = _nt_dot(w_hi, x_hi) + _nt_dot(wr_lo_ref[...], x_hi) + _nt_dot(w_hi, x_lo)
    lg = lg + rb_ref[...]
    row = lax.broadcasted_iota(jnp.int32, (EPG, tm), 0)
    gl = lg[N_EXPERTS:N_EXPERTS + EPG]
    gl = jnp.where(row < N_GROUPS, gl, -jnp.inf)
    gmx = jnp.max(gl, axis=0, keepdims=True)
    grp = jnp.min(jnp.where(gl == gmx, row, EPG), axis=0, keepdims=True)
    p_grp = 1.0 / jnp.sum(jnp.exp(gl - gmx), axis=0, keepdims=True)
    sel = lg[0:EPG]
    for g in range(1, N_GROUPS):
        sel = jnp.where(grp == g, lg[g * EPG:(g + 1) * EPG], sel)
    mx1 = jnp.max(sel, axis=0, keepdims=True)
    i1 = jnp.min(jnp.where(sel == mx1, row, EPG), axis=0, keepdims=True)
    rest = jnp.where(row == i1, -jnp.inf, sel)
    mx2 = jnp.max(rest, axis=0, keepdims=True)
    i2 = jnp.min(jnp.where(rest == mx2, row, EPG), axis=0, keepdims=True)
    e2 = jnp.exp(mx2 - mx1)
    inv = p_grp / (1.0 + e2)
    eid1 = grp * EPG + i1
    eid2 = grp * EPG + i2
    gts_ref[0:1, :] = inv
    gts_ref[1:2, :] = inv * e2
    erow = lax.broadcasted_iota(jnp.int32, (N_EXPERTS, tm), 0)
    oh1 = erow == eid1
    oh2 = erow == eid2
    member = jnp.where(oh1, 1.0, jnp.where(oh2, 1.0, 0.0)).astype(BF16)
    before = jnp.dot(member, tri_ref[...], preferred_element_type=F32)
    cnt = jnp.dot(member, jnp.ones((tm, LANES), BF16), preferred_element_type=F32)
    cnt_ref[0] = cnt
    ids_ref[0:1, :] = eid1
    ids_ref[1:2, :] = eid2
    ids_ref[2:3, :] = jnp.sum(jnp.where(oh1, before, 0.0), axis=0, keepdims=True).astype(jnp.int32)
    ids_ref[3:4, :] = jnp.sum(jnp.where(oh2, before, 0.0), axis=0, keepdims=True).astype(jnp.int32)
    cnt_pad = jnp.floor((cnt + (SORT_PAD - 1)) * (1.0 / SORT_PAD)) * SORT_PAD
    er = lax.broadcasted_iota(jnp.int32, (N_EXPERTS, N_EXPERTS), 0)
    ec = lax.broadcasted_iota(jnp.int32, (N_EXPERTS, N_EXPERTS), 1)
    lower = jnp.where(ec < er, 1.0, 0.0).astype(BF16)
    first = jnp.dot(lower, cnt_pad.astype(BF16), preferred_element_type=F32)
    place = before + jnp.concatenate([first] * (tm // LANES), axis=1)
    pos1 = jnp.sum(jnp.where(oh1, place, 0.0), axis=0, keepdims=True).astype(jnp.int32)
    pos2 = jnp.sum(jnp.where(oh2, place, 0.0), axis=0, keepdims=True).astype(jnp.int32)
    return pos1, pos2, x_hi


def _sort_rows(pos1, pos2, x2b, xs_ref):
    tm = x2b.shape[0]
    for r0 in range(0, xs_ref.shape[0], SORT_CHUNK):
        r = lax.broadcasted_iota(jnp.int32, (SORT_CHUNK, tm), 0) + r0
        perm = jnp.where(r == pos1, 1.0, jnp.where(r == pos2, 1.0, 0.0)).astype(BF16)
        xs_ref[r0:r0 + SORT_CHUNK, :] = jnp.dot(perm, x2b, preferred_element_type=F32).astype(BF16)


def _norm2(x, mv, ng_ref):
    return _rms(x, ng_ref[1:2]) * (1.0 + mv[4:5]) + mv[3:4]


def _norm2_and_route(xg, mv, ng_ref, route_refs, out_refs):
    xo_ref, xs_ref, ids_ref, gts_ref, cnt_ref = out_refs
    x2g = [_norm2(x, mv, ng_ref) for x in xg]
    xo_ref[...] = jnp.concatenate(xg, axis=0) if len(xg) > 1 else xg[0]
    x2 = jnp.concatenate(x2g, axis=0) if len(x2g) > 1 else x2g[0]
    _sort_rows(*_route(x2, *route_refs, ids_ref, gts_ref, cnt_ref), xs_ref)


def _full(shape):
    return pl.BlockSpec(shape, lambda i: (0,) * len(shape))


def _tile_specs(tm, d, has_comb, tiles_per_batch):
    tok = pl.BlockSpec((tm, d), lambda i: (i, 0))
    specs = [tok]
    if has_comb:
        specs += [tok, tok, pl.BlockSpec((tm, 2), lambda i: (i, 0))]
    specs.append(pl.BlockSpec((None, 8, d), lambda i: (i // tiles_per_batch, 0, 0)))
    specs.append(_full((2, d)))
    return specs


def _route_specs(d, tm):
    return [_full((ROUTE_ROWS, d)), _full((ROUTE_ROWS, d)), _full((ROUTE_ROWS, 1)), _full((tm, tm))]


def _sorted_rows(tm):
    return 2 * tm + N_EXPERTS * SORT_PAD


def _mixer_out(t, d, tm, h2_rows, h2_row0):
    sr = _sorted_rows(tm)
    assert h2_row0 % sr == 0
    off = h2_row0 // sr
    tok = pl.BlockSpec((tm, d), lambda i: (i, 0))
    specs = [tok, pl.BlockSpec((sr, d), lambda i: (i + off, 0)),
             pl.BlockSpec((4, tm), lambda i: (0, i)), pl.BlockSpec((2, tm), lambda i: (0, i)),
             pl.BlockSpec((1, N_EXPERTS, LANES), lambda i: (i, 0, 0))]
    shapes = [jax.ShapeDtypeStruct((t, d), F32), jax.ShapeDtypeStruct((h2_rows, d), BF16),
              jax.ShapeDtypeStruct((4, t), jnp.int32), jax.ShapeDtypeStruct((2, t), F32),
              jax.ShapeDtypeStruct((t // tm, N_EXPERTS, LANES), F32)]
    return specs, shapes


def _in_a_kernel(*refs, has_comb):
    n_in = 4 if has_comb else 1
    x_ref, comb, refs = refs[0], refs[1:n_in], refs[n_in:]
    mv_ref, ng_ref, wqt_ref, wk_ref, wvt_ref, wot_ref, wgt_ref, bg_ref = refs[:8]
    outs = refs[8:]
    if has_comb:
        xo_ref, outs = outs[0], outs[1:]
    qt_ref, k_ref, vt_ref, ot_ref, a_ref = outs
    mv = mv_ref[...]
    x = _residual_in(x_ref, comb, mv)
    if has_comb:
        xo_ref[...] = x
    tm = x.shape[0]
    gm = min(tm, 256)
    for r in range(tm // gm):
        tok = slice(r * gm, (r + 1) * gm)
        h = (_rms(x[tok], ng_ref[0:1]) * (1.0 + mv[1:2]) + mv[0:1]).astype(BF16)
        qt_ref[0, :, tok] = (_nt_dot(wqt_ref[...], h) * (DQK ** -0.5)).astype(BF16)
        k_ref[tok, :] = jnp.dot(h, wk_ref[...], preferred_element_type=F32).astype(BF16)
        vt_ref[0, :, tok] = _nt_dot(wvt_ref[...], h).astype(BF16)
        ot_ref[0, :, tok] = jax.nn.sigmoid(_nt_dot(wot_ref[...], h)).astype(BF16)
        g = _nt_dot(wgt_ref[...], h) + bg_ref[...]
        gi, gf = g[0:8], g[8:16]
        a_ref[0, 0:8, tok] = IGATE_SOFTCAP * jnp.tanh(gi / IGATE_SOFTCAP)
        a_ref[0, 8:16, tok] = jnp.minimum(gf, 0.0) - jnp.log(1.0 + jnp.exp(-jnp.abs(gf)))


def _in_a(x, comb, modv, ng, w, batch, tm):
    t, d = x.shape
    s = t // batch
    tpb = s // tm
    has_comb = bool(comb)
    in_specs = _tile_specs(tm, d, has_comb, tpb) + [_full(a.shape) for a in w]
    tok = lambda n: pl.BlockSpec((tm, n), lambda i: (i, 0))
    rows = lambda n: pl.BlockSpec((1, n, tm), lambda i: (i // tpb, 0, i % tpb))
    out_specs = [rows(NQ), tok(NQ), rows(NV), rows(d), rows(16)]
    out_shape = [jax.ShapeDtypeStruct((batch, NQ, s), BF16), jax.ShapeDtypeStruct((t, NQ), BF16),
                 jax.ShapeDtypeStruct((batch, NV, s), BF16), jax.ShapeDtypeStruct((batch, d, s), BF16),
                 jax.ShapeDtypeStruct((batch, 16, s), F32)]
    if has_comb:
        out_specs = [tok(d)] + out_specs
        out_shape = [jax.ShapeDtypeStruct((t, d), F32)] + out_shape
    outs = pl.pallas_call(
        functools.partial(_in_a_kernel, has_comb=has_comb),
        grid=(t // tm,), in_specs=in_specs, out_specs=out_specs, out_shape=out_shape,
        compiler_params=_cparams("parallel"),
    )(x, *comb, modv, ng, *w)
    if has_comb:
        return outs[0], outs[1:]
    return x, outs


def _chunk_scan(x, op, fill, forward):
    length = x.shape[-1]
    pos = lax.broadcasted_iota(jnp.int32, x.shape, 1)
    s = 1
    while s < length:
        if forward:
            x = op(x, jnp.where(pos >= s, pltpu.roll(x, s, axis=1), fill))
        else:
            x = op(x, jnp.where(pos < length - s, pltpu.roll(x, length - s, axis=1), fill))
        s *= 2
    return x


def _gate_kernel(a_ref, o_ref):
    for r in range(2 * HEADS):
        fwd = r < HEADS
        b = _chunk_scan(a_ref[0, 8 + r], jnp.add, 0.0, fwd)
        u = a_ref[0, r] - b
        cm = _chunk_scan(u, jnp.maximum, -jnp.inf, fwd)
        end = b.shape[-1] - 1 if fwd else 0
        o_ref[0, r] = u
        o_ref[0, 8 + r] = b
        o_ref[0, 16 + r] = cm
        o_ref[0, 24 + r] = jnp.broadcast_to(cm[:, end:end + 1], cm.shape)
        o_ref[0, 32 + r] = jnp.broadcast_to(b[:, end:end + 1], b.shape)


GATE_ROWS = 40


def _gate_scans(act, chunk):
    batch, _, s = act.shape
    nc = s // chunk
    g = pl.pallas_call(
        _gate_kernel, grid=(batch,),
        in_specs=[pl.BlockSpec((1, 16, nc, chunk), lambda b: (b, 0, 0, 0))],
        out_specs=pl.BlockSpec((1, GATE_ROWS, nc, chunk), lambda b: (b, 0, 0, 0)),
        out_shape=jax.ShapeDtypeStruct((batch, GATE_ROWS, nc, chunk), F32),
        compiler_params=_cparams("parallel"),
    )(act.reshape(batch, 16, nc, chunk))
    rows = jnp.transpose(g, (0, 2, 1, 3))
    cols = jnp.transpose(g[:, 0:8], (0, 2, 3, 1))
    return rows, cols


def _mlstm_kernel(qf_ref, kf_ref, vf_ref, rf_ref, cf_ref, qb_ref, kb_ref, vb_ref, rb_ref, cb_ref,
                  c0_ref, n0_ref, m0_ref, hf_ref, hb_ref, ct_ref, nt_ref, mt_ref,
                  c_s, n_s, m_s, *, chunk):
    step = pl.program_id(1)
    last = pl.num_programs(1) - 1

    @pl.when(step == 0)
    def _():
        c_s[...] = c0_ref[0]
        n_s[...] = n0_ref[0]
        m_s[...] = m0_ref[0]

    si = lax.broadcasted_iota(jnp.int32, (chunk, chunk), 0)
    ji = lax.broadcasted_iota(jnp.int32, (chunk, chunk), 1)
    sub = qf_ref.shape[2] // chunk
    dirs = ((qf_ref, kf_ref, vf_ref, rf_ref, cf_ref, hf_ref),
            (qb_ref, kb_ref, vb_ref, rb_ref, cb_ref, hb_ref))
    units = [(j if d == 0 else sub - 1 - j, d, h, dirs[d])
             for j in range(sub) for d in range(2) for h in range(HEADS)]
    per_sub = 2 * HEADS
    for u0, (c, d, h, (q_ref, k_ref, v_ref, r_ref, c_ref, h_ref)) in enumerate(units):
        if u0 % per_sub == 0:
            scores = []
            for c2, d2, h2, (q2_ref, k2_ref, _, _, _, _) in units[u0:u0 + per_sub]:
                qt2 = q2_ref[0, h2 * DQK:(h2 + 1) * DQK, c2 * chunk:(c2 + 1) * chunk]
                k2 = k2_ref[c2 * chunk:(c2 + 1) * chunk, h2 * DQK:(h2 + 1) * DQK]
                n16 = jnp.broadcast_to(n_s[d2 * HEADS + h2], (16, DQK)).astype(BF16)
                scores.append(jnp.dot(jnp.concatenate([k2, n16], axis=0), qt2,
                                      preferred_element_type=F32))
        sx = scores[u0 % per_sub]
        mask = (si <= ji) if d == 0 else (si >= ji)
        idx = d * HEADS + h
        lanes = slice(c * chunk, (c + 1) * chunk)
        qt = q_ref[0, h * DQK:(h + 1) * DQK, lanes]
        k = k_ref[c * chunk:(c + 1) * chunk, h * DQK:(h + 1) * DQK]
        vt = v_ref[0, h * DV:(h + 1) * DV, lanes]
        u_col = c_ref[0, c, :, idx:idx + 1]
        u_row = r_ref[0, c, idx:idx + 1, :]
        b_row = r_ref[0, c, 8 + idx:9 + idx, :]
        cm_row = r_ref[0, c, 16 + idx:17 + idx, :]
        cm_end = r_ref[0, c, 24 + idx:25 + idx, :]
        b_end = r_ref[0, c, 32 + idx:33 + idx, :]
        ct_st = c_s[idx]
        n_st = n_s[idx]
        m_st = m_s[idx]
        mm = jnp.maximum(m_st, cm_row)
        dt = jnp.where(mask, jnp.exp(u_col - mm), 0.0)
        pt = sx[:chunk] * dt
        a_int = jnp.exp(m_st - mm)
        den = jnp.sum(pt, axis=0, keepdims=True) + a_int * sx[chunk:chunk + 1]
        inv = 1.0 / jnp.maximum(jnp.abs(den), jnp.exp(-b_row - mm))
        wts = jnp.concatenate([(pt * inv).astype(BF16),
                               (qt.astype(F32) * (a_int * inv)).astype(BF16)], axis=0)
        vals = jnp.concatenate([vt, ct_st.astype(BF16)], axis=1)
        h_ref[0, h * DV:(h + 1) * DV, lanes] = jnp.dot(vals, wts, preferred_element_type=F32)
        mm_end = jnp.maximum(m_st, cm_end)
        a_end = jnp.exp(u_row - mm_end)
        decay = jnp.exp(m_st - mm_end)
        upd = jnp.concatenate([(vt.astype(F32) * a_end).astype(BF16),
                               jnp.broadcast_to(a_end, (16, chunk)).astype(BF16)], axis=0)
        upd = jnp.dot(upd, k, preferred_element_type=F32)
        c_s[idx] = decay * ct_st + upd[:DV]
        n_s[idx] = decay * n_st + upd[DV:DV + 1]
        m_s[idx] = b_end + mm_end

    @pl.when(step == last)
    def _():
        ct_ref[0] = c_s[...]
        nt_ref[0] = n_s[...]
        mt_ref[0] = m_s[...]


def _mlstm_zero_state(batch, chunk):
    return (jnp.zeros((batch, 8, DV, DQK), F32), jnp.zeros((batch, 8, 1, DQK), F32),
            jnp.zeros((batch, 8, 1, chunk), F32))


def _mlstm(qt, k, vt, rows, cols, state, batch, chunk):
    t = k.shape[0]
    s = t // batch
    nc = s // chunk
    assert chunk == DQK
    c0, n0, m0 = state
    sub = MLSTM_SUB if nc % MLSTM_SUB == 0 else 1
    span = sub * chunk
    ns = nc // sub

    def specs(cidx):
        return [pl.BlockSpec((1, NQ, span), lambda b, c: (b, 0, cidx(c))),
                pl.BlockSpec((span, NQ), lambda b, c: (b * ns + cidx(c), 0)),
                pl.BlockSpec((1, NV, span), lambda b, c: (b, 0, cidx(c))),
                pl.BlockSpec((1, sub, GATE_ROWS, chunk), lambda b, c: (b, cidx(c), 0, 0)),
                pl.BlockSpec((1, sub, chunk, 8), lambda b, c: (b, cidx(c), 0, 0))]

    fwd = lambda c: c
    bwd = lambda c: ns - 1 - c
    st_specs = [pl.BlockSpec((1, 8, DV, DQK), lambda b, c: (b, 0, 0, 0)),
                pl.BlockSpec((1, 8, 1, DQK), lambda b, c: (b, 0, 0, 0)),
                pl.BlockSpec((1, 8, 1, chunk), lambda b, c: (b, 0, 0, 0))]
    st_shape = [jax.ShapeDtypeStruct(a.shape, F32) for a in state]
    h_shape = jax.ShapeDtypeStruct((batch, NV, s), F32)
    args = (qt, k, vt, rows, cols)
    hf, hb, ct, nt, mt = pl.pallas_call(
        functools.partial(_mlstm_kernel, chunk=chunk),
        grid=(batch, ns),
        in_specs=specs(fwd) + specs(bwd) + st_specs,
        out_specs=[pl.BlockSpec((1, NV, span), lambda b, c: (b, 0, fwd(c))),
                   pl.BlockSpec((1, NV, span), lambda b, c: (b, 0, bwd(c)))] + st_specs,
        out_shape=[h_shape, h_shape] + st_shape,
        scratch_shapes=[pltpu.VMEM((8, DV, DQK), F32), pltpu.VMEM((8, 1, DQK), F32),
                        pltpu.VMEM((8, 1, chunk), F32)],
        compiler_params=_cparams("arbitrary", "arbitrary"),
    )(*args, *args, c0, n0, m0)
    return hf, hb, (ct, nt, mt)


def _out_a_kernel(*refs, aliased):
    hf_ref, hb_ref, o_ref, x_ref, mv_ref, ng_ref, hg_ref, wout_ref = refs[:8]
    route_refs = refs[8:12]
    out_refs = refs[12 + int(aliased):]
    mv = mv_ref[...]
    hsum = hf_ref[0] + hb_ref[0]
    parts = []
    for h in range(HEADS):
        hh = hsum[h * DV:(h + 1) * DV]
        parts.append(hh * lax.rsqrt(jnp.mean(hh * hh, axis=0, keepdims=True) + EPS))
    hn = jnp.concatenate(parts, axis=0) * hg_ref[...] * o_ref[0].astype(F32)
    y = jnp.dot(wout_ref[...], hn.astype(BF16), preferred_element_type=F32).T
    x = x_ref[...] + mv[2:3] * y
    _norm2_and_route([x], mv, ng_ref, route_refs, out_refs)


def _out_a(hf, hb, o, x, modv, ng, head_g, w_out_t, route_w, batch, tm, h2_buf, h2_rows, h2_row0):
    t, d = x.shape
    tpb = t // batch // tm
    rows = lambda n: pl.BlockSpec((1, n, tm), lambda i: (i // tpb, 0, i % tpb))
    in_specs = ([rows(NV), rows(NV), rows(d)] + _tile_specs(tm, d, False, tpb)
                + [_full((NV, 1)), _full(w_out_t.shape)] + _route_specs(d, tm))
    w_out = w_out_t
    args = [hf, hb, o, x, modv, ng, head_g, w_out, *route_w]
    return _mixer_call(functools.partial(_out_a_kernel, aliased=h2_buf is not None),
                       in_specs, args, t, d, tm, h2_buf, h2_rows, h2_row0)


def _mixer_call(body, in_specs, args, t, d, tm, h2_buf, h2_rows, h2_row0):
    out_specs, out_shape = _mixer_out(t, d, tm, h2_rows, h2_row0)
    aliases = {}
    if h2_buf is not None:
        aliases = {len(args): 1}
        in_specs = in_specs + [pl.BlockSpec(memory_space=pl.ANY)]
        args = args + [h2_buf]
    return pl.pallas_call(
        body, grid=(t // tm,), in_specs=in_specs, out_specs=out_specs, out_shape=out_shape,
        input_output_aliases=aliases, compiler_params=_cparams("parallel"),
    )(*args)


def _conv_kernel(*refs, has_comb, row_w, aliased):
    n_in = 4 if has_comb else 1
    x_ref, comb, refs = refs[0], refs[1:n_in], refs[n_in:]
    mv_ref, ng_ref, win_ref, cw_ref, wout_ref = refs[:5]
    route_refs = refs[5:9]
    out_refs = refs[9 + int(aliased):]
    mv = mv_ref[...]
    x = _residual_in(x_ref, comb, mv)
    d = x.shape[1]
    tm = x.shape[0]
    cw = cw_ref[...]
    groups = 4 if (tm // 4) % row_w == 0 else 1
    gm = tm // groups
    xg = [x[g * gm:(g + 1) * gm] for g in range(groups)]
    hg = [(_rms(a, ng_ref[0:1]) * (1.0 + mv[1:2]) + mv[0:1]).astype(BF16) for a in xg]
    pos = lax.broadcasted_iota(jnp.int32, (gm, d), 0) % row_w

    def in_proj(h):
        return (jnp.dot(h, win_ref[:, :d], preferred_element_type=F32),
                jnp.dot(h, win_ref[:, d:2 * d], preferred_element_type=F32),
                jnp.dot(h, win_ref[:, 2 * d:], preferred_element_type=F32))

    def gate(p):
        bg, cg, u = p
        z = cg * u
        left = jnp.where(pos == 0, 0.0, pltpu.roll(z, 1, axis=0))
        right = jnp.where(pos == row_w - 1, 0.0, pltpu.roll(z, gm - 1, axis=0))
        return (bg * (cw[0:1] * left + cw[1:2] * z + cw[2:3] * right)).astype(BF16)

    pg = [in_proj(h) for h in hg]
    outs = []
    for g in range(groups):
        yl = jnp.dot(gate(pg[g]), wout_ref[...], preferred_element_type=F32)
        outs.append(xg[g] + mv[2:3] * yl)
    _norm2_and_route(outs, mv, ng_ref, route_refs, out_refs)


def _conv_layer(x, comb, modv, ng, w_in, conv_w, w_out, route_w, batch, tm, row_w, h2_buf, h2_rows, h2_row0):
    t, d = x.shape
    tpb = t // batch // tm
    in_specs = (_tile_specs(tm, d, bool(comb), tpb)
                + [_full(w_in.shape), _full(conv_w.shape), _full(w_out.shape)] + _route_specs(d, tm))
    args = [x, *comb, modv, ng, w_in, conv_w, w_out, *route_w]
    body = functools.partial(_conv_kernel, has_comb=bool(comb), row_w=row_w, aliased=h2_buf is not None)
    return _mixer_call(body, in_specs, args, t, d, tm, h2_buf, h2_rows, h2_row0)


def _moe_kernel(be_ref, nb_ref, src_ref, nxt_ref, xs_ref, wg_ref, wu_ref, wd_ref, y_ref,
                xbuf, sem, wst_g, wst_u, wst_d, wsem, wslot, wg_s, wu_s, wd_s):
    i = pl.program_id(0)
    nb = nb_ref[0]
    n_slots = GATHER_AHEAD + 1

    def weight_copies(e, slot):
        return [pltpu.make_async_copy(w_ref.at[e], st.at[slot], wsem.at[slot, j])
                for j, (w_ref, st) in enumerate(((wg_ref, wst_g), (wu_ref, wst_u), (wd_ref, wst_d)))]

    def piece_copies(blk, slot):
        copies = []
        for j in range(PIECES):
            src = pl.multiple_of(src_ref[blk * PIECES + j], SORT_PAD)
            copies.append(pltpu.make_async_copy(
                xs_ref.at[pl.ds(src, SORT_PAD), :],
                xbuf.at[slot, pl.ds(j * SORT_PAD, SORT_PAD), :], sem.at[slot]))
        return copies

    def wait_slot(slot):
        pltpu.make_async_copy(xs_ref.at[pl.ds(0, MOE_BLOCK), :], xbuf.at[slot], sem.at[slot]).wait()

    @pl.when(i < nb)
    def _():
        slot = lax.rem(i, n_slots)

        for b in range(GATHER_AHEAD):
            @pl.when((i == 0) & (b < nb))
            def _(b=b):
                for c in piece_copies(b, b):
                    c.start()

        @pl.when(i + GATHER_AHEAD < nb)
        def _():
            for c in piece_copies(i + GATHER_AHEAD, lax.rem(i + GATHER_AHEAD, n_slots)):
                c.start()

        @pl.when(i == 0)
        def _():
            wslot[0] = 1
            for c in weight_copies(be_ref[0], 0):
                c.start()

        prev = be_ref[jnp.maximum(i - 1, 0)]

        @pl.when((i == 0) | (be_ref[i] != prev))
        def _():
            ws = 1 - wslot[0]
            wslot[0] = ws
            for c in weight_copies(be_ref[i], ws):
                c.wait()
            wg_s[...] = wst_g[ws].astype(BF16)
            wu_s[...] = wst_u[ws].astype(BF16)
            wd_s[...] = wst_d[ws].astype(BF16)

            @pl.when(nxt_ref[i] >= 0)
            def _():
                for c in weight_copies(nxt_ref[i], 1 - ws):
                    c.start()

        wait_slot(slot)
        gm = MOE_BLOCK // MOE_GROUPS
        gu = []
        for r in range(MOE_GROUPS):
            x = xbuf[slot, r * gm:(r + 1) * gm, :]
            gu.append((jnp.dot(x, wg_s[...], preferred_element_type=F32),
                       jnp.dot(x, wu_s[...], preferred_element_type=F32)))
        for r, (g, u) in enumerate(gu):
            a = (g * jax.nn.sigmoid(g) * u).astype(BF16)
            y_ref[r * gm:(r + 1) * gm, :] = jnp.dot(a, wd_s[...], preferred_element_type=F32).astype(BF16)


def _moe_experts(xs, blk_e, nb_used, piece_src, next_e, w_gate, w_up, w_down):
    d = xs.shape[1]
    de = w_gate.shape[-1]
    n_blocks = blk_e.shape[0]
    hbm = pl.BlockSpec(memory_space=pl.ANY)
    grid_spec = pltpu.PrefetchScalarGridSpec(
        num_scalar_prefetch=4, grid=(n_blocks,),
        in_specs=[hbm, hbm, hbm, hbm],
        out_specs=pl.BlockSpec((MOE_BLOCK, d), lambda i, be, nb, src, nxt: (jnp.minimum(i, nb[0] - 1), 0)),
        scratch_shapes=[pltpu.VMEM((GATHER_AHEAD + 1, MOE_BLOCK, d), BF16),
                        pltpu.SemaphoreType.DMA((GATHER_AHEAD + 1,)),
                        pltpu.VMEM((2, d, de), F32), pltpu.VMEM((2, d, de), F32),
                        pltpu.VMEM((2, de, d), F32), pltpu.SemaphoreType.DMA((2, 3)),
                        pltpu.SMEM((1,), jnp.int32),
                        pltpu.VMEM((d, de), BF16), pltpu.VMEM((d, de), BF16),
                        pltpu.VMEM((de, d), BF16)])
    return pl.pallas_call(
        _moe_kernel, grid_spec=grid_spec,
        out_shape=jax.ShapeDtypeStruct((n_blocks * MOE_BLOCK, d), BF16),
        compiler_params=_cparams("arbitrary"),
    )(blk_e, nb_used, piece_src, next_e, xs, w_gate, w_up, w_down)


def _dest_slots(ids, base, tm):
    t = ids.shape[1]
    nt = t // tm
    eid = ids[0:2].reshape(2, nt, tm)
    rank = ids[2:4].reshape(2, nt, tm)
    onehot = eid[..., None] == jnp.arange(N_EXPERTS, dtype=jnp.int32)
    off = jnp.sum(jnp.where(onehot, base[None, :, None, :], 0), axis=-1)
    return (rank + off).reshape(2, t)


def _moe(xs, routed, layer, w_gate, w_up, w_down):
    cnt_tiles = jnp.concatenate([cnt[:, :, 0] for _, cnt, _ in routed], axis=0).astype(jnp.int32)
    cnt_pad = (cnt_tiles + SORT_PAD - 1) // SORT_PAD * SORT_PAD
    run_first = jnp.cumsum(cnt_pad, axis=0) - cnt_pad
    in_tile = jnp.cumsum(cnt_pad, axis=1) - cnt_pad
    region = jnp.sum(cnt_pad, axis=0)
    padded = (region + MOE_BLOCK - 1) // MOE_BLOCK * MOE_BLOCK
    pends = jnp.cumsum(padded)
    pstart = pends - padded
    base = pstart[None, :] + run_first
    dests, tile_row0, row, xrow, worst = [], [], 0, 0, 0
    for ids, cnt, tm in routed:
        nt = cnt.shape[0]
        dests.append(_dest_slots(ids, base[row:row + nt], tm))
        tile_row0.append(xrow + jnp.arange(nt, dtype=jnp.int32) * _sorted_rows(tm))
        row += nt
        xrow += nt * _sorted_rows(tm)
        worst += nt * (2 * tm + N_EXPERTS * (SORT_PAD - 1))
    tile_row0 = jnp.concatenate(tile_row0)
    n_blocks = -(-(worst + N_EXPERTS * (MOE_BLOCK - 1)) // MOE_BLOCK)
    blk_start = jnp.arange(n_blocks, dtype=jnp.int32) * MOE_BLOCK
    blk_x = jnp.minimum(jnp.sum((pends[None, :] <= blk_start[:, None]).astype(jnp.int32), axis=1),
                        N_EXPERTS - 1)
    nb_used = (pends[-1:] // MOE_BLOCK).astype(jnp.int32)
    blk_oh = blk_x[:, None] == jnp.arange(N_EXPERTS, dtype=jnp.int32)[None, :]
    pick = lambda tab: jnp.sum(jnp.where(blk_oh[:, None, :], tab[None], 0), axis=2)
    per_piece = lambda a: jnp.repeat(a, PIECES, axis=0)
    blk_first = per_piece(pick(run_first))
    blk_src0 = per_piece(pick(in_tile - run_first) + tile_row0[None, :])
    blk_scal = per_piece(pick(jnp.stack([pstart, region])))
    piece_rank = jnp.arange(n_blocks * PIECES, dtype=jnp.int32) * SORT_PAD - blk_scal[:, 0]
    piece_tile = jnp.sum((blk_first <= piece_rank[:, None]).astype(jnp.int32), axis=1) - 1
    tile_oh = piece_tile[:, None] == jnp.arange(blk_first.shape[1], dtype=jnp.int32)[None, :]
    src = jnp.sum(jnp.where(tile_oh, blk_src0, 0), axis=1) + piece_rank
    piece_src = jnp.where(piece_rank < blk_scal[:, 1], src, 0).astype(jnp.int32)
    e_iota = jnp.arange(N_EXPERTS, dtype=jnp.int32)
    later = (e_iota[None, :] > e_iota[:, None]) & (padded[None, :] > 0)
    next_exp = jnp.min(jnp.where(later, e_iota[None, :], N_EXPERTS), axis=1)
    next_exp = jnp.where(next_exp < N_EXPERTS, next_exp + layer * N_EXPERTS, -1)
    next_e = jnp.sum(jnp.where(blk_oh, next_exp[None, :], 0), axis=1).astype(jnp.int32)
    yb = _moe_experts(xs, blk_x + layer * N_EXPERTS, nb_used, piece_src, next_e, w_gate, w_up, w_down)
    return [(yb.at[d[0]].get(mode='promise_in_bounds'), yb.at[d[1]].get(mode='promise_in_bounds'))
            for d in dests]


def _final_kernel(x_ref, ya_ref, yb_ref, gt_ref, mv_ref, g_ref, o_ref):
    x = _residual_in(x_ref, (ya_ref, yb_ref, gt_ref), mv_ref[...])
    o_ref[...] = _rms(x, g_ref[...])


def _final(x, comb, modv, g, batch, tm):
    t, d = x.shape
    tpb = t // batch // tm
    tok = pl.BlockSpec((tm, d), lambda i: (i, 0))
    return pl.pallas_call(
        _final_kernel, grid=(t // tm,),
        in_specs=[tok, tok, tok, pl.BlockSpec((tm, 2), lambda i: (i, 0)),
                  pl.BlockSpec((None, 8, d), lambda i: (i // tpb, 0, 0)), _full((1, d))],
        out_specs=tok, out_shape=jax.ShapeDtypeStruct((t, d), F32),
        compiler_params=_cparams("parallel"),
    )(x, *comb, modv, g)


def _prep_a(w_in, b_gate):
    d = w_in.shape[0]
    wqt = w_in[:, :NQ].T.astype(BF16)
    wk = w_in[:, NQ:2 * NQ].astype(BF16)
    wvt = w_in[:, 2 * NQ:2 * NQ + NV].T.astype(BF16)
    wot = w_in[:, 2 * NQ + NV:2 * NQ + NV + d].T.astype(BF16)
    perm = jnp.array([0, 1, 2, 3, 8, 9, 10, 11, 4, 5, 6, 7, 12, 13, 14, 15], jnp.int32)
    wgt = w_in[:, 2 * NQ + NV + d:].T[perm].astype(BF16)
    bg = b_gate.astype(F32)[perm][:, None]
    return wqt, wk, wvt, wot, wgt, bg


def _prep_route(w_group, b_group, w_router, b_router, tm):
    d = w_group.shape[0]
    pad = ROUTE_ROWS - N_EXPERTS - N_GROUPS
    wt = jnp.concatenate([w_router.T, w_group.T, jnp.zeros((pad, d), F32)], axis=0).astype(F32)
    hi = wt.astype(BF16)
    lo = (wt - hi.astype(F32)).astype(BF16)
    rb = jnp.concatenate([b_router, b_group, jnp.zeros((pad,), F32)]).astype(F32)[:, None]
    tri = jnp.triu(jnp.ones((tm, tm), BF16), k=1)
    return hi, lo, rb, tri


def _modv(mod, l, rows):
    depth = mod.shape[0]
    d = mod.shape[-1] // 6
    zero = jnp.zeros((len(rows), 1, d), F32)
    cur = jnp.stack([mod[l, r].reshape(6, d) for r in rows]) if l < depth else jnp.zeros((len(rows), 6, d), F32)
    prev = jnp.stack([mod[l - 1, r].reshape(6, d)[5:6] for r in rows]) if l > 0 else zero
    return jnp.concatenate([cur, prev, zero], axis=1)


def kernel(x, c, ctx, c_ctx, mod_w, mod_b, norm_g, final_g, a_w_in, a_b_gate, a_head_g, a_w_out,
           b_w_in, b_conv_w, b_w_out, moe_w_group, moe_b_group, moe_w_router, moe_b_router,
           moe_w_gate, moe_w_up, moe_w_down):
    batch, seq, d = x.shape
    n_ctx = ctx.shape[1]
    depth = mod_w.shape[0]
    assert batch + 1 <= 8 and seq % MLSTM_CHUNK == 0 and n_ctx % MLSTM_CHUNK == 0
    tm = min(512, seq)
    t_lat = batch * seq
    t_ctx = batch * n_ctx
    assert t_lat % n_ctx == 0 and seq % GRID_W == 0

    cond = jnp.concatenate([c, c_ctx[None, :], jnp.zeros((8 - batch - 1, d), F32)], axis=0)
    mod = _modulation(cond, mod_w, mod_b)
    de = moe_w_gate.shape[-1]
    w_gate = moe_w_gate.reshape(depth * N_EXPERTS, d, de)
    w_up = moe_w_up.reshape(depth * N_EXPERTS, d, de)
    w_down = moe_w_down.reshape(depth * N_EXPERTS, de, d)

    lat = x.reshape(t_lat, d)
    cx = ctx.reshape(t_ctx, d)
    comb_lat, comb_ctx = (), ()
    for l in range(depth):
        kind, j = l % 2, l // 2
        ctx_after = any(i % 2 == 0 for i in range(l + 1, depth))
        mv_lat = _modv(mod, l, list(range(batch)))
        mv_ctx = _modv(mod, l, [batch] * batch)
        route_args = (moe_w_group[l], moe_b_group[l], moe_w_router[l], moe_b_router[l])
        route_lat = _prep_route(*route_args, tm)
        route_ctx = _prep_route(*route_args, n_ctx)
        lat_rows = t_lat // tm * _sorted_rows(tm)
        h2_rows = lat_rows + (t_ctx // n_ctx * _sorted_rows(n_ctx) if ctx_after else 0)
        if kind == 0:
            wa = _prep_a(a_w_in[j], a_b_gate[j])
            state = _mlstm_zero_state(batch, MLSTM_CHUNK)
            cx, (qtc, kc, vtc, oc, actc) = _in_a(cx, comb_ctx, mv_ctx, norm_g[l], wa, batch, n_ctx)
            hfc, hbc, state = _mlstm(qtc, kc, vtc, *_gate_scans(actc, MLSTM_CHUNK), state, batch, MLSTM_CHUNK)
            lat, (qtl, kl, vtl, ol, actl) = _in_a(lat, comb_lat, mv_lat, norm_g[l], wa, batch, tm)
            hfl, hbl, _ = _mlstm(qtl, kl, vtl, *_gate_scans(actl, MLSTM_CHUNK), state, batch, MLSTM_CHUNK)
            w_out = a_w_out[j].T.astype(BF16)
            head_g = a_head_g[j].astype(F32)[:, None]
            lat, h2, idl, gtl, cntl = _out_a(hfl, hbl, ol, lat, mv_lat, norm_g[l], head_g, w_out, route_lat,
                                             batch, tm, None, h2_rows, 0)
            if ctx_after:
                cx, h2, idc, gtc, cntc = _out_a(hfc, hbc, oc, cx, mv_ctx, norm_g[l], head_g, w_out, route_ctx,
                                                batch, n_ctx, h2, h2_rows, lat_rows)
        else:
            w_in = b_w_in[j].astype(BF16)
            w_out = b_w_out[j].astype(BF16)
            conv_w = b_conv_w[j].astype(F32)
            lat, h2, idl, gtl, cntl = _conv_layer(lat, comb_lat, mv_lat, norm_g[l], w_in, conv_w, w_out,
                                                  route_lat, batch, tm, GRID_W, None, h2_rows, 0)
            if ctx_after:
                cx, h2, idc, gtc, cntc = _conv_layer(cx, comb_ctx, mv_ctx, norm_g[l], w_in, conv_w, w_out,
                                                     route_ctx, batch, n_ctx, n_ctx, h2, h2_rows, lat_rows)
        routed = [(idl, cntl, tm)] + ([(idc, cntc, n_ctx)] if ctx_after else [])
        outs = _moe(h2, routed, l, w_gate, w_up, w_down)
        comb_lat = (*outs[0], gtl.T)
        comb_ctx = (*outs[1], gtc.T) if ctx_after else ()
    out = _final(lat, comb_lat, _modv(mod, depth, list(range(batch))), final_g.astype(F32)[None, :], batch, tm)
    return out.reshape(batch, seq, d)
```

```python
import functools

import jax
import jax.numpy as jnp
from jax import lax
from jax.experimental import pallas as pl
from jax.experimental.pallas import tpu as pltpu

F32 = jnp.float32
BF16 = jnp.bfloat16

EPS = 1e-6
HEADS = 4
DQK = 128
DV = 256
NQ = HEADS * DQK
NV = HEADS * DV
IGATE_SOFTCAP = 15.0
GRID_W = 64
N_GROUPS = 4
EPG = 8
N_EXPERTS = N_GROUPS * EPG
ROUTE_ROWS = 40
MOE_BLOCK = 512
GATHER_AHEAD = 2
MOE_GROUPS = 4
SORT_PAD = 16
SORT_CHUNK = 256
PIECES = MOE_BLOCK // SORT_PAD
MLSTM_CHUNK = 128
MLSTM_SUB = 8
LANES = 128
VMEM_LIMIT = 52 * 1024 * 1024


def _cparams(*sem):
    return pltpu.CompilerParams(dimension_semantics=sem, vmem_limit_bytes=VMEM_LIMIT)


def _nt_dot(a, b):
    return lax.dot_general(a, b, (((1,), (1,)), ((), ())), preferred_element_type=F32)


def _rms(x, g):
    return x * lax.rsqrt(jnp.mean(x * x, axis=-1, keepdims=True) + EPS) * g


def _mod_kernel(c_ref, w_ref, b_ref, o_ref):
    c = c_ref[...]
    s = (c * jax.nn.sigmoid(c)).astype(BF16)
    o_ref[0] = jnp.dot(s, w_ref[0].astype(BF16), preferred_element_type=F32) + b_ref[0]


def _modulation(cond, mod_w, mod_b):
    depth, d, n = mod_w.shape
    tn = 1024
    return pl.pallas_call(
        _mod_kernel,
        grid=(depth, n // tn),
        in_specs=[pl.BlockSpec((8, d), lambda l, j: (0, 0)),
                  pl.BlockSpec((1, d, tn), lambda l, j: (l, 0, j)),
                  pl.BlockSpec((1, 1, tn), lambda l, j: (l, 0, j))],
        out_specs=pl.BlockSpec((1, 8, tn), lambda l, j: (l, 0, j)),
        out_shape=jax.ShapeDtypeStruct((depth, 8, n), F32),
        compiler_params=_cparams("arbitrary", "arbitrary"),
    )(cond, mod_w, mod_b.reshape(depth, 1, n))


def _residual_in(x_ref, comb_refs, mv):
    x = x_ref[...]
    if comb_refs:
        ya_ref, yb_ref, gt_ref = comb_refs
        gt = gt_ref[...]
        x = x + mv[6:7] * (gt[:, 0:1] * ya_ref[...].astype(F32) + gt[:, 1:2] * yb_ref[...].astype(F32))
    return x


def _route(x2, wr_hi_ref, wr_lo_ref, rb_ref, tri_ref, ids_ref, gts_ref, cnt_ref):
    tm = x2.shape[0]
    x_hi = x2.astype(BF16)
    x_lo = (x2 - x_hi.astype(F32)).astype(BF16)
    w_hi = wr_hi_ref[...]
    lg = _nt_dot(w_hi, x_hi) + _nt_dot(wr_lo_ref[...], x_hi) + _nt_dot(w_hi, x_lo)
    lg = lg + rb_ref[...]
    row = lax.broadcasted_iota(jnp.int32, (EPG, tm), 0)
    gl = lg[N_EXPERTS:N_EXPERTS + EPG]
    gl = jnp.where(row < N_GROUPS, gl, -jnp.inf)
    gmx = jnp.max(gl, axis=0, keepdims=True)
    grp = jnp.min(jnp.where(gl == gmx, row, EPG), axis=0, keepdims=True)
    p_grp = 1.0 / jnp.sum(jnp.exp(gl - gmx), axis=0, keepdims=True)
    sel = lg[0:EPG]
    for g in range(1, N_GROUPS):
        sel = jnp.where(grp == g, lg[g * EPG:(g + 1) * EPG], sel)
    mx1 = jnp.max(sel, axis=0, keepdims=True)
    i1 = jnp.min(jnp.where(sel == mx1, row, EPG), axis=0, keepdims=True)
    rest = jnp.where(row == i1, -jnp.inf, sel)
    mx2 = jnp.max(rest, axis=0, keepdims=True)
    i2 = jnp.min(jnp.where(rest == mx2, row, EPG), axis=0, keepdims=True)
    e2 = jnp.exp(mx2 - mx1)
    inv = p_grp / (1.0 + e2)
    eid1 = grp * EPG + i1
    eid2 = grp * EPG + i2
    gts_ref[0:1, :] = inv
    gts_ref[1:2, :] = inv * e2
    erow = lax.broadcasted_iota(jnp.int32, (N_EXPERTS, tm), 0)
    oh1 = erow == eid1
    oh2 = erow == eid2
    member = jnp.where(oh1, 1.0, jnp.where(oh2, 1.0, 0.0)).astype(BF16)
    before = jnp.dot(member, tri_ref[...], preferred_element_type=F32)
    cnt = jnp.dot(member, jnp.ones((tm, LANES), BF16), preferred_element_type=F32)
    cnt_ref[0] = cnt
    ids_ref[0:1, :] = eid1
    ids_ref[1:2, :] = eid2
    ids_ref[2:3, :] = jnp.sum(jnp.where(oh1, before, 0.0), axis=0, keepdims=True).astype(jnp.int32)
    ids_ref[3:4, :] = jnp.sum(jnp.where(oh2, before, 0.0), axis=0, keepdims=True).astype(jnp.int32)
    cnt_pad = jnp.floor((cnt + (SORT_PAD - 1)) * (1.0 / SORT_PAD)) * SORT_PAD
    er = lax.broadcasted_iota(jnp.int32, (N_EXPERTS, N_EXPERTS), 0)
    ec = lax.broadcasted_iota(jnp.int32, (N_EXPERTS, N_EXPERTS), 1)
    lower = jnp.where(ec < er, 1.0, 0.0).astype(BF16)
    first = jnp.dot(lower, cnt_pad.astype(BF16), preferred_element_type=F32)
    place = before + jnp.concatenate([first] * (tm // LANES), axis=1)
    pos1 = jnp.sum(jnp.where(oh1, place, 0.0), axis=0, keepdims=True).astype(jnp.int32)
    pos2 = jnp.sum(jnp.where(oh2, place, 0.0), axis=0, keepdims=True).astype(jnp.int32)
    return pos1, pos2, x_hi


def _sort_rows(pos1, pos2, x2b, xs_ref):
    tm = x2b.shape[0]
    for r0 in range(0, xs_ref.shape[0], SORT_CHUNK):
        r = lax.broadcasted_iota(jnp.int32, (SORT_CHUNK, tm), 0) + r0
        perm = jnp.where(r == pos1, 1.0, jnp.where(r == pos2, 1.0, 0.0)).astype(BF16)
        xs_ref[r0:r0 + SORT_CHUNK, :] = jnp.dot(perm, x2b, preferred_element_type=F32).astype(BF16)


def _norm2(x, mv, ng_ref):
    return _rms(x, ng_ref[1:2]) * (1.0 + mv[4:5]) + mv[3:4]


def _norm2_and_route(xg, mv, ng_ref, route_refs, out_refs):
    xo_ref, xs_ref, ids_ref, gts_ref, cnt_ref = out_refs
    x2g = [_norm2(x, mv, ng_ref) for x in xg]
    xo_ref[...] = jnp.concatenate(xg, axis=0) if len(xg) > 1 else xg[0]
    x2 = jnp.concatenate(x2g, axis=0) if len(x2g) > 1 else x2g[0]
    _sort_rows(*_route(x2, *route_refs, ids_ref, gts_ref, cnt_ref), xs_ref)


def _full(shape):
    return pl.BlockSpec(shape, lambda i: (0,) * len(shape))


def _tile_specs(tm, d, has_comb, tiles_per_batch):
    tok = pl.BlockSpec((tm, d), lambda i: (i, 0))
    specs = [tok]
    if has_comb:
        specs += [tok, tok, pl.BlockSpec((tm, 2), lambda i: (i, 0))]
    specs.append(pl.BlockSpec((None, 8, d), lambda i: (i // tiles_per_batch, 0, 0)))
    specs.append(_full((2, d)))
    return specs


def _route_specs(d, tm):
    return [_full((ROUTE_ROWS, d)), _full((ROUTE_ROWS, d)), _full((ROUTE_ROWS, 1)), _full((tm, tm))]


def _sorted_rows(tm):
    return 2 * tm + N_EXPERTS * SORT_PAD


def _mixer_out(t, d, tm, h2_rows, h2_row0):
    sr = _sorted_rows(tm)
    assert h2_row0 % sr == 0
    off = h2_row0 // sr
    tok = pl.BlockSpec((tm, d), lambda i: (i, 0))
    specs = [tok, pl.BlockSpec((sr, d), lambda i: (i + off, 0)),
             pl.BlockSpec((4, tm), lambda i: (0, i)), pl.BlockSpec((2, tm), lambda i: (0, i)),
             pl.BlockSpec((1, N_EXPERTS, LANES), lambda i: (i, 0, 0))]
    shapes = [jax.ShapeDtypeStruct((t, d), F32), jax.ShapeDtypeStruct((h2_rows, d), BF16),
              jax.ShapeDtypeStruct((4, t), jnp.int32), jax.ShapeDtypeStruct((2, t), F32),
              jax.ShapeDtypeStruct((t // tm, N_EXPERTS, LANES), F32)]
    return specs, shapes


def _in_a_kernel(*refs, has_comb):
    n_in = 4 if has_comb else 1
    x_ref, comb, refs = refs[0], refs[1:n_in], refs[n_in:]
    mv_ref, ng_ref, wqt_ref, wk_ref, wvt_ref, wot_ref, wgt_ref, bg_ref = refs[:8]
    outs = refs[8:]
    if has_comb:
        xo_ref, outs = outs[0], outs[1:]
    qt_ref, k_ref, vt_ref, ot_ref, a_ref = outs
    mv = mv_ref[...]
    x = _residual_in(x_ref, comb, mv)
    if has_comb:
        xo_ref[...] = x
    tm = x.shape[0]
    gm = min(tm, 256)
    for r in range(tm // gm):
        tok = slice(r * gm, (r + 1) * gm)
        h = (_rms(x[tok], ng_ref[0:1]) * (1.0 + mv[1:2]) + mv[0:1]).astype(BF16)
        qt_ref[0, :, tok] = (_nt_dot(wqt_ref[...], h) * (DQK ** -0.5)).astype(BF16)
        k_ref[tok, :] = jnp.dot(h, wk_ref[...], preferred_element_type=F32).astype(BF16)
        vt_ref[0, :, tok] = _nt_dot(wvt_ref[...], h).astype(BF16)
        ot_ref[0, :, tok] = jax.nn.sigmoid(_nt_dot(wot_ref[...], h)).astype(BF16)
        g = _nt_dot(wgt_ref[...], h) + bg_ref[...]
        gi, gf = g[0:8], g[8:16]
        a_ref[0, 0:8, tok] = IGATE_SOFTCAP * jnp.tanh(gi / IGATE_SOFTCAP)
        a_ref[0, 8:16, tok] = jnp.minimum(gf, 0.0) - jnp.log(1.0 + jnp.exp(-jnp.abs(gf)))


def _in_a(x, comb, modv, ng, w, batch, tm):
    t, d = x.shape
    s = t // batch
    tpb = s // tm
    has_comb = bool(comb)
    in_specs = _tile_specs(tm, d, has_comb, tpb) + [_full(a.shape) for a in w]
    tok = lambda n: pl.BlockSpec((tm, n), lambda i: (i, 0))
    rows = lambda n: pl.BlockSpec((1, n, tm), lambda i: (i // tpb, 0, i % tpb))
    out_specs = [rows(NQ), tok(NQ), rows(NV), rows(d), rows(16)]
    out_shape = [jax.ShapeDtypeStruct((batch, NQ, s), BF16), jax.ShapeDtypeStruct((t, NQ), BF16),
                 jax.ShapeDtypeStruct((batch, NV, s), BF16), jax.ShapeDtypeStruct((batch, d, s), BF16),
                 jax.ShapeDtypeStruct((batch, 16, s), F32)]
    if has_comb:
        out_specs = [tok(d)] + out_specs
        out_shape = [jax.ShapeDtypeStruct((t, d), F32)] + out_shape
    outs = pl.pallas_call(
        functools.partial(_in_a_kernel, has_comb=has_comb),
        grid=(t // tm,), in_specs=in_specs, out_specs=out_specs, out_shape=out_shape,
        compiler_params=_cparams("parallel"),
    )(x, *comb, modv, ng, *w)
    if has_comb:
        return outs[0], outs[1:]
    return x, outs


def _chunk_scan(x, op, fill, forward):
    length = x.shape[-1]
    pos = lax.broadcasted_iota(jnp.int32, x.shape, 1)
    s = 1
    while s < length:
        if forward:
            x = op(x, jnp.where(pos >= s, pltpu.roll(x, s, axis=1), fill))
        else:
            x = op(x, jnp.where(pos < length - s, pltpu.roll(x, length - s, axis=1), fill))
        s *= 2
    return x


def _gate_kernel(a_ref, o_ref):
    for r in range(2 * HEADS):
        fwd = r < HEADS
        b = _chunk_scan(a_ref[0, 8 + r], jnp.add, 0.0, fwd)
        u = a_ref[0, r] - b
        cm = _chunk_scan(u, jnp.maximum, -jnp.inf, fwd)
        end = b.shape[-1] - 1 if fwd else 0
        o_ref[0, r] = u
        o_ref[0, 8 + r] = b
        o_ref[0, 16 + r] = cm
        o_ref[0, 24 + r] = jnp.broadcast_to(cm[:, end:end + 1], cm.shape)
        o_ref[0, 32 + r] = jnp.broadcast_to(b[:, end:end + 1], b.shape)


GATE_ROWS = 40


def _gate_scans(act, chunk):
    batch, _, s = act.shape
    nc = s // chunk
    g = pl.pallas_call(
        _gate_kernel, grid=(batch,),
        in_specs=[pl.BlockSpec((1, 16, nc, chunk), lambda b: (b, 0, 0, 0))],
        out_specs=pl.BlockSpec((1, GATE_ROWS, nc, chunk), lambda b: (b, 0, 0, 0)),
        out_shape=jax.ShapeDtypeStruct((batch, GATE_ROWS, nc, chunk), F32),
        compiler_params=_cparams("parallel"),
    )(act.reshape(batch, 16, nc, chunk))
    rows = jnp.transpose(g, (0, 2, 1, 3))
    cols = jnp.transpose(g[:, 0:8], (0, 2, 3, 1))
    return rows, cols


def _mlstm_kernel(qf_ref, kf_ref, vf_ref, rf_ref, cf_ref, qb_ref, kb_ref, vb_ref, rb_ref, cb_ref,
                  c0_ref, n0_ref, m0_ref, hf_ref, hb_ref, ct_ref, nt_ref, mt_ref,
                  c_s, n_s, m_s, *, chunk):
    step = pl.program_id(1)
    last = pl.num_programs(1) - 1

    @pl.when(step == 0)
    def _():
        c_s[...] = c0_ref[0]
        n_s[...] = n0_ref[0]
        m_s[...] = m0_ref[0]

    si = lax.broadcasted_iota(jnp.int32, (chunk, chunk), 0)
    ji = lax.broadcasted_iota(jnp.int32, (chunk, chunk), 1)
    sub = qf_ref.shape[2] // chunk
    dirs = ((qf_ref, kf_ref, vf_ref, rf_ref, cf_ref, hf_ref),
            (qb_ref, kb_ref, vb_ref, rb_ref, cb_ref, hb_ref))
    units = [(j if d == 0 else sub - 1 - j, d, h, dirs[d])
             for j in range(sub) for d in range(2) for h in range(HEADS)]
    per_sub = 2 * HEADS
    for u0, (c, d, h, (q_ref, k_ref, v_ref, r_ref, c_ref, h_ref)) in enumerate(units):
        if u0 % per_sub == 0:
            scores = []
            for c2, d2, h2, (q2_ref, k2_ref, _, _, _, _) in units[u0:u0 + per_sub]:
                qt2 = q2_ref[0, h2 * DQK:(h2 + 1) * DQK, c2 * chunk:(c2 + 1) * chunk]
                k2 = k2_ref[c2 * chunk:(c2 + 1) * chunk, h2 * DQK:(h2 + 1) * DQK]
                n16 = jnp.broadcast_to(n_s[d2 * HEADS + h2], (16, DQK)).astype(BF16)
                scores.append(jnp.dot(jnp.concatenate([k2, n16], axis=0), qt2,
                                      preferred_element_type=F32))
        sx = scores[u0 % per_sub]
        mask = (si <= ji) if d == 0 else (si >= ji)
        idx = d * HEADS + h
        lanes = slice(c * chunk, (c + 1) * chunk)
        qt = q_ref[0, h * DQK:(h + 1) * DQK, lanes]
        k = k_ref[c * chunk:(c + 1) * chunk, h * DQK:(h + 1) * DQK]
        vt = v_ref[0, h * DV:(h + 1) * DV, lanes]
        u_col = c_ref[0, c, :, idx:idx + 1]
        u_row = r_ref[0, c, idx:idx + 1, :]
        b_row = r_ref[0, c, 8 + idx:9 + idx, :]
        cm_row = r_ref[0, c, 16 + idx:17 + idx, :]
        cm_end = r_ref[0, c, 24 + idx:25 + idx, :]
        b_end = r_ref[0, c, 32 + idx:33 + idx, :]
        ct_st = c_s[idx]
        n_st = n_s[idx]
        m_st = m_s[idx]
        mm = jnp.maximum(m_st, cm_row)
        dt = jnp.where(mask, jnp.exp(u_col - mm), 0.0)
        pt = sx[:chunk] * dt
        a_int = jnp.exp(m_st - mm)
        den = jnp.sum(pt, axis=0, keepdims=True) + a_int * sx[chunk:chunk + 1]
        inv = 1.0 / jnp.maximum(jnp.abs(den), jnp.exp(-b_row - mm))
        wts = jnp.concatenate([(pt * inv).astype(BF16),
                               (qt.astype(F32) * (a_int * inv)).astype(BF16)], axis=0)
        vals = jnp.concatenate([vt, ct_st.astype(BF16)], axis=1)
        h_ref[0, h * DV:(h + 1) * DV, lanes] = jnp.dot(vals, wts, preferred_element_type=F32)
        mm_end = jnp.maximum(m_st, cm_end)
        a_end = jnp.exp(u_row - mm_end)
        decay = jnp.exp(m_st - mm_end)
        upd = jnp.concatenate([(vt.astype(F32) * a_end).astype(BF16),
                               jnp.broadcast_to(a_end, (16, chunk)).astype(BF16)], axis=0)
        upd = jnp.dot(upd, k, preferred_element_type=F32)
        c_s[idx] = decay * ct_st + upd[:DV]
        n_s[idx] = decay * n_st + upd[DV:DV + 1]
        m_s[idx] = b_end + mm_end

    @pl.when(step == last)
    def _():
        ct_ref[0] = c_s[...]
        nt_ref[0] = n_s[...]
        mt_ref[0] = m_s[...]


def _mlstm_zero_state(batch, chunk):
    return (jnp.zeros((batch, 8, DV, DQK), F32), jnp.zeros((batch, 8, 1, DQK), F32),
            jnp.zeros((batch, 8, 1, chunk), F32))


def _mlstm(qt, k, vt, rows, cols, state, batch, chunk):
    t = k.shape[0]
    s = t // batch
    nc = s // chunk
    assert chunk == DQK
    c0, n0, m0 = state
    sub = MLSTM_SUB if nc % MLSTM_SUB == 0 else 1
    span = sub * chunk
    ns = nc // sub

    def specs(cidx):
        return [pl.BlockSpec((1, NQ, span), lambda b, c: (b, 0, cidx(c))),
                pl.BlockSpec((span, NQ), lambda b, c: (b * ns + cidx(c), 0)),
                pl.BlockSpec((1, NV, span), lambda b, c: (b, 0, cidx(c))),
                pl.BlockSpec((1, sub, GATE_ROWS, chunk), lambda b, c: (b, cidx(c), 0, 0)),
                pl.BlockSpec((1, sub, chunk, 8), lambda b, c: (b, cidx(c), 0, 0))]

    fwd = lambda c: c
    bwd = lambda c: ns - 1 - c
    st_specs = [pl.BlockSpec((1, 8, DV, DQK), lambda b, c: (b, 0, 0, 0)),
                pl.BlockSpec((1, 8, 1, DQK), lambda b, c: (b, 0, 0, 0)),
                pl.BlockSpec((1, 8, 1, chunk), lambda b, c: (b, 0, 0, 0))]
    st_shape = [jax.ShapeDtypeStruct(a.shape, F32) for a in state]
    h_shape = jax.ShapeDtypeStruct((batch, NV, s), F32)
    args = (qt, k, vt, rows, cols)
    hf, hb, ct, nt, mt = pl.pallas_call(
        functools.partial(_mlstm_kernel, chunk=chunk),
        grid=(batch, ns),
        in_specs=specs(fwd) + specs(bwd) + st_specs,
        out_specs=[pl.BlockSpec((1, NV, span), lambda b, c: (b, 0, fwd(c))),
                   pl.BlockSpec((1, NV, span), lambda b, c: (b, 0, bwd(c)))] + st_specs,
        out_shape=[h_shape, h_shape] + st_shape,
        scratch_shapes=[pltpu.VMEM((8, DV, DQK), F32), pltpu.VMEM((8, 1, DQK), F32),
                        pltpu.VMEM((8, 1, chunk), F32)],
        compiler_params=_cparams("arbitrary", "arbitrary"),
    )(*args, *args, c0, n0, m0)
    return hf, hb, (ct, nt, mt)


def _out_a_kernel(*refs, aliased):
    hf_ref, hb_ref, o_ref, x_ref, mv_ref, ng_ref, hg_ref, wout_ref = refs[:8]
    route_refs = refs[8:12]
    out_refs = refs[12 + int(aliased):]
    mv = mv_ref[...]
    hsum = hf_ref[0] + hb_ref[0]
    parts = []
    for h in range(HEADS):
        hh = hsum[h * DV:(h + 1) * DV]
        parts.append(hh * lax.rsqrt(jnp.mean(hh * hh, axis=0, keepdims=True) + EPS))
    hn = jnp.concatenate(parts, axis=0) * hg_ref[...] * o_ref[0].astype(F32)
    y = jnp.dot(wout_ref[...], hn.astype(BF16), preferred_element_type=F32).T
    x = x_ref[...] + mv[2:3] * y
    _norm2_and_route([x], mv, ng_ref, route_refs, out_refs)


def _out_a(hf, hb, o, x, modv, ng, head_g, w_out_t, route_w, batch, tm, h2_buf, h2_rows, h2_row0):
    t, d = x.shape
    tpb = t // batch // tm
    rows = lambda n: pl.BlockSpec((1, n, tm), lambda i: (i // tpb, 0, i % tpb))
    in_specs = ([rows(NV), rows(NV), rows(d)] + _tile_specs(tm, d, False, tpb)
                + [_full((NV, 1)), _full(w_out_t.shape)] + _route_specs(d, tm))
    w_out = w_out_t
    args = [hf, hb, o, x, modv, ng, head_g, w_out, *route_w]
    return _mixer_call(functools.partial(_out_a_kernel, aliased=h2_buf is not None),
                       in_specs, args, t, d, tm, h2_buf, h2_rows, h2_row0)


def _mixer_call(body, in_specs, args, t, d, tm, h2_buf, h2_rows, h2_row0):
    out_specs, out_shape = _mixer_out(t, d, tm, h2_rows, h2_row0)
    aliases = {}
    if h2_buf is not None:
        aliases = {len(args): 1}
        in_specs = in_specs + [pl.BlockSpec(memory_space=pl.ANY)]
        args = args + [h2_buf]
    return pl.pallas_call(
        body, grid=(t // tm,), in_specs=in_specs, out_specs=out_specs, out_shape=out_shape,
        input_output_aliases=aliases, compiler_params=_cparams("parallel"),
    )(*args)


def _conv_kernel(*refs, has_comb, row_w, aliased):
    n_in = 4 if has_comb else 1
    x_ref, comb, refs = refs[0], refs[1:n_in], refs[n_in:]
    mv_ref, ng_ref, win_ref, cw_ref, wout_ref = refs[:5]
    route_refs = refs[5:9]
    out_refs = refs[9 + int(aliased):]
    mv = mv_ref[...]
    x = _residual_in(x_ref, comb, mv)
    d = x.shape[1]
    tm = x.shape[0]
    cw = cw_ref[...]
    groups = 4 if (tm // 4) % row_w == 0 else 1
    gm = tm // groups
    xg = [x[g * gm:(g + 1) * gm] for g in range(groups)]
    hg = [(_rms(a, ng_ref[0:1]) * (1.0 + mv[1:2]) + mv[0:1]).astype(BF16) for a in xg]
    pos = lax.broadcasted_iota(jnp.int32, (gm, d), 0) % row_w

    def in_proj(h):
        return (jnp.dot(h, win_ref[:, :d], preferred_element_type=F32),
                jnp.dot(h, win_ref[:, d:2 * d], preferred_element_type=F32),
                jnp.dot(h, win_ref[:, 2 * d:], preferred_element_type=F32))

    def gate(p):
        bg, cg, u = p
        z = cg * u
        left = jnp.where(pos == 0, 0.0, pltpu.roll(z, 1, axis=0))
        right = jnp.where(pos == row_w - 1, 0.0, pltpu.roll(z, gm - 1, axis=0))
        return (bg * (cw[0:1] * left + cw[1:2] * z + cw[2:3] * right)).astype(BF16)

    pg = [in_proj(h) for h in hg]
    outs = []
    for g in range(groups):
        yl = jnp.dot(gate(pg[g]), wout_ref[...], preferred_element_type=F32)
        outs.append(xg[g] + mv[2:3] * yl)
    _norm2_and_route(outs, mv, ng_ref, route_refs, out_refs)


def _conv_layer(x, comb, modv, ng, w_in, conv_w, w_out, route_w, batch, tm, row_w, h2_buf, h2_rows, h2_row0):
    t, d = x.shape
    tpb = t // batch // tm
    in_specs = (_tile_specs(tm, d, bool(comb), tpb)
                + [_full(w_in.shape), _full(conv_w.shape), _full(w_out.shape)] + _route_specs(d, tm))
    args = [x, *comb, modv, ng, w_in, conv_w, w_out, *route_w]
    body = functools.partial(_conv_kernel, has_comb=bool(comb), row_w=row_w, aliased=h2_buf is not None)
    return _mixer_call(body, in_specs, args, t, d, tm, h2_buf, h2_rows, h2_row0)


def _moe_kernel(be_ref, nb_ref, src_ref, nxt_ref, xs_ref, wg_ref, wu_ref, wd_ref, y_ref,
                xbuf, sem, wst_g, wst_u, wst_d, wsem, wslot, wg_s, wu_s, wd_s):
    i = pl.program_id(0)
    nb = nb_ref[0]
    n_slots = GATHER_AHEAD + 1

    def weight_copies(e, slot):
        return [pltpu.make_async_copy(w_ref.at[e], st.at[slot], wsem.at[slot, j])
                for j, (w_ref, st) in enumerate(((wg_ref, wst_g), (wu_ref, wst_u), (wd_ref, wst_d)))]

    def piece_copies(blk, slot):
        copies = []
        for j in range(PIECES):
            src = pl.multiple_of(src_ref[blk * PIECES + j], SORT_PAD)
            copies.append(pltpu.make_async_copy(
                xs_ref.at[pl.ds(src, SORT_PAD), :],
                xbuf.at[slot, pl.ds(j * SORT_PAD, SORT_PAD), :], sem.at[slot]))
        return copies

    def wait_slot(slot):
        pltpu.make_async_copy(xs_ref.at[pl.ds(0, MOE_BLOCK), :], xbuf.at[slot], sem.at[slot]).wait()

    @pl.when(i < nb)
    def _():
        slot = lax.rem(i, n_slots)

        for b in range(GATHER_AHEAD):
            @pl.when((i == 0) & (b < nb))
            def _(b=b):
                for c in piece_copies(b, b):
                    c.start()

        @pl.when(i + GATHER_AHEAD < nb)
        def _():
            for c in piece_copies(i + GATHER_AHEAD, lax.rem(i + GATHER_AHEAD, n_slots)):
                c.start()

        @pl.when(i == 0)
        def _():
            wslot[0] = 1
            for c in weight_copies(be_ref[0], 0):
                c.start()

        prev = be_ref[jnp.maximum(i - 1, 0)]

        @pl.when((i == 0) | (be_ref[i] != prev))
        def _():
            ws = 1 - wslot[0]
            wslot[0] = ws
            for c in weight_copies(be_ref[i], ws):
                c.wait()
            wg_s[...] = wst_g[ws].astype(BF16)
            wu_s[...] = wst_u[ws].astype(BF16)
            wd_s[...] = wst_d[ws].astype(BF16)

            @pl.when(nxt_ref[i] >= 0)
            def _():
                for c in weight_copies(nxt_ref[i], 1 - ws):
                    c.start()

        wait_slot(slot)
        gm = MOE_BLOCK // MOE_GROUPS
        gu = []
        for r in range(MOE_GROUPS):
            x = xbuf[slot, r * gm:(r + 1) * gm, :]
            gu.append((jnp.dot(x, wg_s[...], preferred_element_type=F32),
                       jnp.dot(x, wu_s[...], preferred_element_type=F32)))
        for r, (g, u) in enumerate(gu):
            a = (g * jax.nn.sigmoid(g) * u).astype(BF16)
            y_ref[r * gm:(r + 1) * gm, :] = jnp.dot(a, wd_s[...], preferred_element_type=F32).astype(BF16)


def _moe_experts(xs, blk_e, nb_used, piece_src, next_e, w_gate, w_up, w_down):
    d = xs.shape[1]
    de = w_gate.shape[-1]
    n_blocks = blk_e.shape[0]
    hbm = pl.BlockSpec(memory_space=pl.ANY)
    grid_spec = pltpu.PrefetchScalarGridSpec(
        num_scalar_prefetch=4, grid=(n_blocks,),
        in_specs=[hbm, hbm, hbm, hbm],
        out_specs=pl.BlockSpec((MOE_BLOCK, d), lambda i, be, nb, src, nxt: (jnp.minimum(i, nb[0] - 1), 0)),
        scratch_shapes=[pltpu.VMEM((GATHER_AHEAD + 1, MOE_BLOCK, d), BF16),
                        pltpu.SemaphoreType.DMA((GATHER_AHEAD + 1,)),
                        pltpu.VMEM((2, d, de), F32), pltpu.VMEM((2, d, de), F32),
                        pltpu.VMEM((2, de, d), F32), pltpu.SemaphoreType.DMA((2, 3)),
                        pltpu.SMEM((1,), jnp.int32),
                        pltpu.VMEM((d, de), BF16), pltpu.VMEM((d, de), BF16),
                        pltpu.VMEM((de, d), BF16)])
    return pl.pallas_call(
        _moe_kernel, grid_spec=grid_spec,
        out_shape=jax.ShapeDtypeStruct((n_blocks * MOE_BLOCK, d), BF16),
        compiler_params=_cparams("arbitrary"),
    )(blk_e, nb_used, piece_src, next_e, xs, w_gate, w_up, w_down)


def _dest_slots(ids, base, tm):
    t = ids.shape[1]
    nt = t // tm
    eid = ids[0:2].reshape(2, nt, tm)
    rank = ids[2:4].reshape(2, nt, tm)
    onehot = eid[..., None] == jnp.arange(N_EXPERTS, dtype=jnp.int32)
    off = jnp.sum(jnp.where(onehot, base[None, :, None, :], 0), axis=-1)
    return (rank + off).reshape(2, t)


def _moe(xs, routed, layer, w_gate, w_up, w_down):
    cnt_tiles = jnp.concatenate([cnt[:, :, 0] for _, cnt, _ in routed], axis=0).astype(jnp.int32)
    cnt_pad = (cnt_tiles + SORT_PAD - 1) // SORT_PAD * SORT_PAD
    run_first = jnp.cumsum(cnt_pad, axis=0) - cnt_pad
    in_tile = jnp.cumsum(cnt_pad, axis=1) - cnt_pad
    region = jnp.sum(cnt_pad, axis=0)
    padded = (region + MOE_BLOCK - 1) // MOE_BLOCK * MOE_BLOCK
    pends = jnp.cumsum(padded)
    pstart = pends - padded
    base = pstart[None, :] + run_first
    dests, tile_row0, row, xrow, worst = [], [], 0, 0, 0
    for ids, cnt, tm in routed:
        nt = cnt.shape[0]
        dests.append(_dest_slots(ids, base[row:row + nt], tm))
        tile_row0.append(xrow + jnp.arange(nt, dtype=jnp.int32) * _sorted_rows(tm))
        row += nt
        xrow += nt * _sorted_rows(tm)
        worst += nt * (2 * tm + N_EXPERTS * (SORT_PAD - 1))
    tile_row0 = jnp.concatenate(tile_row0)
    n_blocks = -(-(worst + N_EXPERTS * (MOE_BLOCK - 1)) // MOE_BLOCK)
    blk_start = jnp.arange(n_blocks, dtype=jnp.int32) * MOE_BLOCK
    blk_x = jnp.minimum(jnp.sum((pends[None, :] <= blk_start[:, None]).astype(jnp.int32), axis=1),
                        N_EXPERTS - 1)
    nb_used = (pends[-1:] // MOE_BLOCK).astype(jnp.int32)
    blk_oh = blk_x[:, None] == jnp.arange(N_EXPERTS, dtype=jnp.int32)[None, :]
    pick = lambda tab: jnp.sum(jnp.where(blk_oh[:, None, :], tab[None], 0), axis=2)
    per_piece = lambda a: jnp.repeat(a, PIECES, axis=0)
    blk_first = per_piece(pick(run_first))
    blk_src0 = per_piece(pick(in_tile - run_first) + tile_row0[None, :])
    blk_scal = per_piece(pick(jnp.stack([pstart, region])))
    piece_rank = jnp.arange(n_blocks * PIECES, dtype=jnp.int32) * SORT_PAD - blk_scal[:, 0]
    piece_tile = jnp.sum((blk_first <= piece_rank[:, None]).astype(jnp.int32), axis=1) - 1
    tile_oh = piece_tile[:, None] == jnp.arange(blk_first.shape[1], dtype=jnp.int32)[None, :]
    src = jnp.sum(jnp.where(tile_oh, blk_src0, 0), axis=1) + piece_rank
    piece_src = jnp.where(piece_rank < blk_scal[:, 1], src, 0).astype(jnp.int32)
    e_iota = jnp.arange(N_EXPERTS, dtype=jnp.int32)
    later = (e_iota[None, :] > e_iota[:, None]) & (padded[None, :] > 0)
    next_exp = jnp.min(jnp.where(later, e_iota[None, :], N_EXPERTS), axis=1)
    next_exp = jnp.where(next_exp < N_EXPERTS, next_exp + layer * N_EXPERTS, -1)
    next_e = jnp.sum(jnp.where(blk_oh, next_exp[None, :], 0), axis=1).astype(jnp.int32)
    yb = _moe_experts(xs, blk_x + layer * N_EXPERTS, nb_used, piece_src, next_e, w_gate, w_up, w_down)
    return [(yb.at[d[0]].get(mode='promise_in_bounds'), yb.at[d[1]].get(mode='promise_in_bounds'))
            for d in dests]


def _final_kernel(x_ref, ya_ref, yb_ref, gt_ref, mv_ref, g_ref, o_ref):
    x = _residual_in(x_ref, (ya_ref, yb_ref, gt_ref), mv_ref[...])
    o_ref[...] = _rms(x, g_ref[...])


def _final(x, comb, modv, g, batch, tm):
    t, d = x.shape
    tpb = t // batch // tm
    tok = pl.BlockSpec((tm, d), lambda i: (i, 0))
    return pl.pallas_call(
        _final_kernel, grid=(t // tm,),
        in_specs=[tok, tok, tok, pl.BlockSpec((tm, 2), lambda i: (i, 0)),
                  pl.BlockSpec((None, 8, d), lambda i: (i // tpb, 0, 0)), _full((1, d))],
        out_specs=tok, out_shape=jax.ShapeDtypeStruct((t, d), F32),
        compiler_params=_cparams("parallel"),
    )(x, *comb, modv, g)


def _prep_a(w_in, b_gate):
    d = w_in.shape[0]
    wqt = w_in[:, :NQ].T.astype(BF16)
    wk = w_in[:, NQ:2 * NQ].astype(BF16)
    wvt = w_in[:, 2 * NQ:2 * NQ + NV].T.astype(BF16)
    wot = w_in[:, 2 * NQ + NV:2 * NQ + NV + d].T.astype(BF16)
    perm = jnp.array([0, 1, 2, 3, 8, 9, 10, 11, 4, 5, 6, 7, 12, 13, 14, 15], jnp.int32)
    wgt = w_in[:, 2 * NQ + NV + d:].T[perm].astype(BF16)
    bg = b_gate.astype(F32)[perm][:, None]
    return wqt, wk, wvt, wot, wgt, bg


def _prep_route(w_group, b_group, w_router, b_router, tm):
    d = w_group.shape[0]
    pad = ROUTE_ROWS - N_EXPERTS - N_GROUPS
    wt = jnp.concatenate([w_router.T, w_group.T, jnp.zeros((pad, d), F32)], axis=0).astype(F32)
    hi = wt.astype(BF16)
    lo = (wt - hi.astype(F32)).astype(BF16)
    rb = jnp.concatenate([b_router, b_group, jnp.zeros((pad,), F32)]).astype(F32)[:, None]
    tri = jnp.triu(jnp.ones((tm, tm), BF16), k=1)
    return hi, lo, rb, tri


def _modv(mod, l, rows):
    depth = mod.shape[0]
    d = mod.shape[-1] // 6
    zero = jnp.zeros((len(rows), 1, d), F32)
    cur = jnp.stack([mod[l, r].reshape(6, d) for r in rows]) if l < depth else jnp.zeros((len(rows), 6, d), F32)
    prev = jnp.stack([mod[l - 1, r].reshape(6, d)[5:6] for r in rows]) if l > 0 else zero
    return jnp.concatenate([cur, prev, zero], axis=1)


def kernel(x, c, ctx, c_ctx, mod_w, mod_b, norm_g, final_g, a_w_in, a_b_gate, a_head_g, a_w_out,
           b_w_in, b_conv_w, b_w_out, moe_w_group, moe_b_group, moe_w_router, moe_b_router,
           moe_w_gate, moe_w_up, moe_w_down):
    batch, seq, d = x.shape
    n_ctx = ctx.shape[1]
    depth = mod_w.shape[0]
    assert batch + 1 <= 8 and seq % MLSTM_CHUNK == 0 and n_ctx % MLSTM_CHUNK == 0
    tm = min(512, seq)
    t_lat = batch * seq
    t_ctx = batch * n_ctx
    assert t_lat % n_ctx == 0 and seq % GRID_W == 0

    cond = jnp.concatenate([c, c_ctx[None, :], jnp.zeros((8 - batch - 1, d), F32)], axis=0)
    mod = _modulation(cond, mod_w, mod_b)
    de = moe_w_gate.shape[-1]
    w_gate = moe_w_gate.reshape(depth * N_EXPERTS, d, de)
    w_up = moe_w_up.reshape(depth * N_EXPERTS, d, de)
    w_down = moe_w_down.reshape(depth * N_EXPERTS, de, d)

    lat = x.reshape(t_lat, d)
    cx = ctx.reshape(t_ctx, d)
    comb_lat, comb_ctx = (), ()
    for l in range(depth):
        kind, j = l % 2, l // 2
        ctx_after = any(i % 2 == 0 for i in range(l + 1, depth))
        mv_lat = _modv(mod, l, list(range(batch)))
        mv_ctx = _modv(mod, l, [batch] * batch)
        route_args = (moe_w_group[l], moe_b_group[l], moe_w_router[l], moe_b_router[l])
        route_lat = _prep_route(*route_args, tm)
        route_ctx = _prep_route(*route_args, n_ctx)
        lat_rows = t_lat // tm * _sorted_rows(tm)
        h2_rows = lat_rows + (t_ctx // n_ctx * _sorted_rows(n_ctx) if ctx_after else 0)
        if kind == 0:
            wa = _prep_a(a_w_in[j], a_b_gate[j])
            state = _mlstm_zero_state(batch, MLSTM_CHUNK)
            cx, (qtc, kc, vtc, oc, actc) = _in_a(cx, comb_ctx, mv_ctx, norm_g[l], wa, batch, n_ctx)
            hfc, hbc, state = _mlstm(qtc, kc, vtc, *_gate_scans(actc, MLSTM_CHUNK), state, batch, MLSTM_CHUNK)
            lat, (qtl, kl, vtl, ol, actl) = _in_a(lat, comb_lat, mv_lat, norm_g[l], wa, batch, tm)
            hfl, hbl, _ = _mlstm(qtl, kl, vtl, *_gate_scans(actl, MLSTM_CHUNK), state, batch, MLSTM_CHUNK)
            w_out = a_w_out[j].T.astype(BF16)
            head_g = a_head_g[j].astype(F32)[:, None]
            lat, h2, idl, gtl, cntl = _out_a(hfl, hbl, ol, lat, mv_lat, norm_g[l], head_g, w_out, route_lat,
                                             batch, tm, None, h2_rows, 0)
            if ctx_after:
                cx, h2, idc, gtc, cntc = _out_a(hfc, hbc, oc, cx, mv_ctx, norm_g[l], head_g, w_out, route_ctx,
                                                batch, n_ctx, h2, h2_rows, lat_rows)
        else:
            w_in = b_w_in[j].astype(BF16)
            w_out = b_w_out[j].astype(BF16)
            conv_w = b_conv_w[j].astype(F32)
            lat, h2, idl, gtl, cntl = _conv_layer(lat, comb_lat, mv_lat, norm_g[l], w_in, conv_w, w_out,
                                                  route_lat, batch, tm, GRID_W, None, h2_rows, 0)
            if ctx_after:
                cx, h2, idc, gtc, cntc = _conv_layer(cx, comb_ctx, mv_ctx, norm_g[l], w_in, conv_w, w_out,
                                                     route_ctx, batch, n_ctx, n_ctx, h2, h2_rows, lat_rows)
        routed = [(idl, cntl, tm)] + ([(idc, cntc, n_ctx)] if ctx_after else [])
        outs = _moe(h2, routed, l, w_gate, w_up, w_down)
        comb_lat = (*outs[0], gtl.T)
        comb_ctx = (*outs[1], gtc.T) if ctx_after else ()
    out = _final(lat, comb_lat, _modv(mod, depth, list(range(batch))), final_g.astype(F32)[None, :], batch, tm)
    return out.reshape(batch, seq, d)
```
